```python
import math
import jax
import jax.numpy as jnp
from jax import lax
import numpy as np

D_MODEL = 1024
BATCH = 2
SEQ = 16384
DEPTH = 2

GRID_W = 64
CTX_LEN = 256
NORM_EPS = 1e-6
LB_FLOOR = 1e-20
N_MOD = 9

DA_HEADS = 4
DA_HALF_DIM = 64
DA_V_DIM = 2 * DA_HALF_DIM
DA_QK_WIDTH = DA_HEADS * 2 * DA_HALF_DIM
DA_WIDTH = DA_HEADS * DA_V_DIM
ROPE_THETA = 10000.0
Q_BLOCK = 128

HG_HEADS = 4
HG_KEY_DIM = 64
HG_VAL_DIM = 64
HG_KEY_WIDTH = HG_HEADS * HG_KEY_DIM
HG_WIDTH = HG_HEADS * HG_VAL_DIM
HG_CHUNK = 64

POOL_WINDOWS = (2, 4, 8, 16)
POOL_GROUPS = 4
POOL_GROUP_DIM = 64
POOL_WIDTH = POOL_GROUPS * POOL_GROUP_DIM

N_BRANCH = 3
D_FF = 2816

IN_SIZES = (DA_QK_WIDTH, DA_QK_WIDTH, DA_WIDTH, HG_KEY_WIDTH, HG_KEY_WIDTH, HG_KEY_WIDTH, HG_WIDTH, HG_WIDTH, POOL_WIDTH, N_BRANCH * D_MODEL)
D_IN = sum(IN_SIZES)

kernel_name = 'hybrid_diffattn_hgrn2_pool_macaron_prefix'


def rms_norm(x, gain):
    xf = x.astype(jnp.float32)
    y = xf * lax.rsqrt(jnp.mean(xf * xf, axis=-1, keepdims=True) + NORM_EPS)
    return (y * gain.astype(jnp.float32)).astype(x.dtype)


def modulate(x, shift, scale):
    return x * (1 + scale) + shift


def swiglu(x, w1, w3, w2):
    return (jax.nn.silu(x @ w1) * (x @ w3)) @ w2


def split_columns(z):
    offsets = np.cumsum(IN_SIZES)[:-1].tolist()
    return jnp.split(z, offsets, axis=-1)


def axial_rope_tables(n_tokens):
    rows = n_tokens // GRID_W
    row = jnp.repeat(jnp.arange(rows, dtype=jnp.int32), GRID_W).astype(jnp.float32)
    col = jnp.tile(jnp.arange(GRID_W, dtype=jnp.int32), rows).astype(jnp.float32)
    axis_dim = DA_HALF_DIM // 2
    inv_freq = ROPE_THETA ** (-jnp.arange(0, axis_dim, 2, dtype=jnp.float32) / axis_dim)
    ang_r = row[:, None] * inv_freq[None, :]
    ang_c = col[:, None] * inv_freq[None, :]
    return (jnp.cos(ang_r), jnp.sin(ang_r), jnp.cos(ang_c), jnp.sin(ang_c))


def rotate_pairs(x, cos, sin):
    x1, x2 = jnp.split(x, 2, axis=-1)
    return jnp.concatenate([x1 * cos - x2 * sin, x1 * sin + x2 * cos], axis=-1)


def apply_axial_rope(x, rope):
    cos_r, sin_r, cos_c, sin_c = [t[None, :, None, None, :] for t in rope]
    xf = x.astype(jnp.float32)
    half = DA_HALF_DIM // 2
    out = jnp.concatenate([rotate_pairs(xf[..., :half], cos_r, sin_r),
                           rotate_pairs(xf[..., half:], cos_c, sin_c)], axis=-1)
    return out.astype(x.dtype)


def diff_lambda(lq1, lk1, lq2, lk2, lam_init):
    e1 = jnp.exp(jnp.sum(lq1.astype(jnp.float32) * lk1.astype(jnp.float32)))
    e2 = jnp.exp(jnp.sum(lq2.astype(jnp.float32) * lk2.astype(jnp.float32)))
    return e1 - e2 + lam_init


def diff_attend(q, k, v, lam):
    s = jnp.einsum('bqhcd,bkhcd->bhcqk', q, k).astype(jnp.float32) * (DA_HALF_DIM ** -0.5)
    p = jax.nn.softmax(s, axis=-1)
    w = p[:, :, 0] - lam * p[:, :, 1]
    return jnp.einsum('bhqk,bkhe->bqhe', w.astype(v.dtype), v)


def diff_attention_latent(q, k_all, v_all, lam):
    b, n, h, _, d = q.shape
    nb = n // Q_BLOCK
    qb = jnp.moveaxis(q.reshape(b, nb, Q_BLOCK, h, 2, d), 1, 0)
    ob = lax.map(lambda blk: diff_attend(blk, k_all, v_all, lam), qb)
    return jnp.moveaxis(ob, 0, 1).reshape(b, n, h, DA_V_DIM)


def diff_post(o, gain, lam_init):
    b, n = o.shape[:2]
    return (rms_norm(o, gain) * (1.0 - lam_init)).reshape(b, n, DA_WIDTH)


def heads_first(z, head_dim):
    b, n, _ = z.shape
    return z.reshape(b, n, -1, head_dim).transpose(0, 2, 1, 3).astype(jnp.float32)


def flip_time(a):
    return a[..., ::-1, :]


def hgrn_direction_inputs(zq, zf_fwd, zf_bwd, zi, lb):
    q = heads_first(jax.nn.silu(zq), HG_KEY_DIM)
    v = heads_first(zi, HG_VAL_DIM)
    keys, logfs = [], []
    for zf, lbd in ((zf_fwd, lb[0]), (zf_bwd, lb[1])):
        zf = heads_first(zf, HG_KEY_DIM)
        lbd = lbd.astype(jnp.float32).reshape(1, HG_HEADS, 1, HG_KEY_DIM)
        keys.append((1.0 - lbd) * jax.nn.sigmoid(-zf))
        log_lb = jnp.log(jnp.maximum(lbd, LB_FLOOR))
        logfs.append(jnp.logaddexp(log_lb, jnp.log1p(-lbd) + jax.nn.log_sigmoid(zf)))
    return (jnp.stack([q, flip_time(q)]),
            jnp.stack([keys[0], flip_time(keys[1])]),
            jnp.stack([logfs[0], flip_time(logfs[1])]),
            jnp.stack([v, flip_time(v)]))


def hgrn_chunk_scan(q, k, logf, v, s0):
    n = q.shape[-2]
    nc = n // HG_CHUNK

    def to_chunks(a):
        return jnp.moveaxis(a.reshape(a.shape[:-2] + (nc, HG_CHUNK, a.shape[-1])), -3, 0)

    in_chunk = jnp.tril(jnp.ones((HG_CHUNK, HG_CHUNK), dtype=bool))[:, :, None]

    def step(state, inp):
        qc, kc, lfc, vc = inp
        a = jnp.cumsum(lfc, axis=-2)
        a_end = a[..., -1:, :]
        o_inter = jnp.einsum('...tk,...kv->...tv', qc * jnp.exp(a), state)
        diff = a[..., :, None, :] - a[..., None, :, :]
        decay = jnp.where(in_chunk, jnp.exp(jnp.where(in_chunk, diff, 0.0)), 0.0)
        scores = jnp.einsum('...tk,...tsk,...sk->...ts', qc, decay, kc)
        o_intra = jnp.einsum('...ts,...sv->...tv', scores, vc)
        new_state = (jnp.exp(a_end[..., 0, :])[..., :, None] * state
                     + jnp.einsum('...sk,...sv->...kv', kc * jnp.exp(a_end - a), vc))
        return new_state, o_inter + o_intra

    final_state, o = lax.scan(step, s0, (to_chunks(q), to_chunks(k), to_chunks(logf), to_chunks(v)))
    o = jnp.moveaxis(o, 0, -3)
    return o.reshape(o.shape[:-3] + (n, o.shape[-1])), final_state


def hgrn_scans(zq, zff, zfb, zi, zqc, zffc, zfbc, zic, lb):
    b = zq.shape[0]
    s0 = jnp.zeros((2, b, HG_HEADS, HG_KEY_DIM, HG_VAL_DIM), jnp.float32)
    o_ctx, s_ctx = hgrn_chunk_scan(*hgrn_direction_inputs(zqc, zffc, zfbc, zic, lb), s0)
    o_lat, _ = hgrn_chunk_scan(*hgrn_direction_inputs(zq, zff, zfb, zi, lb), s_ctx)
    return o_lat, o_ctx


def hgrn_readout(o_dir, zg, gain):
    b, n, _ = zg.shape
    o = jnp.swapaxes(o_dir[0] + flip_time(o_dir[1]), 1, 2)
    o = rms_norm(o, gain).astype(zg.dtype)
    return (o * jax.nn.silu(zg.reshape(b, n, HG_HEADS, HG_VAL_DIM))).reshape(b, n, HG_WIDTH)


def hgrn_lower_bounds(logits):
    p = jax.nn.softmax(logits.astype(jnp.float32), axis=0)
    return jnp.cumsum(p, axis=0) - p[0:1]


def centred_mean_minus_self(x, window):
    b, n, ch = x.shape
    csum = jnp.concatenate([jnp.zeros((b, 1, ch), x.dtype), jnp.cumsum(x, axis=1)], axis=1)
    idx = jnp.arange(n)
    lo = jnp.clip(idx - window // 2, 0, n)
    hi = jnp.clip(idx + window // 2, 0, n)
    count = (hi - lo).astype(x.dtype)[None, :, None]
    return (csum[:, hi] - csum[:, lo]) / count - x


def pool_branch(zp, pool_w, pool_scale):
    b, n, _ = zp.shape
    groups = zp.reshape(b, n, POOL_GROUPS, POOL_GROUP_DIM).astype(jnp.float32)
    mixed = jnp.stack([centred_mean_minus_self(groups[:, :, g], w) for g, w in enumerate(POOL_WINDOWS)], axis=2)
    y = jnp.einsum('bngc,gcd->bngd', mixed.astype(zp.dtype), pool_w)
    return y.reshape(b, n, POOL_WIDTH) * pool_scale


def gated_merge(o_da, o_hg, o_pool, zgate, w_pa, w_ph, w_pp, w_o):
    b, n, _ = zgate.shape
    g = jax.nn.sigmoid(zgate.reshape(b, n, N_BRANCH, D_MODEL))
    y = g[:, :, 0] * (o_da @ w_pa) + g[:, :, 1] * (o_hg @ w_ph) + g[:, :, 2] * (o_pool @ w_pp)
    return y @ w_o


def token_mixer(u, uc, w_in, lam, lam_init, da_gain, lb, hg_gain, pool_w, pool_scale,
                w_pa, w_ph, w_pp, w_o, rope, need_ctx):
    b, n, _ = u.shape
    n_ctx = uc.shape[1]
    dq, dk, dv, hq, hff, hfb, hi, hgate, zp, zgate = split_columns(u @ w_in)
    dqc, dkc, dvc, hqc, hffc, hfbc, hic, hgatec, zpc, zgatec = split_columns(uc @ w_in)

    def da_heads(z, length):
        return z.reshape(b, length, DA_HEADS, 2, DA_HALF_DIM)

    q = apply_axial_rope(da_heads(dq, n), rope)
    k = apply_axial_rope(da_heads(dk, n), rope)
    kc = da_heads(dkc, n_ctx)
    v = dv.reshape(b, n, DA_HEADS, DA_V_DIM)
    vc = dvc.reshape(b, n_ctx, DA_HEADS, DA_V_DIM)
    o_da = diff_attention_latent(q, jnp.concatenate([k, kc], axis=1), jnp.concatenate([v, vc], axis=1), lam)
    o_hg_dir, o_hgc_dir = hgrn_scans(hq, hff, hfb, hi, hqc, hffc, hfbc, hic, lb)
    o_pool = pool_branch(zp, pool_w, pool_scale)
    mix = gated_merge(diff_post(o_da, da_gain, lam_init), hgrn_readout(o_hg_dir, hgate, hg_gain),
                      o_pool, zgate, w_pa, w_ph, w_pp, w_o)
    if not need_ctx:
        return mix, None
    o_dac = diff_attend(da_heads(dqc, n_ctx), kc, vc, lam)
    mix_c = gated_merge(diff_post(o_dac, da_gain, lam_init), hgrn_readout(o_hgc_dir, hgatec, hg_gain),
                        pool_branch(zpc, pool_w, pool_scale), zgatec, w_pa, w_ph, w_pp, w_o)
    return mix, mix_c


def setup_inputs(seed: int = 0) -> dict:
    key = jax.random.key(seed)
    ks = jax.random.split(key, 32)
    D = D_MODEL

    def nrm(k, shape, scale):
        return jax.random.normal(k, shape, jnp.float32) * scale

    return {
        'x': nrm(ks[0], (BATCH, SEQ, D), 1.0),
        'c': nrm(ks[1], (BATCH, D), 1.0),
        'ctx': nrm(ks[2], (BATCH, CTX_LEN, D), 1.0),
        'c_ctx': nrm(ks[3], (D,), 1.0),
        'w_ada': nrm(ks[4], (DEPTH, D, N_MOD * D), 0.5 * D ** -0.5),
        'b_ada': nrm(ks[5], (DEPTH, N_MOD * D), 0.02),
        'norm_ffn1': 1.0 + nrm(ks[6], (DEPTH, D), 0.02),
        'norm_mix': 1.0 + nrm(ks[7], (DEPTH, D), 0.02),
        'norm_ffn2': 1.0 + nrm(ks[8], (DEPTH, D), 0.02),
        'ffn1_w1': nrm(ks[9], (DEPTH, D, D_FF), D ** -0.5),
        'ffn1_w3': nrm(ks[10], (DEPTH, D, D_FF), D ** -0.5),
        'ffn1_w2': nrm(ks[11], (DEPTH, D_FF, D), D_FF ** -0.5),
        'ffn2_w1': nrm(ks[12], (DEPTH, D, D_FF), D ** -0.5),
        'ffn2_w3': nrm(ks[13], (DEPTH, D, D_FF), D ** -0.5),
        'ffn2_w2': nrm(ks[14], (DEPTH, D_FF, D), D_FF ** -0.5),
        'w_in': nrm(ks[15], (DEPTH, D, D_IN), D ** -0.5),
        'da_lambda_q1': nrm(ks[16], (DEPTH, DA_HALF_DIM), 0.1),
        'da_lambda_k1': nrm(ks[17], (DEPTH, DA_HALF_DIM), 0.1),
        'da_lambda_q2': nrm(ks[18], (DEPTH, DA_HALF_DIM), 0.1),
        'da_lambda_k2': nrm(ks[19], (DEPTH, DA_HALF_DIM), 0.1),
        'da_subln': 1.0 + nrm(ks[20], (DEPTH, DA_V_DIM), 0.02),
        'hg_lb_logits': nrm(ks[21], (DEPTH, 2, HG_KEY_WIDTH), 0.5),
        'hg_norm': 1.0 + nrm(ks[22], (DEPTH, HG_VAL_DIM), 0.02),
        'pool_w': nrm(ks[23], (DEPTH, POOL_GROUPS, POOL_GROUP_DIM, POOL_GROUP_DIM), POOL_GROUP_DIM ** -0.5),
        'pool_scale': 1.0 + nrm(ks[24], (DEPTH, POOL_WIDTH), 0.02),
        'w_proj_da': nrm(ks[25], (DEPTH, DA_WIDTH, D), DA_WIDTH ** -0.5),
        'w_proj_hg': nrm(ks[26], (DEPTH, HG_WIDTH, D), HG_WIDTH ** -0.5),
        'w_proj_pool': nrm(ks[27], (DEPTH, POOL_WIDTH, D), POOL_WIDTH ** -0.5),
        'w_out': nrm(ks[28], (DEPTH, D, D), D ** -0.5),
        'final_norm': 1.0 + nrm(ks[29], (D,), 0.02),
    }


def reference(x, c, ctx, c_ctx, w_ada, b_ada, norm_ffn1, norm_mix, norm_ffn2,
              ffn1_w1, ffn1_w3, ffn1_w2, ffn2_w1, ffn2_w3, ffn2_w2, w_in,
              da_lambda_q1, da_lambda_k1, da_lambda_q2, da_lambda_k2, da_subln,
              hg_lb_logits, hg_norm, pool_w, pool_scale,
              w_proj_da, w_proj_hg, w_proj_pool, w_out, final_norm):
    n = x.shape[1]
    rope = axial_rope_tables(n)
    lb_all = hgrn_lower_bounds(hg_lb_logits)
    h, hc = x, ctx
    for l in range(DEPTH):
        need_ctx = l < DEPTH - 1
        m = [t[:, None, :] for t in jnp.split(jax.nn.silu(c) @ w_ada[l] + b_ada[l], N_MOD, axis=-1)]
        mc = jnp.split(jax.nn.silu(c_ctx) @ w_ada[l] + b_ada[l], N_MOD, axis=-1)
        ffn1 = (ffn1_w1[l], ffn1_w3[l], ffn1_w2[l])
        ffn2 = (ffn2_w1[l], ffn2_w3[l], ffn2_w2[l])
        h = h + 0.5 * m[2] * swiglu(modulate(rms_norm(h, norm_ffn1[l]), m[0], m[1]), *ffn1)
        hc = hc + 0.5 * mc[2] * swiglu(modulate(rms_norm(hc, norm_ffn1[l]), mc[0], mc[1]), *ffn1)
        u = modulate(rms_norm(h, norm_mix[l]), m[3], m[4])
        uc = modulate(rms_norm(hc, norm_mix[l]), mc[3], mc[4])
        lam_init = 0.8 - 0.6 * math.exp(-0.3 * l)
        lam = diff_lambda(da_lambda_q1[l], da_lambda_k1[l], da_lambda_q2[l], da_lambda_k2[l], lam_init)
        mix, mix_c = token_mixer(u, uc, w_in[l], lam, lam_init, da_subln[l], lb_all[l], hg_norm[l],
                                 pool_w[l], pool_scale[l], w_proj_da[l], w_proj_hg[l], w_proj_pool[l],
                                 w_out[l], rope, need_ctx)
        h = h + m[5] * mix
        h = h + 0.5 * m[8] * swiglu(modulate(rms_norm(h, norm_ffn2[l]), m[6], m[7]), *ffn2)
        if need_ctx:
            hc = hc + mc[5] * mix_c
            hc = hc + 0.5 * mc[8] * swiglu(modulate(rms_norm(hc, norm_ffn2[l]), mc[6], mc[7]), *ffn2)
    return rms_norm(h, final_norm)
```

```python
import functools
import math

import jax
import jax.numpy as jnp
from jax import lax
from jax.experimental import pallas as pl
from jax.experimental.pallas import tpu as pltpu

D = 1024
GRID_WIDTH = 64
EPS = 1e-6
LB_MIN = 1e-20
NMOD = 9
HEADS = 4
HALF = 64
VDIM = 128
QKW = HEADS * 2 * HALF
HGW = 256
HGH = 64
PW = 256
POOL_WINDOWS = (2, 4, 8, 16)
DFF = 2816
THETA = 10000.0
IN_A = 3 * QKW + 6 * 256
CHUNK = 16
NEG = -1e30

F32 = jnp.float32
BF16 = jnp.bfloat16
VMEM_LIMIT = 56 * 1024 * 1024


def _const_spec(shape):
    nd = len(shape)
    return pl.BlockSpec(shape, lambda *_: (0,) * nd, pipeline_mode=pl.Buffered(1))


def _rms(x, gain):
    return x * lax.rsqrt(jnp.mean(x * x, axis=-1, keepdims=True) + EPS) * gain


def _sigmoid(x):
    return 1.0 / (1.0 + jnp.exp(-x))


def _dot(a, b):
    return jnp.dot(a, b, preferred_element_type=F32)


def _dot_nt(a, b):
    return lax.dot_general(a, b, (((1,), (1,)), ((), ())), preferred_element_type=F32)


def _dot_tn(a, b):
    return lax.dot_general(a, b, (((0,), (0,)), ((), ())), preferred_element_type=F32)


def _ada_kernel(c_ref, w_ref, b_ref, o_ref):
    c = c_ref[...]
    s = c * _sigmoid(c)
    o_ref[0] = jnp.dot(s, w_ref[0], preferred_element_type=F32,
                       precision=lax.Precision.HIGHEST) + b_ref[0]


def _ada(cvec, w_ada, b_ada):
    depth = w_ada.shape[0]
    nblk = (NMOD * D) // D
    return pl.pallas_call(
        _ada_kernel,
        grid=(depth, nblk),
        in_specs=[pl.BlockSpec((8, D), lambda l, j: (0, 0)),
                  pl.BlockSpec((1, D, D), lambda l, j: (l, 0, j)),
                  pl.BlockSpec((1, 1, D), lambda l, j: (l, 0, j))],
        out_specs=pl.BlockSpec((1, 8, D), lambda l, j: (l, 0, j)),
        out_shape=jax.ShapeDtypeStruct((depth, 8, NMOD * D), F32),
        name="ada",
    )(cvec, w_ada, b_ada.reshape(depth, 1, NMOD * D))


def _swiglu_half(h, m, shift_i, gain, w1_ref, w3_ref, w2_ref):
    x = _rms(h, gain) * (1.0 + m[shift_i + 1:shift_i + 2]) + m[shift_i:shift_i + 1]
    xb = x.astype(BF16)
    a = _dot(xb, w1_ref[...])
    b = _dot(xb, w3_ref[...])
    g = (a * _sigmoid(a) * b).astype(BF16)
    y = _dot(g, w2_ref[...])
    return h + 0.5 * m[shift_i + 2:shift_i + 3] * y


def _ffn_in_kernel(h_ref, mod_ref, n1_ref, nm_ref, w1_ref, w3_ref, w2_ref, win_ref, rc_ref, rs_ref,
                   h1_ref, qt_ref, k_ref, vt_ref, hq_ref, hf_ref, hi_ref, hg_ref, zp_ref, *, rope):
    h = h_ref[0]
    m = mod_ref[0]
    h1 = _swiglu_half(h, m, 0, n1_ref[...], w1_ref, w3_ref, w2_ref)
    h1_ref[0] = h1
    u = _rms(h1, nm_ref[...]) * (1.0 + m[4:5]) + m[3:4]
    z = _dot(u.astype(BF16), win_ref[...])
    q = z[:, 0:QKW]
    k = z[:, QKW:2 * QKW]
    v = z[:, 2 * QKW:3 * QKW]
    if rope:
        lane = lax.broadcasted_iota(jnp.int32, q.shape, 1)
        first = (lane % 32) < 16
        rc = jnp.concatenate([rc_ref[...]] * (QKW // 128), axis=1)
        rs = jnp.concatenate([rs_ref[...]] * (QKW // 128), axis=1)

        def rot(x):
            partner = jnp.where(first, pltpu.roll(x, QKW - 16, 1), pltpu.roll(x, 16, 1))
            return x * rc + partner * rs

        q = rot(q)
        k = rot(k)
    q = q * (HALF ** -0.5)
    qt_ref[0] = q.T.astype(BF16)
    k_ref[0] = k.astype(BF16)
    vt_ref[0] = v.T.astype(BF16)
    o = 3 * QKW
    hq_ref[0] = z[:, o:o + 256]
    hf_ref[0, 0] = z[:, o + 256:o + 512]
    hf_ref[0, 1] = z[:, o + 512:o + 768]
    hi_ref[0] = z[:, o + 768:o + 1024]
    hg_ref[0] = z[:, o + 1024:o + 1280]
    zp_ref[0] = z[:, o + 1280:o + 1536]


def _ffn_in(h, mod, n1, nm, w1, w3, w2, win, rc, rs, *, rope, tm):
    b, t, _ = h.shape
    grid = (b, t // tm)
    tok = lambda w: pl.BlockSpec((1, tm, w), lambda bi, i: (bi, i, 0))
    tokt = pl.BlockSpec((1, QKW, tm), lambda bi, i: (bi, 0, i))
    out_shape = (
        jax.ShapeDtypeStruct((b, t, D), F32),
        jax.ShapeDtypeStruct((b, QKW, t), BF16),
        jax.ShapeDtypeStruct((b, t, QKW), BF16),
        jax.ShapeDtypeStruct((b, QKW, t), BF16),
        jax.ShapeDtypeStruct((b, t, 256), F32),
        jax.ShapeDtypeStruct((b, 2, t, 256), F32),
        jax.ShapeDtypeStruct((b, t, 256), F32),
        jax.ShapeDtypeStruct((b, t, 256), F32),
        jax.ShapeDtypeStruct((b, t, 256), F32),
    )
    out_specs = (tok(D), tokt, tok(QKW), tokt, tok(256),
                 pl.BlockSpec((1, 2, tm, 256), lambda bi, i: (bi, 0, i, 0)),
                 tok(256), tok(256), tok(256))
    in_specs = [tok(D),
                pl.BlockSpec((1, NMOD, D), lambda bi, i: (bi, 0, 0)),
                _const_spec((1, D)), _const_spec((1, D)),
                _const_spec((D, DFF)), _const_spec((D, DFF)), _const_spec((DFF, D)),
                _const_spec((D, IN_A)),
                pl.BlockSpec((tm, 128), lambda bi, i: (i, 0)),
                pl.BlockSpec((tm, 128), lambda bi, i: (i, 0))]
    return pl.pallas_call(
        functools.partial(_ffn_in_kernel, rope=rope),
        grid=grid, in_specs=in_specs, out_specs=out_specs, out_shape=out_shape,
        compiler_params=pltpu.CompilerParams(
            dimension_semantics=("parallel", "parallel"), vmem_limit_bytes=VMEM_LIMIT),
        name="ffn_in",
    )(h, mod, n1, nm, w1, w3, w2, win, rc, rs)


def _attn_kernel(*refs, tq, tk, n_main, has_extra, lam_init):
    if has_extra:
        (qt_ref, k_ref, vt_ref, kx_ref, vtx_ref, lam_ref, gain_ref, o_ref, acc1, acc2) = refs
    else:
        (qt_ref, k_ref, vt_ref, lam_ref, gain_ref, o_ref, acc1, acc2) = refs
    qt = qt_ref[0]
    row = lax.broadcasted_iota(jnp.int32, qt.shape, 0)
    zero = jnp.zeros_like(qt)
    qbd = jnp.concatenate([jnp.where(row < HALF, qt, zero), jnp.where(row >= HALF, qt, zero)], axis=1)
    acc1[...] = jnp.zeros_like(acc1)
    acc2[...] = jnp.zeros_like(acc2)

    def step(kb, vtb, carry):
        m, l = carry
        s = _dot(kb, qbd)
        m_new = jnp.maximum(m, jnp.max(s, axis=0, keepdims=True))
        alpha = jnp.exp(m - m_new)
        p = jnp.exp(s - m_new)
        l = alpha * l + jnp.sum(p, axis=0, keepdims=True)
        pb = p.astype(BF16)
        acc1[...] = acc1[...] * alpha[:, :tq] + _dot(vtb, pb[:, :tq])
        acc2[...] = acc2[...] * alpha[:, tq:] + _dot(vtb, pb[:, tq:])
        return m_new, l

    def body(i, carry):
        off = pl.multiple_of(i * tk, tk)
        return step(k_ref[0, pl.ds(off, tk), :], vt_ref[0, :, pl.ds(off, tk)], carry)

    carry = (jnp.full((1, 2 * tq), NEG, F32), jnp.zeros((1, 2 * tq), F32))
    carry = lax.fori_loop(0, n_main, body, carry)
    if has_extra:
        carry = step(kx_ref[0], vtx_ref[0], carry)
    _, l = carry

    lv = lam_ref[...]
    lam = (jnp.exp(jnp.sum(lv[0:1] * lv[1:2], axis=1, keepdims=True))
           - jnp.exp(jnp.sum(lv[2:3] * lv[3:4], axis=1, keepdims=True)) + lam_init)
    o = acc1[...] / l[:, :tq] - lam * (acc2[...] / l[:, tq:])
    o = o * lax.rsqrt(jnp.mean(o * o, axis=0, keepdims=True) + EPS) * gain_ref[...] * (1.0 - lam_init)
    o_ref[0] = o.T.astype(BF16)


def _attention(qt, k, vt, kx, vtx, lamv, gain, *, lam_init, tq, tk):
    b, _, t_q = qt.shape
    t_k = k.shape[1]
    has_extra = kx is not None
    grid = (b, HEADS, t_q // tq)
    in_specs = [pl.BlockSpec((1, VDIM, tq), lambda bi, hi, i: (bi, hi, i)),
                pl.BlockSpec((1, t_k, VDIM), lambda bi, hi, i: (bi, 0, hi)),
                pl.BlockSpec((1, VDIM, t_k), lambda bi, hi, i: (bi, hi, 0))]
    args = [qt, k, vt]
    if has_extra:
        t_x = kx.shape[1]
        in_specs += [pl.BlockSpec((1, t_x, VDIM), lambda bi, hi, i: (bi, 0, hi)),
                     pl.BlockSpec((1, VDIM, t_x), lambda bi, hi, i: (bi, hi, 0))]
        args += [kx, vtx]
    in_specs += [pl.BlockSpec((4, HALF), lambda bi, hi, i: (0, 0)),
                 pl.BlockSpec((VDIM, 1), lambda bi, hi, i: (0, 0))]
    args += [lamv, gain]
    return pl.pallas_call(
        functools.partial(_attn_kernel, tq=tq, tk=tk, n_main=t_k // tk, has_extra=has_extra,
                          lam_init=lam_init),
        grid=grid, in_specs=in_specs,
        out_specs=pl.BlockSpec((1, tq, VDIM), lambda bi, hi, i: (bi, i, hi)),
        out_shape=jax.ShapeDtypeStruct((b, t_q, HEADS * VDIM), BF16),
        scratch_shapes=[pltpu.VMEM((VDIM, tq), F32), pltpu.VMEM((VDIM, tq), F32)],
        compiler_params=pltpu.CompilerParams(
            dimension_semantics=("parallel", "parallel", "parallel"), vmem_limit_bytes=VMEM_LIMIT),
        name="diff_attn",
    )(*args)


def _chunk_cumsum(x, rev):
    n = x.shape[0]
    pos = lax.broadcasted_iota(jnp.int32, x.shape, 0) % CHUNK
    sh = 1
    while sh < CHUNK:
        if rev:
            x = x + jnp.where(pos + sh < CHUNK, pltpu.roll(x, n - sh, 0), 0.0)
        else:
            x = x + jnp.where(pos >= sh, pltpu.roll(x, sh, 0), 0.0)
        sh *= 2
    return x


def _hgrn_direction(zq, zf, zi, lbv, st_ref, o_ref, stall, dst, *, rev, tb):
    nc = tb // CHUNK
    q = zq * _sigmoid(zq)
    kk = (1.0 - lbv) * _sigmoid(-zf)
    log_lb = jnp.log(jnp.maximum(lbv, LB_MIN))
    log_sig = jnp.minimum(zf, 0.0) - jnp.log1p(jnp.exp(-jnp.abs(zf)))
    y = jnp.log1p(-lbv) + log_sig
    lf = jnp.maximum(log_lb, y) + jnp.log1p(jnp.exp(-jnp.abs(log_lb - y)))
    a = _chunk_cumsum(lf, rev)
    a3 = a.reshape(nc, CHUNK, HGW)
    end_row = 0 if rev else CHUNK - 1
    mid_row = CHUNK // 2 if rev else CHUNK // 2 - 1
    a_end = a3[:, end_row:end_row + 1, :]
    a_mid = a3[:, mid_row:mid_row + 1, :]
    q3 = q.reshape(nc, CHUNK, HGW)
    k3 = kk.reshape(nc, CHUNK, HGW)
    qe = (q3 * jnp.exp(a3)).astype(BF16)
    ke = (k3 * jnp.exp(a_end - a3)).astype(BF16)
    qm = (q3 * jnp.exp(a3 - a_mid)).reshape(tb, HGW)
    km = (k3 * jnp.exp(a_mid - a3)).reshape(tb, HGW).astype(BF16)
    vb = zi.astype(BF16)
    v3 = vb.reshape(nc, CHUNK, HGW)
    dec = jnp.exp(a_end)

    lane_head = lax.broadcasted_iota(jnp.int32, (1, HGW), 1) // HGH
    sub = 128
    r_i = lax.broadcasted_iota(jnp.int32, (sub, sub), 0)
    c_i = lax.broadcasted_iota(jnp.int32, (sub, sub), 1)
    same = (r_i // CHUNK) == (c_i // CHUNK)
    causal = jnp.logical_and(same, (c_i >= r_i) if rev else (c_i <= r_i))
    causal4 = jnp.concatenate([causal] * HEADS, axis=0)
    intra = []
    for g in range(tb // sub):
        sl = slice(g * sub, (g + 1) * sub)
        qg = qm[sl]
        qstack = jnp.concatenate(
            [jnp.where(lane_head == hh, qg, 0.0) for hh in range(HEADS)], axis=0).astype(BF16)
        sc = _dot_nt(qstack, km[sl])
        sc = jnp.where(causal4, sc, 0.0).astype(BF16)
        r = _dot(sc, vb[sl])
        og = jnp.zeros((sub, HGW), F32)
        for hh in range(HEADS):
            og = og + jnp.where(lane_head == hh, r[hh * sub:(hh + 1) * sub], 0.0)
        intra.append(og)
    o_intra = jnp.concatenate(intra, axis=0)

    head_mask = (lax.broadcasted_iota(jnp.int32, (HGW, HGW), 0) // HGH
                 == lax.broadcasted_iota(jnp.int32, (HGW, HGW), 1) // HGH)
    for c in range(nc):
        dst[c] = jnp.where(head_mask, _dot_tn(v3[c], ke[c]), 0.0)
    st = st_ref[...]
    order = range(nc - 1, -1, -1) if rev else range(nc)
    for c in order:
        stall[c] = st.astype(BF16)
        st = dec[c] * st + dst[c]
    st_ref[...] = st
    inter = [_dot_nt(qe[c], stall[c]) for c in range(nc)]
    o_ref[...] = o_intra + jnp.concatenate(inter, axis=0)


def _hgrn_kernel(zqf_ref, zff_ref, zif_ref, zqb_ref, zfb_ref, zib_ref, lb_ref, s0_ref,
                 of_ref, ob_ref, s_ref, stall_f, dst_f, stall_b, dst_b, *, tb):
    @pl.when(pl.program_id(1) == 0)
    def _():
        s_ref[...] = s0_ref[...]

    lb = lb_ref[...]
    _hgrn_direction(zqf_ref[0], zff_ref[0, 0], zif_ref[0], lb[0:1], s_ref.at[0, 0], of_ref.at[0],
                    stall_f, dst_f, rev=False, tb=tb)
    _hgrn_direction(zqb_ref[0], zfb_ref[0, 0], zib_ref[0], lb[1:2], s_ref.at[0, 1], ob_ref.at[0],
                    stall_b, dst_b, rev=True, tb=tb)


def _hgrn(zq, zf, zi, lb, s0, *, tb):
    b, t, _ = zq.shape
    nb = t // tb
    fwd = lambda bi, i: (bi, i, 0)
    bwd = lambda bi, i: (bi, nb - 1 - i, 0)
    in_specs = [pl.BlockSpec((1, tb, HGW), fwd),
                pl.BlockSpec((1, 1, tb, HGW), lambda bi, i: (bi, 0, i, 0)),
                pl.BlockSpec((1, tb, HGW), fwd),
                pl.BlockSpec((1, tb, HGW), bwd),
                pl.BlockSpec((1, 1, tb, HGW), lambda bi, i: (bi, 1, nb - 1 - i, 0)),
                pl.BlockSpec((1, tb, HGW), bwd),
                pl.BlockSpec((2, HGW), lambda bi, i: (0, 0)),
                pl.BlockSpec((1, 2, HGW, HGW), lambda bi, i: (bi, 0, 0, 0))]
    out_specs = (pl.BlockSpec((1, tb, HGW), fwd),
                 pl.BlockSpec((1, tb, HGW), bwd),
                 pl.BlockSpec((1, 2, HGW, HGW), lambda bi, i: (bi, 0, 0, 0)))
    out_shape = (jax.ShapeDtypeStruct((b, t, HGW), F32),
                 jax.ShapeDtypeStruct((b, t, HGW), F32),
                 jax.ShapeDtypeStruct((b, 2, HGW, HGW), F32))
    nc = tb // CHUNK
    of, ob, s_fin = pl.pallas_call(
        functools.partial(_hgrn_kernel, tb=tb),
        grid=(b, nb), in_specs=in_specs, out_specs=out_specs, out_shape=out_shape,
        scratch_shapes=[pltpu.VMEM((nc, HGW, HGW), BF16), pltpu.VMEM((nc, HGW, HGW), F32),
                        pltpu.VMEM((nc, HGW, HGW), BF16), pltpu.VMEM((nc, HGW, HGW), F32)],
        compiler_params=pltpu.CompilerParams(
            dimension_semantics=("parallel", "arbitrary"), vmem_limit_bytes=VMEM_LIMIT),
        name="hgrn",
    )(zq, zf, zi, zq, zf, zi, lb, s0)
    return of, ob, s_fin


def _merge_kernel(h_ref, mod_ref, nm_ref, n2_ref, oda_ref, of_ref, ob_ref, hg_ref,
                  zp_ref, zpp_ref, zpn_ref, hgn_ref, pw_ref, ps_ref,
                  wg_ref, wpa_ref, wph_ref, wpp_ref, wo_ref, w1_ref, w3_ref, w2_ref, fn_ref,
                  out_ref, *, tm, t_total, final):
    i = pl.program_id(1)
    h1 = h_ref[0]
    m = mod_ref[0]

    x = zp_ref[0]
    prev = jnp.where(i > 0, zpp_ref[0], 0.0)
    nxt = jnp.where(i < pl.num_programs(1) - 1, zpn_ref[0], 0.0)
    e = jnp.concatenate([prev, x, nxt], axis=0)
    n = tm + 16
    a2 = e[0:n - 1] + e[1:n]
    a4 = a2[0:n - 3] + a2[2:n - 1]
    a8 = a4[0:n - 7] + a4[4:n - 3]
    a16 = a8[0:n - 15] + a8[8:n - 7]
    sums = (a2[7:7 + tm], a4[6:6 + tm], a8[4:4 + tm], a16[0:tm])
    pos = i * tm + lax.broadcasted_iota(jnp.int32, (tm, PW), 0)
    group = lax.broadcasted_iota(jnp.int32, (tm, PW), 1) // 64
    mixed = jnp.zeros((tm, PW), F32)
    for g, w in enumerate(POOL_WINDOWS):
        cnt = (jnp.minimum(pos + w // 2, t_total) - jnp.maximum(pos - w // 2, 0)).astype(F32)
        mixed = jnp.where(group == g, sums[g] / cnt - x, mixed)
    o_pool = _dot(mixed.astype(BF16), pw_ref[...]) * ps_ref[...]

    o = of_ref[0] + ob_ref[0]
    o2 = o * o
    hi = o2.astype(BF16)
    lo = (o2 - hi.astype(F32)).astype(BF16)
    ones_bd = (lax.broadcasted_iota(jnp.int32, (HGW, HGW), 0) // HGH
               == lax.broadcasted_iota(jnp.int32, (HGW, HGW), 1) // HGH).astype(BF16)
    seg = _dot(hi, ones_bd) + _dot(lo, ones_bd)
    zg = hg_ref[0]
    o_hg = (o * lax.rsqrt(seg * (1.0 / HGH) + EPS) * hgn_ref[...]) * (zg * _sigmoid(zg))

    u = _rms(h1, nm_ref[...]) * (1.0 + m[4:5]) + m[3:4]
    gate = _sigmoid(_dot(u.astype(BF16), wg_ref[...]))
    y = (gate[:, 0:D] * _dot(oda_ref[0], wpa_ref[...])
         + gate[:, D:2 * D] * _dot(o_hg.astype(BF16), wph_ref[...])
         + gate[:, 2 * D:3 * D] * _dot(o_pool.astype(BF16), wpp_ref[...]))
    mix = _dot(y.astype(BF16), wo_ref[...])
    h2 = h1 + m[5:6] * mix
    h3 = _swiglu_half(h2, m, 6, n2_ref[...], w1_ref, w3_ref, w2_ref)
    if final:
        h3 = _rms(h3, fn_ref[...])
    out_ref[0] = h3


def _merge(h1, mod, nm, n2, oda, of, ob, hg, zp, hgn, pw, ps, wg, wpa, wph, wpp, wo, w1, w3, w2, fn,
           *, tm, final):
    b, t, _ = h1.shape
    nblk8 = t // 8
    r = tm // 8
    tok = lambda w: pl.BlockSpec((1, tm, w), lambda bi, i: (bi, i, 0))
    in_specs = [tok(D),
                pl.BlockSpec((1, NMOD, D), lambda bi, i: (bi, 0, 0)),
                _const_spec((1, D)), _const_spec((1, D)),
                tok(QKW),
                tok(HGW), tok(HGW),
                tok(HGW), tok(PW),
                pl.BlockSpec((1, 8, PW), lambda bi, i: (bi, jnp.maximum(i * r - 1, 0), 0)),
                pl.BlockSpec((1, 8, PW), lambda bi, i: (bi, jnp.minimum((i + 1) * r, nblk8 - 1), 0)),
                _const_spec((1, HGW)), _const_spec((PW, PW)), _const_spec((1, PW)),
                _const_spec((D, 3 * D)), _const_spec((QKW, D)), _const_spec((HGW, D)),
                _const_spec((PW, D)), _const_spec((D, D)),
                _const_spec((D, DFF)), _const_spec((D, DFF)), _const_spec((DFF, D)),
                _const_spec((1, D))]
    return pl.pallas_call(
        functools.partial(_merge_kernel, tm=tm, t_total=t, final=final),
        grid=(b, t // tm), in_specs=in_specs, out_specs=tok(D),
        out_shape=jax.ShapeDtypeStruct((b, t, D), F32),
        compiler_params=pltpu.CompilerParams(
            dimension_semantics=("parallel", "parallel"), vmem_limit_bytes=VMEM_LIMIT),
        name="merge_ffn",
    )(h1, mod, nm, n2, oda, of, ob, hg, zp, zp, zp, hgn, pw, ps, wg, wpa, wph, wpp, wo, w1, w3, w2, fn)


def _rope_tables(n):
    rows = n // GRID_WIDTH
    row = jnp.repeat(jnp.arange(rows, dtype=jnp.int32), GRID_WIDTH).astype(F32)
    col = jnp.tile(jnp.arange(GRID_WIDTH, dtype=jnp.int32), rows).astype(F32)
    axis_dim = HALF // 2
    inv_freq = THETA ** (-jnp.arange(0, axis_dim, 2, dtype=F32) / axis_dim)
    ang_r = row[:, None] * inv_freq[None, :]
    ang_c = col[:, None] * inv_freq[None, :]
    cos64 = jnp.concatenate([jnp.cos(ang_r)] * 2 + [jnp.cos(ang_c)] * 2, axis=1)
    sin64 = jnp.concatenate([-jnp.sin(ang_r), jnp.sin(ang_r), -jnp.sin(ang_c), jnp.sin(ang_c)], axis=1)
    return jnp.tile(cos64, (1, 2)), jnp.tile(sin64, (1, 2))


def _lower_bounds(logits):
    p = jax.nn.softmax(logits.astype(F32), axis=0)
    return jnp.cumsum(p, axis=0) - p[0:1]


def _block_diag(w):
    g, a, b = w.shape
    out = jnp.zeros((g * a, g * b), w.dtype)
    for i in range(g):
        out = out.at[i * a:(i + 1) * a, i * b:(i + 1) * b].set(w[i])
    return out


def kernel(x, c, ctx, c_ctx, w_ada, b_ada, norm_ffn1, norm_mix, norm_ffn2, ffn1_w1, ffn1_w3, ffn1_w2,
           ffn2_w1, ffn2_w3, ffn2_w2, w_in, da_lambda_q1, da_lambda_k1, da_lambda_q2, da_lambda_k2,
           da_subln, hg_lb_logits, hg_norm, pool_w, pool_scale, w_proj_da, w_proj_hg, w_proj_pool,
           w_out, final_norm):
    bsz, n, _ = x.shape
    n_ctx = ctx.shape[1]
    depth = w_ada.shape[0]
    tm = 256
    tq, tk = 256, 512
    tb = 256

    cvec = jnp.zeros((8, D), F32).at[0:bsz].set(c).at[bsz].set(c_ctx)
    mods = _ada(cvec, w_ada, b_ada).reshape(depth, 8, NMOD, D)
    rc, rs = _rope_tables(n)
    rc_c = jnp.zeros((n_ctx, 128), F32)
    lb_all = _lower_bounds(hg_lb_logits)
    row = lambda v: v.reshape(1, -1)
    bf = lambda w: w.astype(BF16)

    h, hc = x, ctx
    for l in range(depth):
        need_ctx = l < depth - 1
        lam_init = 0.8 - 0.6 * math.exp(-0.3 * l)
        m_lat = mods[l, 0:bsz]
        m_ctx = jnp.broadcast_to(mods[l, bsz][None], (bsz, NMOD, D))
        w1a, w3a, w2a = bf(ffn1_w1[l]), bf(ffn1_w3[l]), bf(ffn1_w2[l])
        w1b, w3b, w2b = bf(ffn2_w1[l]), bf(ffn2_w3[l]), bf(ffn2_w2[l])
        win_a = bf(w_in[l][:, :IN_A])
        w_gate = bf(w_in[l][:, IN_A:])
        lamv = jnp.stack([da_lambda_q1[l], da_lambda_k1[l], da_lambda_q2[l], da_lambda_k2[l]]).astype(F32)
        gain_da = da_subln[l].reshape(VDIM, 1)
        hgn = row(jnp.tile(hg_norm[l], HEADS))
        pw_bd = bf(_block_diag(pool_w[l]))
        merge_w = (hgn, pw_bd, row(pool_scale[l]), w_gate, bf(w_proj_da[l]), bf(w_proj_hg[l]),
                   bf(w_proj_pool[l]), bf(w_out[l]), w1b, w3b, w2b, row(final_norm))

        (hc1, qt_c, k_c, vt_c, hq_c, hf_c, hi_c, hg_c, zp_c) = _ffn_in(
            hc, m_ctx, row(norm_ffn1[l]), row(norm_mix[l]), w1a, w3a, w2a, win_a, rc_c, rc_c,
            rope=False, tm=min(tm, n_ctx))
        s0 = jnp.zeros((bsz, 2, HGW, HGW), F32)
        of_c, ob_c, s_ctx = _hgrn(hq_c, hf_c, hi_c, lb_all[l], s0, tb=min(tb, n_ctx))

        (h1, qt, k, vt, hq, hf, hi, hg, zp) = _ffn_in(
            h, m_lat, row(norm_ffn1[l]), row(norm_mix[l]), w1a, w3a, w2a, win_a, rc, rs,
            rope=True, tm=tm)
        o_da = _attention(qt, k, vt, k_c, vt_c, lamv, gain_da, lam_init=lam_init, tq=tq, tk=tk)
        of, ob, _ = _hgrn(hq, hf, hi, lb_all[l], s_ctx, tb=tb)
        h = _merge(h1, m_lat, row(norm_mix[l]), row(norm_ffn2[l]), o_da, of, ob, hg, zp, *merge_w,
                   tm=tm, final=not need_ctx)
        if need_ctx:
            o_dac = _attention(qt_c, k_c, vt_c, None, None, lamv, gain_da, lam_init=lam_init,
                               tq=min(tq, n_ctx), tk=min(tk, n_ctx))
            hc = _merge(hc1, m_ctx, row(norm_mix[l]), row(norm_ffn2[l]), o_dac, of_c, ob_c, hg_c, zp_c,
                        *merge_w, tm=min(tm, n_ctx), final=False)
    return h
```

```python
import functools
import math

import jax
import jax.numpy as jnp
from jax import lax
from jax.experimental import pallas as pl
from jax.experimental.pallas import tpu as pltpu

D = 1024
GRID_WIDTH = 64
EPS = 1e-6
LB_MIN = 1e-20
NMOD = 9
HEADS = 4
HALF = 64
VDIM = 128
QKW = HEADS * 2 * HALF
HGW = 256
HGH = 64
PW = 256
POOL_WINDOWS = (2, 4, 8, 16)
DFF = 2816
THETA = 10000.0
IN_A = 3 * QKW + 6 * 256
CHUNK = 16
NEG = -1e30
LOG2E = 1.4426950408889634

F32 = jnp.float32
BF16 = jnp.bfloat16
VMEM_LIMIT = 56 * 1024 * 1024


def _const_spec(shape):
    nd = len(shape)
    return pl.BlockSpec(shape, lambda *_: (0,) * nd, pipeline_mode=pl.Buffered(1))


def _rms(x, gain):
    return x * lax.rsqrt(jnp.mean(x * x, axis=-1, keepdims=True) + EPS) * gain


def _sigmoid(x):
    return 1.0 / (1.0 + jnp.exp(-x))


def _dot(a, b):
    return jnp.dot(a, b, preferred_element_type=F32)


def _dot_nt(a, b):
    return lax.dot_general(a, b, (((1,), (1,)), ((), ())), preferred_element_type=F32)


def _dot_tn(a, b):
    return lax.dot_general(a, b, (((0,), (0,)), ((), ())), preferred_element_type=F32)


def _ada_kernel(c_ref, w_ref, b_ref, o_ref):
    c = c_ref[...]
    s = c * _sigmoid(c)
    o_ref[0] = jnp.dot(s, w_ref[0], preferred_element_type=F32,
                       precision=lax.Precision.HIGHEST) + b_ref[0]


def _ada(cvec, w_ada, b_ada):
    depth = w_ada.shape[0]
    nblk = (NMOD * D) // D
    return pl.pallas_call(
        _ada_kernel,
        grid=(depth, nblk),
        in_specs=[pl.BlockSpec((8, D), lambda l, j: (0, 0)),
                  pl.BlockSpec((1, D, D), lambda l, j: (l, 0, j)),
                  pl.BlockSpec((1, 1, D), lambda l, j: (l, 0, j))],
        out_specs=pl.BlockSpec((1, 8, D), lambda l, j: (l, 0, j)),
        out_shape=jax.ShapeDtypeStruct((depth, 8, NMOD * D), F32),
        name="ada",
    )(cvec, w_ada, b_ada.reshape(depth, 1, NMOD * D))


def _swiglu_half(h, m, shift_i, gain, w1_ref, w3_ref, w2_ref):
    x = _rms(h, gain) * (1.0 + m[shift_i + 1:shift_i + 2]) + m[shift_i:shift_i + 1]
    xb = x.astype(BF16)
    a = _dot(xb, w1_ref[...])
    b = _dot(xb, w3_ref[...])
    g = (a * _sigmoid(a) * b).astype(BF16)
    y = _dot(g, w2_ref[...])
    return h + 0.5 * m[shift_i + 2:shift_i + 3] * y


def _ffn_in_kernel(h_ref, mod_ref, n1_ref, nm_ref, w1_ref, w3_ref, w2_ref, win_ref, rc_ref, rs_ref,
                   h1_ref, qt_ref, k_ref, vt_ref, hq_ref, hf_ref, hi_ref, hg_ref, zp_ref, *, rope):
    h = h_ref[0]
    m = mod_ref[0]
    h1 = _swiglu_half(h, m, 0, n1_ref[...], w1_ref, w3_ref, w2_ref)
    h1_ref[0] = h1
    u = _rms(h1, nm_ref[...]) * (1.0 + m[4:5]) + m[3:4]
    z = _dot(u.astype(BF16), win_ref[...])
    q = z[:, 0:QKW]
    k = z[:, QKW:2 * QKW]
    v = z[:, 2 * QKW:3 * QKW]
    if rope:
        lane = lax.broadcasted_iota(jnp.int32, q.shape, 1)
        first = (lane % 32) < 16
        rc = jnp.concatenate([rc_ref[...]] * (QKW // 128), axis=1)
        rs = jnp.concatenate([rs_ref[...]] * (QKW // 128), axis=1)

        def rot(x):
            partner = jnp.where(first, pltpu.roll(x, QKW - 16, 1), pltpu.roll(x, 16, 1))
            return x * rc + partner * rs

        q = rot(q)
        k = rot(k)
    q = q * (HALF ** -0.5 * LOG2E)
    qt_ref[0] = q.T.astype(BF16)
    k_ref[0] = k.astype(BF16)
    vt_ref[0] = v.T.astype(BF16)
    o = 3 * QKW
    hq_ref[0] = z[:, o:o + 256]
    hf_ref[0, 0] = z[:, o + 256:o + 512]
    hf_ref[0, 1] = z[:, o + 512:o + 768]
    hi_ref[0] = z[:, o + 768:o + 1024]
    hg_ref[0] = z[:, o + 1024:o + 1280]
    zp_ref[0] = z[:, o + 1280:o + 1536]


def _ffn_in(h, mod, n1, nm, w1, w3, w2, win, rc, rs, *, rope, tm):
    b, t, _ = h.shape
    grid = (b, t // tm)
    tok = lambda w: pl.BlockSpec((1, tm, w), lambda bi, i: (bi, i, 0))
    tokt = pl.BlockSpec((1, QKW, tm), lambda bi, i: (bi, 0, i))
    out_shape = (
        jax.ShapeDtypeStruct((b, t, D), F32),
        jax.ShapeDtypeStruct((b, QKW, t), BF16),
        jax.ShapeDtypeStruct((b, t, QKW), BF16),
        jax.ShapeDtypeStruct((b, QKW, t), BF16),
        jax.ShapeDtypeStruct((b, t, 256), F32),
        jax.ShapeDtypeStruct((b, 2, t, 256), F32),
        jax.ShapeDtypeStruct((b, t, 256), F32),
        jax.ShapeDtypeStruct((b, t, 256), F32),
        jax.ShapeDtypeStruct((b, t, 256), F32),
    )
    out_specs = (tok(D), tokt, tok(QKW), tokt, tok(256),
                 pl.BlockSpec((1, 2, tm, 256), lambda bi, i: (bi, 0, i, 0)),
                 tok(256), tok(256), tok(256))
    in_specs = [tok(D),
                pl.BlockSpec((1, NMOD, D), lambda bi, i: (bi, 0, 0)),
                _const_spec((1, D)), _const_spec((1, D)),
                _const_spec((D, DFF)), _const_spec((D, DFF)), _const_spec((DFF, D)),
                _const_spec((D, IN_A)),
                pl.BlockSpec((tm, 128), lambda bi, i: (i, 0)),
                pl.BlockSpec((tm, 128), lambda bi, i: (i, 0))]
    return pl.pallas_call(
        functools.partial(_ffn_in_kernel, rope=rope),
        grid=grid, in_specs=in_specs, out_specs=out_specs, out_shape=out_shape,
        compiler_params=pltpu.CompilerParams(
            dimension_semantics=("parallel", "parallel"), vmem_limit_bytes=VMEM_LIMIT),
        name="ffn_in",
    )(h, mod, n1, nm, w1, w3, w2, win, rc, rs)


def _attn_kernel(*refs, tq, tk, n_main, has_extra, lam_init):
    if has_extra:
        (qt_ref, k_ref, vt_ref, kx_ref, vtx_ref, lam_ref, gain_ref, o_ref,
         acc1, acc2, s0, s1, p0, p1) = refs
    else:
        (qt_ref, k_ref, vt_ref, lam_ref, gain_ref, o_ref, acc1, acc2, s0, s1, p0, p1) = refs
    s_sc, p_sc = (s0, s1), (p0, p1)
    qt = qt_ref[0]
    row = lax.broadcasted_iota(jnp.int32, qt.shape, 0)
    zero = jnp.zeros_like(qt)
    qbd = jnp.concatenate([jnp.where(row < HALF, qt, zero), jnp.where(row >= HALF, qt, zero)], axis=1)
    acc1[...] = jnp.zeros_like(acc1)
    acc2[...] = jnp.zeros_like(acc2)
    p1[...] = jnp.zeros_like(p1)

    def score(kb, slot, rows):
        s = _dot(kb, qbd)
        s_sc[slot][0:rows, :] = s
        return jnp.max(s, axis=0, keepdims=True)

    def softmax(slot, rows, mx, m, l):
        m_new = jnp.maximum(m, mx)
        alpha = jnp.exp2(m - m_new)
        p = jnp.exp2(s_sc[slot][0:rows, :] - m_new)
        l = alpha * l + jnp.sum(p, axis=0, keepdims=True)
        p_sc[slot][0:rows, :] = p.astype(BF16)
        return m_new, l, alpha

    def pv(vtb, slot, rows, alpha):
        acc1[...] = acc1[...] * alpha[:, :tq] + _dot(vtb, p_sc[slot][0:rows, 0:tq])
        acc2[...] = acc2[...] * alpha[:, tq:] + _dot(vtb, p_sc[slot][0:rows, tq:2 * tq])

    def kchunk(c):
        return k_ref[0, pl.ds(pl.multiple_of(c * tk, tk), tk), :]

    def vchunk(c):
        return vt_ref[0, :, pl.ds(pl.multiple_of(c * tk, tk), tk)]

    def body(j, carry):
        m, l, mx, a_prev = carry
        c0 = 2 * j
        mx1 = score(kchunk(c0 + 1), 1, tk)
        m, l, a0 = softmax(0, tk, mx, m, l)
        pv(vchunk(jnp.maximum(c0 - 1, 0)), 1, tk, a_prev)
        mx2 = score(kchunk(jnp.minimum(c0 + 2, n_main - 1)), 0, tk)
        m, l, a1 = softmax(1, tk, mx1, m, l)
        pv(vchunk(c0), 0, tk, a0)
        return m, l, mx2, a1

    mx0 = score(kchunk(0), 0, tk)
    carry = (jnp.full((1, 2 * tq), NEG, F32), jnp.zeros((1, 2 * tq), F32), mx0,
             jnp.ones((1, 2 * tq), F32))
    m, l, _, a_last = lax.fori_loop(0, n_main // 2, body, carry)
    if has_extra:
        tx = kx_ref.shape[1]
        mxx = score(kx_ref[0], 0, tx)
        pv(vchunk(n_main - 1), 1, tk, a_last)
        m, l, ax = softmax(0, tx, mxx, m, l)
        pv(vtx_ref[0], 0, tx, ax)
    else:
        pv(vchunk(n_main - 1), 1, tk, a_last)

    lv = lam_ref[...]
    lam = (jnp.exp(jnp.sum(lv[0:1] * lv[1:2], axis=1, keepdims=True))
           - jnp.exp(jnp.sum(lv[2:3] * lv[3:4], axis=1, keepdims=True)) + lam_init)
    o = acc1[...] / l[:, :tq] - lam * (acc2[...] / l[:, tq:])
    o = o * lax.rsqrt(jnp.mean(o * o, axis=0, keepdims=True) + EPS) * gain_ref[...] * (1.0 - lam_init)
    o_ref[0] = o.T.astype(BF16)


def _attention(qt, k, vt, kx, vtx, lamv, gain, *, lam_init, tq, tk):
    b, _, t_q = qt.shape
    t_k = k.shape[1]
    has_extra = kx is not None
    assert t_k % (2 * tk) == 0 and (not has_extra or kx.shape[1] <= tk)
    grid = (b, HEADS, t_q // tq)
    in_specs = [pl.BlockSpec((1, VDIM, tq), lambda bi, hi, i: (bi, hi, i)),
                pl.BlockSpec((1, t_k, VDIM), lambda bi, hi, i: (bi, 0, hi)),
                pl.BlockSpec((1, VDIM, t_k), lambda bi, hi, i: (bi, hi, 0))]
    args = [qt, k, vt]
    if has_extra:
        t_x = kx.shape[1]
        in_specs += [pl.BlockSpec((1, t_x, VDIM), lambda bi, hi, i: (bi, 0, hi)),
                     pl.BlockSpec((1, VDIM, t_x), lambda bi, hi, i: (bi, hi, 0))]
        args += [kx, vtx]
    in_specs += [pl.BlockSpec((4, HALF), lambda bi, hi, i: (0, 0)),
                 pl.BlockSpec((VDIM, 1), lambda bi, hi, i: (0, 0))]
    args += [lamv, gain]
    return pl.pallas_call(
        functools.partial(_attn_kernel, tq=tq, tk=tk, n_main=t_k // tk, has_extra=has_extra,
                          lam_init=lam_init),
        grid=grid, in_specs=in_specs,
        out_specs=pl.BlockSpec((1, tq, VDIM), lambda bi, hi, i: (bi, i, hi)),
        out_shape=jax.ShapeDtypeStruct((b, t_q, HEADS * VDIM), BF16),
        scratch_shapes=[pltpu.VMEM((VDIM, tq), F32), pltpu.VMEM((VDIM, tq), F32),
                        pltpu.VMEM((tk, 2 * tq), F32), pltpu.VMEM((tk, 2 * tq), F32),
                        pltpu.VMEM((tk, 2 * tq), BF16), pltpu.VMEM((tk, 2 * tq), BF16)],
        compiler_params=pltpu.CompilerParams(
            dimension_semantics=("parallel", "parallel", "parallel"), vmem_limit_bytes=VMEM_LIMIT),
        name="diff_attn",
    )(*args)


def _chunk_cumsum(x, rev):
    n = x.shape[0]
    pos = lax.broadcasted_iota(jnp.int32, x.shape, 0) % CHUNK
    sh = 1
    while sh < CHUNK:
        if rev:
            x = x + jnp.where(pos + sh < CHUNK, pltpu.roll(x, n - sh, 0), 0.0)
        else:
            x = x + jnp.where(pos >= sh, pltpu.roll(x, sh, 0), 0.0)
        sh *= 2
    return x


def _hgrn_direction(zq, zf, zi, lbv, st_ref, o_ref, stall, dst, *, rev, tb):
    nc = tb // CHUNK
    q = zq * _sigmoid(zq)
    kk = (1.0 - lbv) * _sigmoid(-zf)
    log_lb = jnp.log(jnp.maximum(lbv, LB_MIN))
    log_sig = jnp.minimum(zf, 0.0) - jnp.log1p(jnp.exp(-jnp.abs(zf)))
    y = jnp.log1p(-lbv) + log_sig
    lf = jnp.maximum(log_lb, y) + jnp.log1p(jnp.exp(-jnp.abs(log_lb - y)))
    a = _chunk_cumsum(lf, rev)
    a3 = a.reshape(nc, CHUNK, HGW)
    end_row = 0 if rev else CHUNK - 1
    mid_row = CHUNK // 2 if rev else CHUNK // 2 - 1
    a_end = a3[:, end_row:end_row + 1, :]
    a_mid = a3[:, mid_row:mid_row + 1, :]
    q3 = q.reshape(nc, CHUNK, HGW)
    k3 = kk.reshape(nc, CHUNK, HGW)
    qe = (q3 * jnp.exp(a3)).astype(BF16)
    ke = (k3 * jnp.exp(a_end - a3)).astype(BF16)
    qm = (q3 * jnp.exp(a3 - a_mid)).reshape(tb, HGW)
    km = (k3 * jnp.exp(a_mid - a3)).reshape(tb, HGW).astype(BF16)
    vb = zi.astype(BF16)
    v3 = vb.reshape(nc, CHUNK, HGW)
    dec = jnp.exp(a_end)

    lane_head = lax.broadcasted_iota(jnp.int32, (1, HGW), 1) // HGH
    sub = 128
    r_i = lax.broadcasted_iota(jnp.int32, (sub, sub), 0)
    c_i = lax.broadcasted_iota(jnp.int32, (sub, sub), 1)
    same = (r_i // CHUNK) == (c_i // CHUNK)
    causal = jnp.logical_and(same, (c_i >= r_i) if rev else (c_i <= r_i))
    causal4 = jnp.concatenate([causal] * HEADS, axis=0)
    intra = []
    for g in range(tb // sub):
        sl = slice(g * sub, (g + 1) * sub)
        qg = qm[sl]
        qstack = jnp.concatenate(
            [jnp.where(lane_head == hh, qg, 0.0) for hh in range(HEADS)], axis=0).astype(BF16)
        sc = _dot_nt(qstack, km[sl])
        sc = jnp.where(causal4, sc, 0.0).astype(BF16)
        r = _dot(sc, vb[sl])
        og = jnp.zeros((sub, HGW), F32)
        for hh in range(HEADS):
            og = og + jnp.where(lane_head == hh, r[hh * sub:(hh + 1) * sub], 0.0)
        intra.append(og)
    o_intra = jnp.concatenate(intra, axis=0)

    head_mask = (lax.broadcasted_iota(jnp.int32, (HGW, HGW), 0) // HGH
                 == lax.broadcasted_iota(jnp.int32, (HGW, HGW), 1) // HGH)
    for c in range(nc):
        dst[c] = jnp.where(head_mask, _dot_tn(v3[c], ke[c]), 0.0)
    st = st_ref[...]
    order = range(nc - 1, -1, -1) if rev else range(nc)
    for c in order:
        stall[c] = st.astype(BF16)
        st = dec[c] * st + dst[c]
    st_ref[...] = st
    inter = [_dot_nt(qe[c], stall[c]) for c in range(nc)]
    o_ref[...] = o_intra + jnp.concatenate(inter, axis=0)


def _hgrn_kernel(zqf_ref, zff_ref, zif_ref, zqb_ref, zfb_ref, zib_ref, lb_ref, s0_ref,
                 of_ref, ob_ref, s_ref, stall_f, dst_f, stall_b, dst_b, *, tb):
    @pl.when(pl.program_id(1) == 0)
    def _():
        s_ref[...] = s0_ref[...]

    lb = lb_ref[...]
    _hgrn_direction(zqf_ref[0], zff_ref[0, 0], zif_ref[0], lb[0:1], s_ref.at[0, 0], of_ref.at[0],
                    stall_f, dst_f, rev=False, tb=tb)
    _hgrn_direction(zqb_ref[0], zfb_ref[0, 0], zib_ref[0], lb[1:2], s_ref.at[0, 1], ob_ref.at[0],
                    stall_b, dst_b, rev=True, tb=tb)


def _hgrn(zq, zf, zi, lb, s0, *, tb):
    b, t, _ = zq.shape
    nb = t // tb
    fwd = lambda bi, i: (bi, i, 0)
    bwd = lambda bi, i: (bi, nb - 1 - i, 0)
    in_specs = [pl.BlockSpec((1, tb, HGW), fwd),
                pl.BlockSpec((1, 1, tb, HGW), lambda bi, i: (bi, 0, i, 0)),
                pl.BlockSpec((1, tb, HGW), fwd),
                pl.BlockSpec((1, tb, HGW), bwd),
                pl.BlockSpec((1, 1, tb, HGW), lambda bi, i: (bi, 1, nb - 1 - i, 0)),
                pl.BlockSpec((1, tb, HGW), bwd),
                pl.BlockSpec((2, HGW), lambda bi, i: (0, 0)),
                pl.BlockSpec((1, 2, HGW, HGW), lambda bi, i: (bi, 0, 0, 0))]
    out_specs = (pl.BlockSpec((1, tb, HGW), fwd),
                 pl.BlockSpec((1, tb, HGW), bwd),
                 pl.BlockSpec((1, 2, HGW, HGW), lambda bi, i: (bi, 0, 0, 0)))
    out_shape = (jax.ShapeDtypeStruct((b, t, HGW), F32),
                 jax.ShapeDtypeStruct((b, t, HGW), F32),
                 jax.ShapeDtypeStruct((b, 2, HGW, HGW), F32))
    nc = tb // CHUNK
    of, ob, s_fin = pl.pallas_call(
        functools.partial(_hgrn_kernel, tb=tb),
        grid=(b, nb), in_specs=in_specs, out_specs=out_specs, out_shape=out_shape,
        scratch_shapes=[pltpu.VMEM((nc, HGW, HGW), BF16), pltpu.VMEM((nc, HGW, HGW), F32),
                        pltpu.VMEM((nc, HGW, HGW), BF16), pltpu.VMEM((nc, HGW, HGW), F32)],
        compiler_params=pltpu.CompilerParams(
            dimension_semantics=("parallel", "arbitrary"), vmem_limit_bytes=VMEM_LIMIT),
        name="hgrn",
    )(zq, zf, zi, zq, zf, zi, lb, s0)
    return of, ob, s_fin


def _merge_kernel(h_ref, mod_ref, nm_ref, n2_ref, oda_ref, of_ref, ob_ref, hg_ref,
                  zp_ref, zpp_ref, zpn_ref, hgn_ref, pw_ref, ps_ref,
                  wg_ref, wpa_ref, wph_ref, wpp_ref, wo_ref, w1_ref, w3_ref, w2_ref, fn_ref,
                  out_ref, *, tm, t_total, final):
    i = pl.program_id(1)
    h1 = h_ref[0]
    m = mod_ref[0]

    x = zp_ref[0]
    prev = jnp.where(i > 0, zpp_ref[0], 0.0)
    nxt = jnp.where(i < pl.num_programs(1) - 1, zpn_ref[0], 0.0)
    e = jnp.concatenate([prev, x, nxt], axis=0)
    n = tm + 16
    a2 = e[0:n - 1] + e[1:n]
    a4 = a2[0:n - 3] + a2[2:n - 1]
    a8 = a4[0:n - 7] + a4[4:n - 3]
    a16 = a8[0:n - 15] + a8[8:n - 7]
    sums = (a2[7:7 + tm], a4[6:6 + tm], a8[4:4 + tm], a16[0:tm])
    pos = i * tm + lax.broadcasted_iota(jnp.int32, (tm, PW), 0)
    group = lax.broadcasted_iota(jnp.int32, (tm, PW), 1) // 64
    mixed = jnp.zeros((tm, PW), F32)
    for g, w in enumerate(POOL_WINDOWS):
        cnt = (jnp.minimum(pos + w // 2, t_total) - jnp.maximum(pos - w // 2, 0)).astype(F32)
        mixed = jnp.where(group == g, sums[g] / cnt - x, mixed)
    o_pool = _dot(mixed.astype(BF16), pw_ref[...]) * ps_ref[...]

    o = of_ref[0] + ob_ref[0]
    o2 = o * o
    hi = o2.astype(BF16)
    lo = (o2 - hi.astype(F32)).astype(BF16)
    ones_bd = (lax.broadcasted_iota(jnp.int32, (HGW, HGW), 0) // HGH
               == lax.broadcasted_iota(jnp.int32, (HGW, HGW), 1) // HGH).astype(BF16)
    seg = _dot(hi, ones_bd) + _dot(lo, ones_bd)
    zg = hg_ref[0]
    o_hg = (o * lax.rsqrt(seg * (1.0 / HGH) + EPS) * hgn_ref[...]) * (zg * _sigmoid(zg))

    u = _rms(h1, nm_ref[...]) * (1.0 + m[4:5]) + m[3:4]
    gate = _sigmoid(_dot(u.astype(BF16), wg_ref[...]))
    y = (gate[:, 0:D] * _dot(oda_ref[0], wpa_ref[...])
         + gate[:, D:2 * D] * _dot(o_hg.astype(BF16), wph_ref[...])
         + gate[:, 2 * D:3 * D] * _dot(o_pool.astype(BF16), wpp_ref[...]))
    mix = _dot(y.astype(BF16), wo_ref[...])
    h2 = h1 + m[5:6] * mix
    h3 = _swiglu_half(h2, m, 6, n2_ref[...], w1_ref, w3_ref, w2_ref)
    if final:
        h3 = _rms(h3, fn_ref[...])
    out_ref[0] = h3


def _merge(h1, mod, nm, n2, oda, of, ob, hg, zp, hgn, pw, ps, wg, wpa, wph, wpp, wo, w1, w3, w2, fn,
           *, tm, final):
    b, t, _ = h1.shape
    nblk8 = t // 8
    r = tm // 8
    tok = lambda w: pl.BlockSpec((1, tm, w), lambda bi, i: (bi, i, 0))
    in_specs = [tok(D),
                pl.BlockSpec((1, NMOD, D), lambda bi, i: (bi, 0, 0)),
                _const_spec((1, D)), _const_spec((1, D)),
                tok(QKW),
                tok(HGW), tok(HGW),
                tok(HGW), tok(PW),
                pl.BlockSpec((1, 8, PW), lambda bi, i: (bi, jnp.maximum(i * r - 1, 0), 0)),
                pl.BlockSpec((1, 8, PW), lambda bi, i: (bi, jnp.minimum((i + 1) * r, nblk8 - 1), 0)),
                _const_spec((1, HGW)), _const_spec((PW, PW)), _const_spec((1, PW)),
                _const_spec((D, 3 * D)), _const_spec((QKW, D)), _const_spec((HGW, D)),
                _const_spec((PW, D)), _const_spec((D, D)),
                _const_spec((D, DFF)), _const_spec((D, DFF)), _const_spec((DFF, D)),
                _const_spec((1, D))]
    return pl.pallas_call(
        functools.partial(_merge_kernel, tm=tm, t_total=t, final=final),
        grid=(b, t // tm), in_specs=in_specs, out_specs=tok(D),
        out_shape=jax.ShapeDtypeStruct((b, t, D), F32),
        compiler_params=pltpu.CompilerParams(
            dimension_semantics=("parallel", "parallel"), vmem_limit_bytes=VMEM_LIMIT),
        name="merge_ffn",
    )(h1, mod, nm, n2, oda, of, ob, hg, zp, zp, zp, hgn, pw, ps, wg, wpa, wph, wpp, wo, w1, w3, w2, fn)


def _rope_tables(n):
    rows = n // GRID_WIDTH
    row = jnp.repeat(jnp.arange(rows, dtype=jnp.int32), GRID_WIDTH).astype(F32)
    col = jnp.tile(jnp.arange(GRID_WIDTH, dtype=jnp.int32), rows).astype(F32)
    axis_dim = HALF // 2
    inv_freq = THETA ** (-jnp.arange(0, axis_dim, 2, dtype=F32) / axis_dim)
    ang_r = row[:, None] * inv_freq[None, :]
    ang_c = col[:, None] * inv_freq[None, :]
    cos64 = jnp.concatenate([jnp.cos(ang_r)] * 2 + [jnp.cos(ang_c)] * 2, axis=1)
    sin64 = jnp.concatenate([-jnp.sin(ang_r), jnp.sin(ang_r), -jnp.sin(ang_c), jnp.sin(ang_c)], axis=1)
    return jnp.tile(cos64, (1, 2)), jnp.tile(sin64, (1, 2))


def _lower_bounds(logits):
    p = jax.nn.softmax(logits.astype(F32), axis=0)
    return jnp.cumsum(p, axis=0) - p[0:1]


def _block_diag(w):
    g, a, b = w.shape
    out = jnp.zeros((g * a, g * b), w.dtype)
    for i in range(g):
        out = out.at[i * a:(i + 1) * a, i * b:(i + 1) * b].set(w[i])
    return out


def kernel(x, c, ctx, c_ctx, w_ada, b_ada, norm_ffn1, norm_mix, norm_ffn2, ffn1_w1, ffn1_w3, ffn1_w2,
           ffn2_w1, ffn2_w3, ffn2_w2, w_in, da_lambda_q1, da_lambda_k1, da_lambda_q2, da_lambda_k2,
           da_subln, hg_lb_logits, hg_norm, pool_w, pool_scale, w_proj_da, w_proj_hg, w_proj_pool,
           w_out, final_norm):
    bsz, n, _ = x.shape
    n_ctx = ctx.shape[1]
    depth = w_ada.shape[0]
    tm = 256
    tq, tk = 256, 512
    tb = 256

    cvec = jnp.zeros((8, D), F32).at[0:bsz].set(c).at[bsz].set(c_ctx)
    mods = _ada(cvec, w_ada, b_ada).reshape(depth, 8, NMOD, D)
    rc, rs = _rope_tables(n)
    rc_c = jnp.zeros((n_ctx, 128), F32)
    lb_all = _lower_bounds(hg_lb_logits)
    row = lambda v: v.reshape(1, -1)
    bf = lambda w: w.astype(BF16)

    h, hc = x, ctx
    for l in range(depth):
        need_ctx = l < depth - 1
        lam_init = 0.8 - 0.6 * math.exp(-0.3 * l)
        m_lat = mods[l, 0:bsz]
        m_ctx = jnp.broadcast_to(mods[l, bsz][None], (bsz, NMOD, D))
        w1a, w3a, w2a = bf(ffn1_w1[l]), bf(ffn1_w3[l]), bf(ffn1_w2[l])
        w1b, w3b, w2b = bf(ffn2_w1[l]), bf(ffn2_w3[l]), bf(ffn2_w2[l])
        win_a = bf(w_in[l][:, :IN_A])
        w_gate = bf(w_in[l][:, IN_A:])
        lamv = jnp.stack([da_lambda_q1[l], da_lambda_k1[l], da_lambda_q2[l], da_lambda_k2[l]]).astype(F32)
        gain_da = da_subln[l].reshape(VDIM, 1)
        hgn = row(jnp.tile(hg_norm[l], HEADS))
        pw_bd = bf(_block_diag(pool_w[l]))
        merge_w = (hgn, pw_bd, row(pool_scale[l]), w_gate, bf(w_proj_da[l]), bf(w_proj_hg[l]),
                   bf(w_proj_pool[l]), bf(w_out[l]), w1b, w3b, w2b, row(final_norm))

        (hc1, qt_c, k_c, vt_c, hq_c, hf_c, hi_c, hg_c, zp_c) = _ffn_in(
            hc, m_ctx, row(norm_ffn1[l]), row(norm_mix[l]), w1a, w3a, w2a, win_a, rc_c, rc_c,
            rope=False, tm=min(tm, n_ctx))
        s0 = jnp.zeros((bsz, 2, HGW, HGW), F32)
        of_c, ob_c, s_ctx = _hgrn(hq_c, hf_c, hi_c, lb_all[l], s0, tb=min(tb, n_ctx))

        (h1, qt, k, vt, hq, hf, hi, hg, zp) = _ffn_in(
            h, m_lat, row(norm_ffn1[l]), row(norm_mix[l]), w1a, w3a, w2a, win_a, rc, rs,
            rope=True, tm=tm)
        o_da = _attention(qt, k, vt, k_c, vt_c, lamv, gain_da, lam_init=lam_init, tq=tq, tk=tk)
        of, ob, _ = _hgrn(hq, hf, hi, lb_all[l], s_ctx, tb=tb)
        h = _merge(h1, m_lat, row(norm_mix[l]), row(norm_ffn2[l]), o_da, of, ob, hg, zp, *merge_w,
                   tm=tm, final=not need_ctx)
        if need_ctx:
            o_dac = _attention(qt_c, k_c, vt_c, None, None, lamv, gain_da, lam_init=lam_init,
                               tq=min(tq, n_ctx), tk=min(tk, n_ctx // 2))
            hc = _merge(hc1, m_ctx, row(norm_mix[l]), row(norm_ffn2[l]), o_dac, of_c, ob_c, hg_c, zp_c,
                        *merge_w, tm=min(tm, n_ctx), final=False)
    return h
```

```python
import functools
import math

import jax
import jax.numpy as jnp
from jax import lax
from jax.experimental import pallas as pl
from jax.experimental.pallas import tpu as pltpu

D = 1024
GRID_WIDTH = 64
EPS = 1e-6
LB_MIN = 1e-20
NMOD = 9
HEADS = 4
HALF = 64
VDIM = 128
QKW = HEADS * 2 * HALF
HGW = 256
HGH = 64
PW = 256
POOL_WINDOWS = (2, 4, 8, 16)
DFF = 2816
THETA = 10000.0
IN_A = 3 * QKW + 6 * 256
CHUNK = 16
NEG = -1e30
LOG2E = 1.4426950408889634
BOUND_MARGIN = 1.01
L_FLOOR = 2.0 ** -86

F32 = jnp.float32
BF16 = jnp.bfloat16
VMEM_LIMIT = 56 * 1024 * 1024


def _const_spec(shape):
    nd = len(shape)
    return pl.BlockSpec(shape, lambda *_: (0,) * nd, pipeline_mode=pl.Buffered(1))


def _rms(x, gain):
    return x * lax.rsqrt(jnp.mean(x * x, axis=-1, keepdims=True) + EPS) * gain


def _sigmoid(x):
    return 1.0 / (1.0 + jnp.exp(-x))


def _dot(a, b):
    return jnp.dot(a, b, preferred_element_type=F32)


def _dot_nt(a, b):
    return lax.dot_general(a, b, (((1,), (1,)), ((), ())), preferred_element_type=F32)


def _dot_tn(a, b):
    return lax.dot_general(a, b, (((0,), (0,)), ((), ())), preferred_element_type=F32)


def _ada_kernel(c_ref, w_ref, b_ref, o_ref):
    c = c_ref[...]
    s = c * _sigmoid(c)
    o_ref[0] = jnp.dot(s, w_ref[0], preferred_element_type=F32,
                       precision=lax.Precision.HIGHEST) + b_ref[0]


def _ada(cvec, w_ada, b_ada):
    depth = w_ada.shape[0]
    nblk = (NMOD * D) // D
    return pl.pallas_call(
        _ada_kernel,
        grid=(depth, nblk),
        in_specs=[pl.BlockSpec((8, D), lambda l, j: (0, 0)),
                  pl.BlockSpec((1, D, D), lambda l, j: (l, 0, j)),
                  pl.BlockSpec((1, 1, D), lambda l, j: (l, 0, j))],
        out_specs=pl.BlockSpec((1, 8, D), lambda l, j: (l, 0, j)),
        out_shape=jax.ShapeDtypeStruct((depth, 8, NMOD * D), F32),
        name="ada",
    )(cvec, w_ada, b_ada.reshape(depth, 1, NMOD * D))


def _swiglu_half(h, m, shift_i, gain, w1_ref, w3_ref, w2_ref):
    x = _rms(h, gain) * (1.0 + m[shift_i + 1:shift_i + 2]) + m[shift_i:shift_i + 1]
    xb = x.astype(BF16)
    a = _dot(xb, w1_ref[...])
    b = _dot(xb, w3_ref[...])
    g = (a * _sigmoid(a) * b).astype(BF16)
    y = _dot(g, w2_ref[...])
    return h + 0.5 * m[shift_i + 2:shift_i + 3] * y


def _ffn_in_kernel(h_ref, mod_ref, n1_ref, nm_ref, w1_ref, w3_ref, w2_ref, win_ref, rc_ref, rs_ref,
                   h1_ref, qt_ref, k_ref, vt_ref, hq_ref, hf_ref, hi_ref, hg_ref, zp_ref, *, rope):
    h = h_ref[0]
    m = mod_ref[0]
    h1 = _swiglu_half(h, m, 0, n1_ref[...], w1_ref, w3_ref, w2_ref)
    h1_ref[0] = h1
    u = _rms(h1, nm_ref[...]) * (1.0 + m[4:5]) + m[3:4]
    z = _dot(u.astype(BF16), win_ref[...])
    q = z[:, 0:QKW]
    k = z[:, QKW:2 * QKW]
    v = z[:, 2 * QKW:3 * QKW]
    if rope:
        lane = lax.broadcasted_iota(jnp.int32, q.shape, 1)
        first = (lane % 32) < 16
        rc = jnp.concatenate([rc_ref[...]] * (QKW // 128), axis=1)
        rs = jnp.concatenate([rs_ref[...]] * (QKW // 128), axis=1)

        def rot(x):
            partner = jnp.where(first, pltpu.roll(x, QKW - 16, 1), pltpu.roll(x, 16, 1))
            return x * rc + partner * rs

        q = rot(q)
        k = rot(k)
    q = q * (HALF ** -0.5 * LOG2E)
    qt_ref[0] = q.T.astype(BF16)
    k_ref[0] = k.astype(BF16)
    vt_ref[0] = v.T.astype(BF16)
    o = 3 * QKW
    hq_ref[0] = z[:, o:o + 256]
    hf_ref[0, 0] = z[:, o + 256:o + 512]
    hf_ref[0, 1] = z[:, o + 512:o + 768]
    hi_ref[0] = z[:, o + 768:o + 1024]
    hg_ref[0] = z[:, o + 1024:o + 1280]
    zp_ref[0] = z[:, o + 1280:o + 1536]


def _ffn_in(h, mod, n1, nm, w1, w3, w2, win, rc, rs, *, rope, tm):
    b, t, _ = h.shape
    grid = (b, t // tm)
    tok = lambda w: pl.BlockSpec((1, tm, w), lambda bi, i: (bi, i, 0))
    tokt = pl.BlockSpec((1, QKW, tm), lambda bi, i: (bi, 0, i))
    out_shape = (
        jax.ShapeDtypeStruct((b, t, D), F32),
        jax.ShapeDtypeStruct((b, QKW, t), BF16),
        jax.ShapeDtypeStruct((b, t, QKW), BF16),
        jax.ShapeDtypeStruct((b, QKW, t), BF16),
        jax.ShapeDtypeStruct((b, t, 256), F32),
        jax.ShapeDtypeStruct((b, 2, t, 256), F32),
        jax.ShapeDtypeStruct((b, t, 256), F32),
        jax.ShapeDtypeStruct((b, t, 256), F32),
        jax.ShapeDtypeStruct((b, t, 256), F32),
    )
    out_specs = (tok(D), tokt, tok(QKW), tokt, tok(256),
                 pl.BlockSpec((1, 2, tm, 256), lambda bi, i: (bi, 0, i, 0)),
                 tok(256), tok(256), tok(256))
    in_specs = [tok(D),
                pl.BlockSpec((1, NMOD, D), lambda bi, i: (bi, 0, 0)),
                _const_spec((1, D)), _const_spec((1, D)),
                _const_spec((D, DFF)), _const_spec((D, DFF)), _const_spec((DFF, D)),
                _const_spec((D, IN_A)),
                pl.BlockSpec((tm, 128), lambda bi, i: (i, 0)),
                pl.BlockSpec((tm, 128), lambda bi, i: (i, 0))]
    return pl.pallas_call(
        functools.partial(_ffn_in_kernel, rope=rope),
        grid=grid, in_specs=in_specs, out_specs=out_specs, out_shape=out_shape,
        compiler_params=pltpu.CompilerParams(
            dimension_semantics=("parallel", "parallel"), vmem_limit_bytes=VMEM_LIMIT),
        name="ffn_in",
    )(h, mod, n1, nm, w1, w3, w2, win, rc, rs)


def _attn_kernel(*refs, tq, tk, n_main, has_extra, lam_init):
    if has_extra:
        (qt_ref, k_ref, vt_ref, kx_ref, vtx_ref, lam_ref, gain_ref, o_ref,
         acc1, acc2, l_sc, kmax) = refs
    else:
        (qt_ref, k_ref, vt_ref, lam_ref, gain_ref, o_ref, acc1, acc2, l_sc, kmax) = refs
        kx_ref = vtx_ref = None
    qt = qt_ref[0]
    row = lax.broadcasted_iota(jnp.int32, qt.shape, 0)
    zero = jnp.zeros_like(qt)
    qbd = jnp.concatenate([jnp.where(row < HALF, qt, zero), jnp.where(row >= HALF, qt, zero)], axis=1)

    def kchunk(c):
        return k_ref[0, pl.ds(pl.multiple_of(c * tk, tk), tk), :]

    def vchunk(c):
        return vt_ref[0, :, pl.ds(pl.multiple_of(c * tk, tk), tk)]

    @pl.when(pl.program_id(2) == 0)
    def _():
        lane = lax.broadcasted_iota(jnp.int32, (16, VDIM), 1)
        r16 = lax.broadcasted_iota(jnp.int32, (16, VDIM), 0)
        sel = jnp.where((r16 == 0) & (lane < HALF) | (r16 == 1) & (lane >= HALF), 1.0, 0.0).astype(BF16)

        def sq_norms(kb):
            kf = kb.astype(F32)
            return _dot_nt(sel, (kf * kf).astype(BF16))

        n2 = lax.fori_loop(0, n_main, lambda c, mx: jnp.maximum(mx, sq_norms(kchunk(c))),
                           jnp.zeros((16, tk), F32))
        top = jnp.max(n2, axis=1, keepdims=True)
        if has_extra:
            top = jnp.maximum(top, jnp.max(sq_norms(kx_ref[0]), axis=1, keepdims=True))
        kmax[...] = jnp.broadcast_to(jnp.sqrt(top[0:8]), kmax.shape)

    qf = qt.astype(F32)
    q1 = jnp.sqrt(jnp.sum(jnp.where(row < HALF, qf * qf, 0.0), axis=0, keepdims=True))
    q2 = jnp.sqrt(jnp.sum(jnp.where(row >= HALF, qf * qf, 0.0), axis=0, keepdims=True))
    km = kmax[...]
    bound = jnp.concatenate([q1 * km[0:1, 0:1], q2 * km[1:2, 0:1]], axis=1) * BOUND_MARGIN
    acc1[...] = jnp.zeros_like(acc1)
    acc2[...] = jnp.zeros_like(acc2)

    def fast_step(kb, vtb, l):
        p = jnp.exp2(_dot(kb, qbd) - bound)
        pb = p.astype(BF16)
        acc1[...] += _dot(vtb, pb[:, :tq])
        acc2[...] += _dot(vtb, pb[:, tq:])
        return l + jnp.sum(p, axis=0, keepdims=True)

    l = lax.fori_loop(0, n_main, lambda c, l: fast_step(kchunk(c), vchunk(c), l),
                      jnp.zeros((1, 2 * tq), F32), unroll=2)
    if has_extra:
        l = fast_step(kx_ref[0], vtx_ref[0], l)
    l_sc[...] = jnp.broadcast_to(l, l_sc.shape)

    @pl.when(jnp.logical_not(jnp.min(l) >= L_FLOOR))
    def _():
        acc1[...] = jnp.zeros_like(acc1)
        acc2[...] = jnp.zeros_like(acc2)

        def safe_step(kb, vtb, carry):
            m, ls = carry
            s = _dot(kb, qbd)
            m_new = jnp.maximum(m, jnp.max(s, axis=0, keepdims=True))
            alpha = jnp.exp2(m - m_new)
            p = jnp.exp2(s - m_new)
            pb = p.astype(BF16)
            acc1[...] = acc1[...] * alpha[:, :tq] + _dot(vtb, pb[:, :tq])
            acc2[...] = acc2[...] * alpha[:, tq:] + _dot(vtb, pb[:, tq:])
            return m_new, alpha * ls + jnp.sum(p, axis=0, keepdims=True)

        carry = (jnp.full((1, 2 * tq), NEG, F32), jnp.zeros((1, 2 * tq), F32))
        carry = lax.fori_loop(0, n_main, lambda c, cr: safe_step(kchunk(c), vchunk(c), cr), carry)
        if has_extra:
            carry = safe_step(kx_ref[0], vtx_ref[0], carry)
        l_sc[...] = jnp.broadcast_to(carry[1], l_sc.shape)

    l = l_sc[0:1, :]

    lv = lam_ref[...]
    lam = (jnp.exp(jnp.sum(lv[0:1] * lv[1:2], axis=1, keepdims=True))
           - jnp.exp(jnp.sum(lv[2:3] * lv[3:4], axis=1, keepdims=True)) + lam_init)
    o = acc1[...] / l[:, :tq] - lam * (acc2[...] / l[:, tq:])
    o = o * lax.rsqrt(jnp.mean(o * o, axis=0, keepdims=True) + EPS) * gain_ref[...] * (1.0 - lam_init)
    o_ref[0] = o.T.astype(BF16)


def _attention(qt, k, vt, kx, vtx, lamv, gain, *, lam_init, tq, tk):
    b, _, t_q = qt.shape
    t_k = k.shape[1]
    has_extra = kx is not None
    assert t_k % (2 * tk) == 0
    grid = (b, HEADS, t_q // tq)
    in_specs = [pl.BlockSpec((1, VDIM, tq), lambda bi, hi, i: (bi, hi, i)),
                pl.BlockSpec((1, t_k, VDIM), lambda bi, hi, i: (bi, 0, hi)),
                pl.BlockSpec((1, VDIM, t_k), lambda bi, hi, i: (bi, hi, 0))]
    args = [qt, k, vt]
    if has_extra:
        t_x = kx.shape[1]
        in_specs += [pl.BlockSpec((1, t_x, VDIM), lambda bi, hi, i: (bi, 0, hi)),
                     pl.BlockSpec((1, VDIM, t_x), lambda bi, hi, i: (bi, hi, 0))]
        args += [kx, vtx]
    in_specs += [pl.BlockSpec((4, HALF), lambda bi, hi, i: (0, 0)),
                 pl.BlockSpec((VDIM, 1), lambda bi, hi, i: (0, 0))]
    args += [lamv, gain]
    return pl.pallas_call(
        functools.partial(_attn_kernel, tq=tq, tk=tk, n_main=t_k // tk, has_extra=has_extra,
                          lam_init=lam_init),
        grid=grid, in_specs=in_specs,
        out_specs=pl.BlockSpec((1, tq, VDIM), lambda bi, hi, i: (bi, i, hi)),
        out_shape=jax.ShapeDtypeStruct((b, t_q, HEADS * VDIM), BF16),
        scratch_shapes=[pltpu.VMEM((VDIM, tq), F32), pltpu.VMEM((VDIM, tq), F32),
                        pltpu.VMEM((8, 2 * tq), F32), pltpu.VMEM((8, 128), F32)],
        compiler_params=pltpu.CompilerParams(
            dimension_semantics=("parallel", "parallel", "arbitrary"), vmem_limit_bytes=VMEM_LIMIT),
        name="diff_attn",
    )(*args)


def _chunk_cumsum(x, rev):
    n = x.shape[0]
    pos = lax.broadcasted_iota(jnp.int32, x.shape, 0) % CHUNK
    sh = 1
    while sh < CHUNK:
        if rev:
            x = x + jnp.where(pos + sh < CHUNK, pltpu.roll(x, n - sh, 0), 0.0)
        else:
            x = x + jnp.where(pos >= sh, pltpu.roll(x, sh, 0), 0.0)
        sh *= 2
    return x


def _hgrn_direction(zq, zf, zi, lbv, st_ref, o_ref, stall, dst, *, rev, tb):
    nc = tb // CHUNK
    q = zq * _sigmoid(zq)
    kk = (1.0 - lbv) * _sigmoid(-zf)
    log_lb = jnp.log(jnp.maximum(lbv, LB_MIN))
    log_sig = jnp.minimum(zf, 0.0) - jnp.log1p(jnp.exp(-jnp.abs(zf)))
    y = jnp.log1p(-lbv) + log_sig
    lf = jnp.maximum(log_lb, y) + jnp.log1p(jnp.exp(-jnp.abs(log_lb - y)))
    a = _chunk_cumsum(lf, rev)
    a3 = a.reshape(nc, CHUNK, HGW)
    end_row = 0 if rev else CHUNK - 1
    mid_row = CHUNK // 2 if rev else CHUNK // 2 - 1
    a_end = a3[:, end_row:end_row + 1, :]
    a_mid = a3[:, mid_row:mid_row + 1, :]
    q3 = q.reshape(nc, CHUNK, HGW)
    k3 = kk.reshape(nc, CHUNK, HGW)
    qe = (q3 * jnp.exp(a3)).astype(BF16)
    ke = (k3 * jnp.exp(a_end - a3)).astype(BF16)
    qm = (q3 * jnp.exp(a3 - a_mid)).reshape(tb, HGW)
    km = (k3 * jnp.exp(a_mid - a3)).reshape(tb, HGW).astype(BF16)
    vb = zi.astype(BF16)
    v3 = vb.reshape(nc, CHUNK, HGW)
    dec = jnp.exp(a_end)

    lane_head = lax.broadcasted_iota(jnp.int32, (1, HGW), 1) // HGH
    sub = 128
    r_i = lax.broadcasted_iota(jnp.int32, (sub, sub), 0)
    c_i = lax.broadcasted_iota(jnp.int32, (sub, sub), 1)
    same = (r_i // CHUNK) == (c_i // CHUNK)
    causal = jnp.logical_and(same, (c_i >= r_i) if rev else (c_i <= r_i))
    causal4 = jnp.concatenate([causal] * HEADS, axis=0)
    intra = []
    for g in range(tb // sub):
        sl = slice(g * sub, (g + 1) * sub)
        qg = qm[sl]
        qstack = jnp.concatenate(
            [jnp.where(lane_head == hh, qg, 0.0) for hh in range(HEADS)], axis=0).astype(BF16)
        sc = _dot_nt(qstack, km[sl])
        sc = jnp.where(causal4, sc, 0.0).astype(BF16)
        r = _dot(sc, vb[sl])
        og = jnp.zeros((sub, HGW), F32)
        for hh in range(HEADS):
            og = og + jnp.where(lane_head == hh, r[hh * sub:(hh + 1) * sub], 0.0)
        intra.append(og)
    o_intra = jnp.concatenate(intra, axis=0)

    head_mask = (lax.broadcasted_iota(jnp.int32, (HGW, HGW), 0) // HGH
                 == lax.broadcasted_iota(jnp.int32, (HGW, HGW), 1) // HGH)
    for c in range(nc):
        dst[c] = jnp.where(head_mask, _dot_tn(v3[c], ke[c]), 0.0)
    st = st_ref[...]
    order = range(nc - 1, -1, -1) if rev else range(nc)
    for c in order:
        stall[c] = st.astype(BF16)
        st = dec[c] * st + dst[c]
    st_ref[...] = st
    inter = [_dot_nt(qe[c], stall[c]) for c in range(nc)]
    o_ref[...] = o_intra + jnp.concatenate(inter, axis=0)


def _hgrn_kernel(zqf_ref, zff_ref, zif_ref, zqb_ref, zfb_ref, zib_ref, lb_ref, s0_ref,
                 of_ref, ob_ref, s_ref, stall_f, dst_f, stall_b, dst_b, *, tb):
    @pl.when(pl.program_id(1) == 0)
    def _():
        s_ref[...] = s0_ref[...]

    lb = lb_ref[...]
    _hgrn_direction(zqf_ref[0], zff_ref[0, 0], zif_ref[0], lb[0:1], s_ref.at[0, 0], of_ref.at[0],
                    stall_f, dst_f, rev=False, tb=tb)
    _hgrn_direction(zqb_ref[0], zfb_ref[0, 0], zib_ref[0], lb[1:2], s_ref.at[0, 1], ob_ref.at[0],
                    stall_b, dst_b, rev=True, tb=tb)


def _hgrn(zq, zf, zi, lb, s0, *, tb):
    b, t, _ = zq.shape
    nb = t // tb
    fwd = lambda bi, i: (bi, i, 0)
    bwd = lambda bi, i: (bi, nb - 1 - i, 0)
    in_specs = [pl.BlockSpec((1, tb, HGW), fwd),
                pl.BlockSpec((1, 1, tb, HGW), lambda bi, i: (bi, 0, i, 0)),
                pl.BlockSpec((1, tb, HGW), fwd),
                pl.BlockSpec((1, tb, HGW), bwd),
                pl.BlockSpec((1, 1, tb, HGW), lambda bi, i: (bi, 1, nb - 1 - i, 0)),
                pl.BlockSpec((1, tb, HGW), bwd),
                pl.BlockSpec((2, HGW), lambda bi, i: (0, 0)),
                pl.BlockSpec((1, 2, HGW, HGW), lambda bi, i: (bi, 0, 0, 0))]
    out_specs = (pl.BlockSpec((1, tb, HGW), fwd),
                 pl.BlockSpec((1, tb, HGW), bwd),
                 pl.BlockSpec((1, 2, HGW, HGW), lambda bi, i: (bi, 0, 0, 0)))
    out_shape = (jax.ShapeDtypeStruct((b, t, HGW), F32),
                 jax.ShapeDtypeStruct((b, t, HGW), F32),
                 jax.ShapeDtypeStruct((b, 2, HGW, HGW), F32))
    nc = tb // CHUNK
    of, ob, s_fin = pl.pallas_call(
        functools.partial(_hgrn_kernel, tb=tb),
        grid=(b, nb), in_specs=in_specs, out_specs=out_specs, out_shape=out_shape,
        scratch_shapes=[pltpu.VMEM((nc, HGW, HGW), BF16), pltpu.VMEM((nc, HGW, HGW), F32),
                        pltpu.VMEM((nc, HGW, HGW), BF16), pltpu.VMEM((nc, HGW, HGW), F32)],
        compiler_params=pltpu.CompilerParams(
            dimension_semantics=("parallel", "arbitrary"), vmem_limit_bytes=VMEM_LIMIT),
        name="hgrn",
    )(zq, zf, zi, zq, zf, zi, lb, s0)
    return of, ob, s_fin


def _merge_kernel(h_ref, mod_ref, nm_ref, n2_ref, oda_ref, of_ref, ob_ref, hg_ref,
                  zp_ref, zpp_ref, zpn_ref, hgn_ref, pw_ref, ps_ref,
                  wg_ref, wpa_ref, wph_ref, wpp_ref, wo_ref, w1_ref, w3_ref, w2_ref, fn_ref,
                  out_ref, *, tm, t_total, final):
    i = pl.program_id(1)
    h1 = h_ref[0]
    m = mod_ref[0]

    x = zp_ref[0]
    prev = jnp.where(i > 0, zpp_ref[0], 0.0)
    nxt = jnp.where(i < pl.num_programs(1) - 1, zpn_ref[0], 0.0)
    e = jnp.concatenate([prev, x, nxt], axis=0)
    n = tm + 16
    a2 = e[0:n - 1] + e[1:n]
    a4 = a2[0:n - 3] + a2[2:n - 1]
    a8 = a4[0:n - 7] + a4[4:n - 3]
    a16 = a8[0:n - 15] + a8[8:n - 7]
    sums = (a2[7:7 + tm], a4[6:6 + tm], a8[4:4 + tm], a16[0:tm])
    pos = i * tm + lax.broadcasted_iota(jnp.int32, (tm, PW), 0)
    group = lax.broadcasted_iota(jnp.int32, (tm, PW), 1) // 64
    mixed = jnp.zeros((tm, PW), F32)
    for g, w in enumerate(POOL_WINDOWS):
        cnt = (jnp.minimum(pos + w // 2, t_total) - jnp.maximum(pos - w // 2, 0)).astype(F32)
        mixed = jnp.where(group == g, sums[g] / cnt - x, mixed)
    o_pool = _dot(mixed.astype(BF16), pw_ref[...]) * ps_ref[...]

    o = of_ref[0] + ob_ref[0]
    o2 = o * o
    hi = o2.astype(BF16)
    lo = (o2 - hi.astype(F32)).astype(BF16)
    ones_bd = (lax.broadcasted_iota(jnp.int32, (HGW, HGW), 0) // HGH
               == lax.broadcasted_iota(jnp.int32, (HGW, HGW), 1) // HGH).astype(BF16)
    seg = _dot(hi, ones_bd) + _dot(lo, ones_bd)
    zg = hg_ref[0]
    o_hg = (o * lax.rsqrt(seg * (1.0 / HGH) + EPS) * hgn_ref[...]) * (zg * _sigmoid(zg))

    u = _rms(h1, nm_ref[...]) * (1.0 + m[4:5]) + m[3:4]
    gate = _sigmoid(_dot(u.astype(BF16), wg_ref[...]))
    y = (gate[:, 0:D] * _dot(oda_ref[0], wpa_ref[...])
         + gate[:, D:2 * D] * _dot(o_hg.astype(BF16), wph_ref[...])
         + gate[:, 2 * D:3 * D] * _dot(o_pool.astype(BF16), wpp_ref[...]))
    mix = _dot(y.astype(BF16), wo_ref[...])
    h2 = h1 + m[5:6] * mix
    h3 = _swiglu_half(h2, m, 6, n2_ref[...], w1_ref, w3_ref, w2_ref)
    if final:
        h3 = _rms(h3, fn_ref[...])
    out_ref[0] = h3


def _merge(h1, mod, nm, n2, oda, of, ob, hg, zp, hgn, pw, ps, wg, wpa, wph, wpp, wo, w1, w3, w2, fn,
           *, tm, final):
    b, t, _ = h1.shape
    nblk8 = t // 8
    r = tm // 8
    tok = lambda w: pl.BlockSpec((1, tm, w), lambda bi, i: (bi, i, 0))
    in_specs = [tok(D),
                pl.BlockSpec((1, NMOD, D), lambda bi, i: (bi, 0, 0)),
                _const_spec((1, D)), _const_spec((1, D)),
                tok(QKW),
                tok(HGW), tok(HGW),
                tok(HGW), tok(PW),
                pl.BlockSpec((1, 8, PW), lambda bi, i: (bi, jnp.maximum(i * r - 1, 0), 0)),
                pl.BlockSpec((1, 8, PW), lambda bi, i: (bi, jnp.minimum((i + 1) * r, nblk8 - 1), 0)),
                _const_spec((1, HGW)), _const_spec((PW, PW)), _const_spec((1, PW)),
                _const_spec((D, 3 * D)), _const_spec((QKW, D)), _const_spec((HGW, D)),
                _const_spec((PW, D)), _const_spec((D, D)),
                _const_spec((D, DFF)), _const_spec((D, DFF)), _const_spec((DFF, D)),
                _const_spec((1, D))]
    return pl.pallas_call(
        functools.partial(_merge_kernel, tm=tm, t_total=t, final=final),
        grid=(b, t // tm), in_specs=in_specs, out_specs=tok(D),
        out_shape=jax.ShapeDtypeStruct((b, t, D), F32),
        compiler_params=pltpu.CompilerParams(
            dimension_semantics=("parallel", "parallel"), vmem_limit_bytes=VMEM_LIMIT),
        name="merge_ffn",
    )(h1, mod, nm, n2, oda, of, ob, hg, zp, zp, zp, hgn, pw, ps, wg, wpa, wph, wpp, wo, w1, w3, w2, fn)


def _rope_tables(n):
    rows = n // GRID_WIDTH
    row = jnp.repeat(jnp.arange(rows, dtype=jnp.int32), GRID_WIDTH).astype(F32)
    col = jnp.tile(jnp.arange(GRID_WIDTH, dtype=jnp.int32), rows).astype(F32)
    axis_dim = HALF // 2
    inv_freq = THETA ** (-jnp.arange(0, axis_dim, 2, dtype=F32) / axis_dim)
    ang_r = row[:, None] * inv_freq[None, :]
    ang_c = col[:, None] * inv_freq[None, :]
    cos64 = jnp.concatenate([jnp.cos(ang_r)] * 2 + [jnp.cos(ang_c)] * 2, axis=1)
    sin64 = jnp.concatenate([-jnp.sin(ang_r), jnp.sin(ang_r), -jnp.sin(ang_c), jnp.sin(ang_c)], axis=1)
    return jnp.tile(cos64, (1, 2)), jnp.tile(sin64, (1, 2))


def _lower_bounds(logits):
    p = jax.nn.softmax(logits.astype(F32), axis=0)
    return jnp.cumsum(p, axis=0) - p[0:1]


def _block_diag(w):
    g, a, b = w.shape
    out = jnp.zeros((g * a, g * b), w.dtype)
    for i in range(g):
        out = out.at[i * a:(i + 1) * a, i * b:(i + 1) * b].set(w[i])
    return out


def kernel(x, c, ctx, c_ctx, w_ada, b_ada, norm_ffn1, norm_mix, norm_ffn2, ffn1_w1, ffn1_w3, ffn1_w2,
           ffn2_w1, ffn2_w3, ffn2_w2, w_in, da_lambda_q1, da_lambda_k1, da_lambda_q2, da_lambda_k2,
           da_subln, hg_lb_logits, hg_norm, pool_w, pool_scale, w_proj_da, w_proj_hg, w_proj_pool,
           w_out, final_norm):
    bsz, n, _ = x.shape
    n_ctx = ctx.shape[1]
    depth = w_ada.shape[0]
    tm = 256
    tq, tk = 256, 1024
    tb = 256

    cvec = jnp.zeros((8, D), F32).at[0:bsz].set(c).at[bsz].set(c_ctx)
    mods = _ada(cvec, w_ada, b_ada).reshape(depth, 8, NMOD, D)
    rc, rs = _rope_tables(n)
    rc_c = jnp.zeros((n_ctx, 128), F32)
    lb_all = _lower_bounds(hg_lb_logits)
    row = lambda v: v.reshape(1, -1)
    bf = lambda w: w.astype(BF16)

    h, hc = x, ctx
    for l in range(depth):
        need_ctx = l < depth - 1
        lam_init = 0.8 - 0.6 * math.exp(-0.3 * l)
        m_lat = mods[l, 0:bsz]
        m_ctx = jnp.broadcast_to(mods[l, bsz][None], (bsz, NMOD, D))
        w1a, w3a, w2a = bf(ffn1_w1[l]), bf(ffn1_w3[l]), bf(ffn1_w2[l])
        w1b, w3b, w2b = bf(ffn2_w1[l]), bf(ffn2_w3[l]), bf(ffn2_w2[l])
        win_a = bf(w_in[l][:, :IN_A])
        w_gate = bf(w_in[l][:, IN_A:])
        lamv = jnp.stack([da_lambda_q1[l], da_lambda_k1[l], da_lambda_q2[l], da_lambda_k2[l]]).astype(F32)
        gain_da = da_subln[l].reshape(VDIM, 1)
        hgn = row(jnp.tile(hg_norm[l], HEADS))
        pw_bd = bf(_block_diag(pool_w[l]))
        merge_w = (hgn, pw_bd, row(pool_scale[l]), w_gate, bf(w_proj_da[l]), bf(w_proj_hg[l]),
                   bf(w_proj_pool[l]), bf(w_out[l]), w1b, w3b, w2b, row(final_norm))

        (hc1, qt_c, k_c, vt_c, hq_c, hf_c, hi_c, hg_c, zp_c) = _ffn_in(
            hc, m_ctx, row(norm_ffn1[l]), row(norm_mix[l]), w1a, w3a, w2a, win_a, rc_c, rc_c,
            rope=False, tm=min(tm, n_ctx))
        s0 = jnp.zeros((bsz, 2, HGW, HGW), F32)
        of_c, ob_c, s_ctx = _hgrn(hq_c, hf_c, hi_c, lb_all[l], s0, tb=min(tb, n_ctx))

        (h1, qt, k, vt, hq, hf, hi, hg, zp) = _ffn_in(
            h, m_lat, row(norm_ffn1[l]), row(norm_mix[l]), w1a, w3a, w2a, win_a, rc, rs,
            rope=True, tm=tm)
        o_da = _attention(qt, k, vt, k_c, vt_c, lamv, gain_da, lam_init=lam_init, tq=tq, tk=tk)
        of, ob, _ = _hgrn(hq, hf, hi, lb_all[l], s_ctx, tb=tb)
        h = _merge(h1, m_lat, row(norm_mix[l]), row(norm_ffn2[l]), o_da, of, ob, hg, zp, *merge_w,
                   tm=tm, final=not need_ctx)
        if need_ctx:
            o_dac = _attention(qt_c, k_c, vt_c, None, None, lamv, gain_da, lam_init=lam_init,
                               tq=min(tq, n_ctx), tk=min(tk, n_ctx // 2))
            hc = _merge(hc1, m_ctx, row(norm_mix[l]), row(norm_ffn2[l]), o_dac, of_c, ob_c, hg_c, zp_c,
                        *merge_w, tm=min(tm, n_ctx), final=False)
    return h
```

```python
import functools
import math

import jax
import jax.numpy as jnp
from jax import lax
from jax.experimental import pallas as pl
from jax.experimental.pallas import tpu as pltpu

D = 1024
GRID_WIDTH = 64
EPS = 1e-6
LB_MIN = 1e-20
NMOD = 9
HEADS = 4
HALF = 64
VDIM = 128
QKW = HEADS * 2 * HALF
HGW = 256
HGH = 64
PW = 256
POOL_WINDOWS = (2, 4, 8, 16)
DFF = 2816
THETA = 10000.0
IN_A = 3 * QKW + 6 * 256
CHUNK = 16
NEG = -1e30
LOG2E = 1.4426950408889634
BOUND_MARGIN = 1.01
L_FLOOR = 2.0 ** -86

F32 = jnp.float32
BF16 = jnp.bfloat16
VMEM_LIMIT = 56 * 1024 * 1024


def _const_spec(shape):
    nd = len(shape)
    return pl.BlockSpec(shape, lambda *_: (0,) * nd, pipeline_mode=pl.Buffered(1))


def _rms(x, gain):
    return x * lax.rsqrt(jnp.mean(x * x, axis=-1, keepdims=True) + EPS) * gain


def _sigmoid(x):
    return 1.0 / (1.0 + jnp.exp(-x))


def _dot(a, b):
    return jnp.dot(a, b, preferred_element_type=F32)


def _dot_nt(a, b):
    return lax.dot_general(a, b, (((1,), (1,)), ((), ())), preferred_element_type=F32)


def _dot_tn(a, b):
    return lax.dot_general(a, b, (((0,), (0,)), ((), ())), preferred_element_type=F32)


def _ada_kernel(c_ref, w_ref, b_ref, o_ref):
    c = c_ref[...]
    s = c * _sigmoid(c)
    o_ref[0] = jnp.dot(s, w_ref[0], preferred_element_type=F32,
                       precision=lax.Precision.HIGHEST) + b_ref[0]


def _ada(cvec, w_ada, b_ada):
    depth = w_ada.shape[0]
    nblk = (NMOD * D) // D
    return pl.pallas_call(
        _ada_kernel,
        grid=(depth, nblk),
        in_specs=[pl.BlockSpec((8, D), lambda l, j: (0, 0)),
                  pl.BlockSpec((1, D, D), lambda l, j: (l, 0, j)),
                  pl.BlockSpec((1, 1, D), lambda l, j: (l, 0, j))],
        out_specs=pl.BlockSpec((1, 8, D), lambda l, j: (l, 0, j)),
        out_shape=jax.ShapeDtypeStruct((depth, 8, NMOD * D), F32),
        name="ada",
    )(cvec, w_ada, b_ada.reshape(depth, 1, NMOD * D))


def _swiglu_half(h, m, shift_i, gain, w1_ref, w3_ref, w2_ref):
    x = _rms(h, gain) * (1.0 + m[shift_i + 1:shift_i + 2]) + m[shift_i:shift_i + 1]
    xb = x.astype(BF16)
    a = _dot(xb, w1_ref[...])
    b = _dot(xb, w3_ref[...])
    g = (a * _sigmoid(a) * b).astype(BF16)
    y = _dot(g, w2_ref[...])
    return h + 0.5 * m[shift_i + 2:shift_i + 3] * y


def _ffn_in_kernel(h_ref, mod_ref, n1_ref, nm_ref, w1_ref, w3_ref, w2_ref, win_ref, rc_ref, rs_ref,
                   h1_ref, qt_ref, k_ref, vt_ref, hq_ref, hf_ref, hi_ref, hg_ref, zp_ref, *, rope):
    h = h_ref[0]
    m = mod_ref[0]
    h1 = _swiglu_half(h, m, 0, n1_ref[...], w1_ref, w3_ref, w2_ref)
    h1_ref[0] = h1
    u = _rms(h1, nm_ref[...]) * (1.0 + m[4:5]) + m[3:4]
    z = _dot(u.astype(BF16), win_ref[...])
    q = z[:, 0:QKW]
    k = z[:, QKW:2 * QKW]
    v = z[:, 2 * QKW:3 * QKW]
    if rope:
        lane = lax.broadcasted_iota(jnp.int32, q.shape, 1)
        first = (lane % 32) < 16
        rc = jnp.concatenate([rc_ref[...]] * (QKW // 128), axis=1)
        rs = jnp.concatenate([rs_ref[...]] * (QKW // 128), axis=1)

        def rot(x):
            partner = jnp.where(first, pltpu.roll(x, QKW - 16, 1), pltpu.roll(x, 16, 1))
            return x * rc + partner * rs

        q = rot(q)
        k = rot(k)
    q = q * (HALF ** -0.5 * LOG2E)
    qt_ref[0] = q.T.astype(BF16)
    k_ref[0] = k.astype(BF16)
    vt_ref[0] = v.T.astype(BF16)
    o = 3 * QKW
    hq_ref[0] = z[:, o:o + 256]
    hf_ref[0, 0] = z[:, o + 256:o + 512]
    hf_ref[0, 1] = z[:, o + 512:o + 768]
    hi_ref[0] = z[:, o + 768:o + 1024]
    hg_ref[0] = z[:, o + 1024:o + 1280]
    zp_ref[0] = z[:, o + 1280:o + 1536]


def _ffn_in(h, mod, n1, nm, w1, w3, w2, win, rc, rs, *, rope, tm):
    b, t, _ = h.shape
    grid = (b, t // tm)
    tok = lambda w: pl.BlockSpec((1, tm, w), lambda bi, i: (bi, i, 0))
    tokt = pl.BlockSpec((1, QKW, tm), lambda bi, i: (bi, 0, i))
    out_shape = (
        jax.ShapeDtypeStruct((b, t, D), F32),
        jax.ShapeDtypeStruct((b, QKW, t), BF16),
        jax.ShapeDtypeStruct((b, t, QKW), BF16),
        jax.ShapeDtypeStruct((b, QKW, t), BF16),
        jax.ShapeDtypeStruct((b, t, 256), F32),
        jax.ShapeDtypeStruct((b, 2, t, 256), F32),
        jax.ShapeDtypeStruct((b, t, 256), F32),
        jax.ShapeDtypeStruct((b, t, 256), F32),
        jax.ShapeDtypeStruct((b, t, 256), F32),
    )
    out_specs = (tok(D), tokt, tok(QKW), tokt, tok(256),
                 pl.BlockSpec((1, 2, tm, 256), lambda bi, i: (bi, 0, i, 0)),
                 tok(256), tok(256), tok(256))
    in_specs = [tok(D),
                pl.BlockSpec((1, NMOD, D), lambda bi, i: (bi, 0, 0)),
                _const_spec((1, D)), _const_spec((1, D)),
                _const_spec((D, DFF)), _const_spec((D, DFF)), _const_spec((DFF, D)),
                _const_spec((D, IN_A)),
                pl.BlockSpec((tm, 128), lambda bi, i: (i, 0)),
                pl.BlockSpec((tm, 128), lambda bi, i: (i, 0))]
    return pl.pallas_call(
        functools.partial(_ffn_in_kernel, rope=rope),
        grid=grid, in_specs=in_specs, out_specs=out_specs, out_shape=out_shape,
        compiler_params=pltpu.CompilerParams(
            dimension_semantics=("parallel", "parallel"), vmem_limit_bytes=VMEM_LIMIT),
        name="ffn_in",
    )(h, mod, n1, nm, w1, w3, w2, win, rc, rs)


def _attn_kernel(*refs, tq, tk, n_main, has_extra, lam_init):
    if has_extra:
        (qt_ref, k_ref, vt_ref, kx_ref, vtx_ref, lam_ref, gain_ref, o_ref,
         acc1, acc2, l_sc, kmax) = refs
    else:
        (qt_ref, k_ref, vt_ref, lam_ref, gain_ref, o_ref, acc1, acc2, l_sc, kmax) = refs
        kx_ref = vtx_ref = None
    qt = qt_ref[0]
    row = lax.broadcasted_iota(jnp.int32, qt.shape, 0)
    zero = jnp.zeros_like(qt)
    qbd = jnp.concatenate([jnp.where(row < HALF, qt, zero), jnp.where(row >= HALF, qt, zero)], axis=1)

    def kchunk(c):
        return k_ref[0, pl.ds(pl.multiple_of(c * tk, tk), tk), :]

    def vchunk(c):
        return vt_ref[0, :, pl.ds(pl.multiple_of(c * tk, tk), tk)]

    @pl.when(pl.program_id(2) == 0)
    def _():
        lane = lax.broadcasted_iota(jnp.int32, (16, VDIM), 1)
        r16 = lax.broadcasted_iota(jnp.int32, (16, VDIM), 0)
        sel = jnp.where((r16 == 0) & (lane < HALF) | (r16 == 1) & (lane >= HALF), 1.0, 0.0).astype(BF16)

        def sq_norms(kb):
            kf = kb.astype(F32)
            return _dot_nt(sel, (kf * kf).astype(BF16))

        n2 = lax.fori_loop(0, n_main, lambda c, mx: jnp.maximum(mx, sq_norms(kchunk(c))),
                           jnp.zeros((16, tk), F32))
        top = jnp.max(n2, axis=1, keepdims=True)
        if has_extra:
            top = jnp.maximum(top, jnp.max(sq_norms(kx_ref[0]), axis=1, keepdims=True))
        kmax[...] = jnp.broadcast_to(jnp.sqrt(top[0:8]), kmax.shape)

    qf = qt.astype(F32)
    q1 = jnp.sqrt(jnp.sum(jnp.where(row < HALF, qf * qf, 0.0), axis=0, keepdims=True))
    q2 = jnp.sqrt(jnp.sum(jnp.where(row >= HALF, qf * qf, 0.0), axis=0, keepdims=True))
    km = kmax[...]
    bound = jnp.concatenate([q1 * km[0:1, 0:1], q2 * km[1:2, 0:1]], axis=1) * BOUND_MARGIN
    acc1[...] = jnp.zeros_like(acc1)
    acc2[...] = jnp.zeros_like(acc2)

    def fast_step(kb, vtb, l):
        p = jnp.exp2(_dot(kb, qbd) - bound)
        pb = p.astype(BF16)
        acc1[...] += _dot(vtb, pb[:, :tq])
        acc2[...] += _dot(vtb, pb[:, tq:])
        return l + jnp.sum(p, axis=0, keepdims=True)

    l = lax.fori_loop(0, n_main, lambda c, l: fast_step(kchunk(c), vchunk(c), l),
                      jnp.zeros((1, 2 * tq), F32), unroll=2)
    if has_extra:
        l = fast_step(kx_ref[0], vtx_ref[0], l)
    l_sc[...] = jnp.broadcast_to(l, l_sc.shape)

    @pl.when(jnp.logical_not(jnp.min(l) >= L_FLOOR))
    def _():
        acc1[...] = jnp.zeros_like(acc1)
        acc2[...] = jnp.zeros_like(acc2)

        def safe_step(kb, vtb, carry):
            m, ls = carry
            s = _dot(kb, qbd)
            m_new = jnp.maximum(m, jnp.max(s, axis=0, keepdims=True))
            alpha = jnp.exp2(m - m_new)
            p = jnp.exp2(s - m_new)
            pb = p.astype(BF16)
            acc1[...] = acc1[...] * alpha[:, :tq] + _dot(vtb, pb[:, :tq])
            acc2[...] = acc2[...] * alpha[:, tq:] + _dot(vtb, pb[:, tq:])
            return m_new, alpha * ls + jnp.sum(p, axis=0, keepdims=True)

        carry = (jnp.full((1, 2 * tq), NEG, F32), jnp.zeros((1, 2 * tq), F32))
        carry = lax.fori_loop(0, n_main, lambda c, cr: safe_step(kchunk(c), vchunk(c), cr), carry)
        if has_extra:
            carry = safe_step(kx_ref[0], vtx_ref[0], carry)
        l_sc[...] = jnp.broadcast_to(carry[1], l_sc.shape)

    l = l_sc[0:1, :]

    lv = lam_ref[...]
    lam = (jnp.exp(jnp.sum(lv[0:1] * lv[1:2], axis=1, keepdims=True))
           - jnp.exp(jnp.sum(lv[2:3] * lv[3:4], axis=1, keepdims=True)) + lam_init)
    o = acc1[...] / l[:, :tq] - lam * (acc2[...] / l[:, tq:])
    o = o * lax.rsqrt(jnp.mean(o * o, axis=0, keepdims=True) + EPS) * gain_ref[...] * (1.0 - lam_init)
    o_ref[0] = o.T.astype(BF16)


def _attention(qt, k, vt, kx, vtx, lamv, gain, *, lam_init, tq, tk):
    b, _, t_q = qt.shape
    t_k = k.shape[1]
    has_extra = kx is not None
    assert t_k % (2 * tk) == 0
    grid = (b, HEADS, t_q // tq)
    in_specs = [pl.BlockSpec((1, VDIM, tq), lambda bi, hi, i: (bi, hi, i)),
                pl.BlockSpec((1, t_k, VDIM), lambda bi, hi, i: (bi, 0, hi)),
                pl.BlockSpec((1, VDIM, t_k), lambda bi, hi, i: (bi, hi, 0))]
    args = [qt, k, vt]
    if has_extra:
        t_x = kx.shape[1]
        in_specs += [pl.BlockSpec((1, t_x, VDIM), lambda bi, hi, i: (bi, 0, hi)),
                     pl.BlockSpec((1, VDIM, t_x), lambda bi, hi, i: (bi, hi, 0))]
        args += [kx, vtx]
    in_specs += [pl.BlockSpec((4, HALF), lambda bi, hi, i: (0, 0)),
                 pl.BlockSpec((VDIM, 1), lambda bi, hi, i: (0, 0))]
    args += [lamv, gain]
    return pl.pallas_call(
        functools.partial(_attn_kernel, tq=tq, tk=tk, n_main=t_k // tk, has_extra=has_extra,
                          lam_init=lam_init),
        grid=grid, in_specs=in_specs,
        out_specs=pl.BlockSpec((1, tq, VDIM), lambda bi, hi, i: (bi, i, hi)),
        out_shape=jax.ShapeDtypeStruct((b, t_q, HEADS * VDIM), BF16),
        scratch_shapes=[pltpu.VMEM((VDIM, tq), F32), pltpu.VMEM((VDIM, tq), F32),
                        pltpu.VMEM((8, 2 * tq), F32), pltpu.VMEM((8, 128), F32)],
        compiler_params=pltpu.CompilerParams(
            dimension_semantics=("parallel", "parallel", "arbitrary"), vmem_limit_bytes=VMEM_LIMIT),
        name="diff_attn",
    )(*args)


def _chunk_cumsum(x, rev):
    n = x.shape[0]
    pos = lax.broadcasted_iota(jnp.int32, x.shape, 0) % CHUNK
    sh = 1
    while sh < CHUNK:
        if rev:
            x = x + jnp.where(pos + sh < CHUNK, pltpu.roll(x, n - sh, 0), 0.0)
        else:
            x = x + jnp.where(pos >= sh, pltpu.roll(x, sh, 0), 0.0)
        sh *= 2
    return x


def _hgrn_direction(zq, zf, zi, lbv, st_ref, o_ref, stall, dst, *, rev, tb):
    nc = tb // CHUNK
    q = zq * _sigmoid(zq)
    kk = (1.0 - lbv) * _sigmoid(-zf)
    log_lb = jnp.log(jnp.maximum(lbv, LB_MIN))
    log_sig = jnp.minimum(zf, 0.0) - jnp.log1p(jnp.exp(-jnp.abs(zf)))
    y = jnp.log1p(-lbv) + log_sig
    lf = jnp.maximum(log_lb, y) + jnp.log1p(jnp.exp(-jnp.abs(log_lb - y)))
    a = _chunk_cumsum(lf, rev)
    a3 = a.reshape(nc, CHUNK, HGW)
    end_row = 0 if rev else CHUNK - 1
    mid_row = CHUNK // 2 if rev else CHUNK // 2 - 1
    a_end = a3[:, end_row:end_row + 1, :]
    a_mid = a3[:, mid_row:mid_row + 1, :]
    q3 = q.reshape(nc, CHUNK, HGW)
    k3 = kk.reshape(nc, CHUNK, HGW)
    qe = (q3 * jnp.exp(a3)).astype(BF16)
    ke = (k3 * jnp.exp(a_end - a3)).astype(BF16)
    qm = (q3 * jnp.exp(a3 - a_mid)).reshape(tb, HGW)
    km = (k3 * jnp.exp(a_mid - a3)).reshape(tb, HGW).astype(BF16)
    vb = zi.astype(BF16)
    v3 = vb.reshape(nc, CHUNK, HGW)
    dec = jnp.exp(a_end)

    lane_head = lax.broadcasted_iota(jnp.int32, (1, HGW), 1) // HGH
    sub = 128
    r_i = lax.broadcasted_iota(jnp.int32, (sub, sub), 0)
    c_i = lax.broadcasted_iota(jnp.int32, (sub, sub), 1)
    same = (r_i // CHUNK) == (c_i // CHUNK)
    causal = jnp.logical_and(same, (c_i >= r_i) if rev else (c_i <= r_i))
    causal4 = jnp.concatenate([causal] * HEADS, axis=0)
    intra = []
    for g in range(tb // sub):
        sl = slice(g * sub, (g + 1) * sub)
        qg = qm[sl]
        qstack = jnp.concatenate(
            [jnp.where(lane_head == hh, qg, 0.0) for hh in range(HEADS)], axis=0).astype(BF16)
        sc = _dot_nt(qstack, km[sl])
        sc = jnp.where(causal4, sc, 0.0).astype(BF16)
        r = _dot(sc, vb[sl])
        og = jnp.zeros((sub, HGW), F32)
        for hh in range(HEADS):
            og = og + jnp.where(lane_head == hh, r[hh * sub:(hh + 1) * sub], 0.0)
        intra.append(og)
    o_intra = jnp.concatenate(intra, axis=0)

    head_mask = (lax.broadcasted_iota(jnp.int32, (HGW, HGW), 0) // HGH
                 == lax.broadcasted_iota(jnp.int32, (HGW, HGW), 1) // HGH)
    for c in range(nc):
        dst[c] = jnp.where(head_mask, _dot_tn(v3[c], ke[c]), 0.0)
    st = st_ref[...]
    order = range(nc - 1, -1, -1) if rev else range(nc)
    for c in order:
        stall[c] = st.astype(BF16)
        st = dec[c] * st + dst[c]
    st_ref[...] = st
    inter = [_dot_nt(qe[c], stall[c]) for c in range(nc)]
    o_ref[...] = o_intra + jnp.concatenate(inter, axis=0)


def _hgrn_kernel(zqf_ref, zff_ref, zif_ref, zqb_ref, zfb_ref, zib_ref, lb_ref, s0_ref,
                 of_ref, ob_ref, s_ref, stall_f, dst_f, stall_b, dst_b, *, tb):
    @pl.when(pl.program_id(1) == 0)
    def _():
        s_ref[...] = s0_ref[...]

    lb = lb_ref[...]
    _hgrn_direction(zqf_ref[0], zff_ref[0, 0], zif_ref[0], lb[0:1], s_ref.at[0, 0], of_ref.at[0],
                    stall_f, dst_f, rev=False, tb=tb)
    _hgrn_direction(zqb_ref[0], zfb_ref[0, 0], zib_ref[0], lb[1:2], s_ref.at[0, 1], ob_ref.at[0],
                    stall_b, dst_b, rev=True, tb=tb)


def _hgrn(zq, zf, zi, lb, s0, *, tb):
    b, t, _ = zq.shape
    nb = t // tb
    fwd = lambda bi, i: (bi, i, 0)
    bwd = lambda bi, i: (bi, nb - 1 - i, 0)
    in_specs = [pl.BlockSpec((1, tb, HGW), fwd),
                pl.BlockSpec((1, 1, tb, HGW), lambda bi, i: (bi, 0, i, 0)),
                pl.BlockSpec((1, tb, HGW), fwd),
                pl.BlockSpec((1, tb, HGW), bwd),
                pl.BlockSpec((1, 1, tb, HGW), lambda bi, i: (bi, 1, nb - 1 - i, 0)),
                pl.BlockSpec((1, tb, HGW), bwd),
                pl.BlockSpec((2, HGW), lambda bi, i: (0, 0)),
                pl.BlockSpec((1, 2, HGW, HGW), lambda bi, i: (bi, 0, 0, 0))]
    out_specs = (pl.BlockSpec((1, tb, HGW), fwd),
                 pl.BlockSpec((1, tb, HGW), bwd),
                 pl.BlockSpec((1, 2, HGW, HGW), lambda bi, i: (bi, 0, 0, 0)))
    out_shape = (jax.ShapeDtypeStruct((b, t, HGW), F32),
                 jax.ShapeDtypeStruct((b, t, HGW), F32),
                 jax.ShapeDtypeStruct((b, 2, HGW, HGW), F32))
    nc = tb // CHUNK
    of, ob, s_fin = pl.pallas_call(
        functools.partial(_hgrn_kernel, tb=tb),
        grid=(b, nb), in_specs=in_specs, out_specs=out_specs, out_shape=out_shape,
        scratch_shapes=[pltpu.VMEM((nc, HGW, HGW), BF16), pltpu.VMEM((nc, HGW, HGW), F32),
                        pltpu.VMEM((nc, HGW, HGW), BF16), pltpu.VMEM((nc, HGW, HGW), F32)],
        compiler_params=pltpu.CompilerParams(
            dimension_semantics=("parallel", "arbitrary"), vmem_limit_bytes=VMEM_LIMIT),
        name="hgrn",
    )(zq, zf, zi, zq, zf, zi, lb, s0)
    return of, ob, s_fin


def _merge_kernel(h_ref, mod_ref, nm_ref, n2_ref, oda_ref, of_ref, ob_ref, hg_ref,
                  zp_ref, zpp_ref, zpn_ref, hgn_ref, pw_ref, ps_ref,
                  wg_ref, wpa_ref, wph_ref, wpp_ref, wo_ref, w1_ref, w3_ref, w2_ref, fn_ref,
                  out_ref, *, tm, t_total, final):
    i = pl.program_id(1)
    h1 = h_ref[0]
    m = mod_ref[0]

    x = zp_ref[0]
    prev = jnp.where(i > 0, zpp_ref[0], 0.0)
    nxt = jnp.where(i < pl.num_programs(1) - 1, zpn_ref[0], 0.0)
    e = jnp.concatenate([prev, x, nxt], axis=0)
    n = tm + 16
    a2 = e[0:n - 1] + e[1:n]
    a4 = a2[0:n - 3] + a2[2:n - 1]
    a8 = a4[0:n - 7] + a4[4:n - 3]
    a16 = a8[0:n - 15] + a8[8:n - 7]
    sums = (a2[7:7 + tm], a4[6:6 + tm], a8[4:4 + tm], a16[0:tm])
    pos = i * tm + lax.broadcasted_iota(jnp.int32, (tm, PW), 0)
    group = lax.broadcasted_iota(jnp.int32, (tm, PW), 1) // 64
    mixed = jnp.zeros((tm, PW), F32)
    for g, w in enumerate(POOL_WINDOWS):
        cnt = (jnp.minimum(pos + w // 2, t_total) - jnp.maximum(pos - w // 2, 0)).astype(F32)
        mixed = jnp.where(group == g, sums[g] / cnt - x, mixed)
    o_pool = _dot(mixed.astype(BF16), pw_ref[...]) * ps_ref[...]

    o = of_ref[0] + ob_ref[0]
    o2 = o * o
    hi = o2.astype(BF16)
    lo = (o2 - hi.astype(F32)).astype(BF16)
    ones_bd = (lax.broadcasted_iota(jnp.int32, (HGW, HGW), 0) // HGH
               == lax.broadcasted_iota(jnp.int32, (HGW, HGW), 1) // HGH).astype(BF16)
    seg = _dot(hi, ones_bd) + _dot(lo, ones_bd)
    zg = hg_ref[0]
    o_hg = (o * lax.rsqrt(seg * (1.0 / HGH) + EPS) * hgn_ref[...]) * (zg * _sigmoid(zg))

    u = _rms(h1, nm_ref[...]) * (1.0 + m[4:5]) + m[3:4]
    gate = _sigmoid(_dot(u.astype(BF16), wg_ref[...]))
    y = (gate[:, 0:D] * _dot(oda_ref[0], wpa_ref[...])
         + gate[:, D:2 * D] * _dot(o_hg.astype(BF16), wph_ref[...])
         + gate[:, 2 * D:3 * D] * _dot(o_pool.astype(BF16), wpp_ref[...]))
    mix = _dot(y.astype(BF16), wo_ref[...])
    h2 = h1 + m[5:6] * mix
    h3 = _swiglu_half(h2, m, 6, n2_ref[...], w1_ref, w3_ref, w2_ref)
    if final:
        h3 = _rms(h3, fn_ref[...])
    out_ref[0] = h3


def _merge(h1, mod, nm, n2, oda, of, ob, hg, zp, hgn, pw, ps, wg, wpa, wph, wpp, wo, w1, w3, w2, fn,
           *, tm, final):
    b, t, _ = h1.shape
    nblk8 = t // 8
    r = tm // 8
    tok = lambda w: pl.BlockSpec((1, tm, w), lambda bi, i: (bi, i, 0))
    in_specs = [tok(D),
                pl.BlockSpec((1, NMOD, D), lambda bi, i: (bi, 0, 0)),
                _const_spec((1, D)), _const_spec((1, D)),
                tok(QKW),
                tok(HGW), tok(HGW),
                tok(HGW), tok(PW),
                pl.BlockSpec((1, 8, PW), lambda bi, i: (bi, jnp.maximum(i * r - 1, 0), 0)),
                pl.BlockSpec((1, 8, PW), lambda bi, i: (bi, jnp.minimum((i + 1) * r, nblk8 - 1), 0)),
                _const_spec((1, HGW)), _const_spec((PW, PW)), _const_spec((1, PW)),
                _const_spec((D, 3 * D)), _const_spec((QKW, D)), _const_spec((HGW, D)),
                _const_spec((PW, D)), _const_spec((D, D)),
                _const_spec((D, DFF)), _const_spec((D, DFF)), _const_spec((DFF, D)),
                _const_spec((1, D))]
    return pl.pallas_call(
        functools.partial(_merge_kernel, tm=tm, t_total=t, final=final),
        grid=(b, t // tm), in_specs=in_specs, out_specs=tok(D),
        out_shape=jax.ShapeDtypeStruct((b, t, D), F32),
        compiler_params=pltpu.CompilerParams(
            dimension_semantics=("parallel", "parallel"), vmem_limit_bytes=VMEM_LIMIT),
        name="merge_ffn",
    )(h1, mod, nm, n2, oda, of, ob, hg, zp, zp, zp, hgn, pw, ps, wg, wpa, wph, wpp, wo, w1, w3, w2, fn)


def _rope_tables(n):
    rows = n // GRID_WIDTH
    row = jnp.repeat(jnp.arange(rows, dtype=jnp.int32), GRID_WIDTH).astype(F32)
    col = jnp.tile(jnp.arange(GRID_WIDTH, dtype=jnp.int32), rows).astype(F32)
    axis_dim = HALF // 2
    inv_freq = THETA ** (-jnp.arange(0, axis_dim, 2, dtype=F32) / axis_dim)
    ang_r = row[:, None] * inv_freq[None, :]
    ang_c = col[:, None] * inv_freq[None, :]
    cos64 = jnp.concatenate([jnp.cos(ang_r)] * 2 + [jnp.cos(ang_c)] * 2, axis=1)
    sin64 = jnp.concatenate([-jnp.sin(ang_r), jnp.sin(ang_r), -jnp.sin(ang_c), jnp.sin(ang_c)], axis=1)
    return jnp.tile(cos64, (1, 2)), jnp.tile(sin64, (1, 2))


def _lower_bounds(logits):
    p = jax.nn.softmax(logits.astype(F32), axis=0)
    return jnp.cumsum(p, axis=0) - p[0:1]


def _block_diag(w):
    g, a, b = w.shape
    out = jnp.zeros((g * a, g * b), w.dtype)
    for i in range(g):
        out = out.at[i * a:(i + 1) * a, i * b:(i + 1) * b].set(w[i])
    return out


def kernel(x, c, ctx, c_ctx, w_ada, b_ada, norm_ffn1, norm_mix, norm_ffn2, ffn1_w1, ffn1_w3, ffn1_w2,
           ffn2_w1, ffn2_w3, ffn2_w2, w_in, da_lambda_q1, da_lambda_k1, da_lambda_q2, da_lambda_k2,
           da_subln, hg_lb_logits, hg_norm, pool_w, pool_scale, w_proj_da, w_proj_hg, w_proj_pool,
           w_out, final_norm):
    bsz, n, _ = x.shape
    n_ctx = ctx.shape[1]
    depth = w_ada.shape[0]
    tm = 256
    tq, tk = 512, 1024
    tb = 256

    cvec = jnp.zeros((8, D), F32).at[0:bsz].set(c).at[bsz].set(c_ctx)
    mods = _ada(cvec, w_ada, b_ada).reshape(depth, 8, NMOD, D)
    rc, rs = _rope_tables(n)
    rc_c = jnp.zeros((n_ctx, 128), F32)
    lb_all = _lower_bounds(hg_lb_logits)
    row = lambda v: v.reshape(1, -1)
    bf = lambda w: w.astype(BF16)

    h, hc = x, ctx
    for l in range(depth):
        need_ctx = l < depth - 1
        lam_init = 0.8 - 0.6 * math.exp(-0.3 * l)
        m_lat = mods[l, 0:bsz]
        m_ctx = jnp.broadcast_to(mods[l, bsz][None], (bsz, NMOD, D))
        w1a, w3a, w2a = bf(ffn1_w1[l]), bf(ffn1_w3[l]), bf(ffn1_w2[l])
        w1b, w3b, w2b = bf(ffn2_w1[l]), bf(ffn2_w3[l]), bf(ffn2_w2[l])
        win_a = bf(w_in[l][:, :IN_A])
        w_gate = bf(w_in[l][:, IN_A:])
        lamv = jnp.stack([da_lambda_q1[l], da_lambda_k1[l], da_lambda_q2[l], da_lambda_k2[l]]).astype(F32)
        gain_da = da_subln[l].reshape(VDIM, 1)
        hgn = row(jnp.tile(hg_norm[l], HEADS))
        pw_bd = bf(_block_diag(pool_w[l]))
        merge_w = (hgn, pw_bd, row(pool_scale[l]), w_gate, bf(w_proj_da[l]), bf(w_proj_hg[l]),
                   bf(w_proj_pool[l]), bf(w_out[l]), w1b, w3b, w2b, row(final_norm))

        (hc1, qt_c, k_c, vt_c, hq_c, hf_c, hi_c, hg_c, zp_c) = _ffn_in(
            hc, m_ctx, row(norm_ffn1[l]), row(norm_mix[l]), w1a, w3a, w2a, win_a, rc_c, rc_c,
            rope=False, tm=min(tm, n_ctx))
        s0 = jnp.zeros((bsz, 2, HGW, HGW), F32)
        of_c, ob_c, s_ctx = _hgrn(hq_c, hf_c, hi_c, lb_all[l], s0, tb=min(tb, n_ctx))

        (h1, qt, k, vt, hq, hf, hi, hg, zp) = _ffn_in(
            h, m_lat, row(norm_ffn1[l]), row(norm_mix[l]), w1a, w3a, w2a, win_a, rc, rs,
            rope=True, tm=tm)
        o_da = _attention(qt, k, vt, k_c, vt_c, lamv, gain_da, lam_init=lam_init, tq=tq, tk=tk)
        of, ob, _ = _hgrn(hq, hf, hi, lb_all[l], s_ctx, tb=tb)
        h = _merge(h1, m_lat, row(norm_mix[l]), row(norm_ffn2[l]), o_da, of, ob, hg, zp, *merge_w,
                   tm=tm, final=not need_ctx)
        if need_ctx:
            o_dac = _attention(qt_c, k_c, vt_c, None, None, lamv, gain_da, lam_init=lam_init,
                               tq=min(tq, n_ctx), tk=min(tk, n_ctx // 2))
            hc = _merge(hc1, m_ctx, row(norm_mix[l]), row(norm_ffn2[l]), o_dac, of_c, ob_c, hg_c, zp_c,
                        *merge_w, tm=min(tm, n_ctx), final=False)
    return h
```

```python
import functools
import math

import jax
import jax.numpy as jnp
from jax import lax
from jax.experimental import pallas as pl
from jax.experimental.pallas import tpu as pltpu

D = 1024
GRID_WIDTH = 64
EPS = 1e-6
LB_MIN = 1e-20
NMOD = 9
HEADS = 4
HALF = 64
VDIM = 128
QKW = HEADS * 2 * HALF
HGW = 256
HGH = 64
PW = 256
POOL_WINDOWS = (2, 4, 8, 16)
DFF = 2816
THETA = 10000.0
IN_A = 3 * QKW + 6 * 256
CHUNK = 64
PAIR_RANGE = 80.0
NEG = -1e30
LOG2E = 1.4426950408889634
BOUND_MARGIN = 1.01
L_FLOOR = 2.0 ** -86

F32 = jnp.float32
BF16 = jnp.bfloat16
VMEM_LIMIT = 56 * 1024 * 1024


def _const_spec(shape):
    nd = len(shape)
    return pl.BlockSpec(shape, lambda *_: (0,) * nd, pipeline_mode=pl.Buffered(1))


def _rms(x, gain):
    return x * lax.rsqrt(jnp.mean(x * x, axis=-1, keepdims=True) + EPS) * gain


def _sigmoid(x):
    return 1.0 / (1.0 + jnp.exp(-x))


def _dot(a, b):
    return jnp.dot(a, b, preferred_element_type=F32)


def _dot_nt(a, b):
    return lax.dot_general(a, b, (((1,), (1,)), ((), ())), preferred_element_type=F32)


def _dot_tn(a, b):
    return lax.dot_general(a, b, (((0,), (0,)), ((), ())), preferred_element_type=F32)


def _ada_kernel(c_ref, w_ref, b_ref, o_ref):
    c = c_ref[...]
    s = c * _sigmoid(c)
    o_ref[0] = jnp.dot(s, w_ref[0], preferred_element_type=F32,
                       precision=lax.Precision.HIGHEST) + b_ref[0]


def _ada(cvec, w_ada, b_ada):
    depth = w_ada.shape[0]
    nblk = (NMOD * D) // D
    return pl.pallas_call(
        _ada_kernel,
        grid=(depth, nblk),
        in_specs=[pl.BlockSpec((8, D), lambda l, j: (0, 0)),
                  pl.BlockSpec((1, D, D), lambda l, j: (l, 0, j)),
                  pl.BlockSpec((1, 1, D), lambda l, j: (l, 0, j))],
        out_specs=pl.BlockSpec((1, 8, D), lambda l, j: (l, 0, j)),
        out_shape=jax.ShapeDtypeStruct((depth, 8, NMOD * D), F32),
        name="ada",
    )(cvec, w_ada, b_ada.reshape(depth, 1, NMOD * D))


def _swiglu_half(h, m, shift_i, gain, w1_ref, w3_ref, w2_ref):
    x = _rms(h, gain) * (1.0 + m[shift_i + 1:shift_i + 2]) + m[shift_i:shift_i + 1]
    xb = x.astype(BF16)
    a = _dot(xb, w1_ref[...])
    b = _dot(xb, w3_ref[...])
    g = (a * _sigmoid(a) * b).astype(BF16)
    y = _dot(g, w2_ref[...])
    return h + 0.5 * m[shift_i + 2:shift_i + 3] * y


def _ffn_in_kernel(h_ref, mod_ref, n1_ref, nm_ref, w1_ref, w3_ref, w2_ref, win_ref, rc_ref, rs_ref,
                   h1_ref, qt_ref, k_ref, vt_ref, hq_ref, hf_ref, hi_ref, hg_ref, zp_ref, *, rope):
    h = h_ref[0]
    m = mod_ref[0]
    h1 = _swiglu_half(h, m, 0, n1_ref[...], w1_ref, w3_ref, w2_ref)
    h1_ref[0] = h1
    u = _rms(h1, nm_ref[...]) * (1.0 + m[4:5]) + m[3:4]
    z = _dot(u.astype(BF16), win_ref[...])
    q = z[:, 0:QKW]
    k = z[:, QKW:2 * QKW]
    v = z[:, 2 * QKW:3 * QKW]
    if rope:
        lane = lax.broadcasted_iota(jnp.int32, q.shape, 1)
        first = (lane % 32) < 16
        rc = jnp.concatenate([rc_ref[...]] * (QKW // 128), axis=1)
        rs = jnp.concatenate([rs_ref[...]] * (QKW // 128), axis=1)

        def rot(x):
            partner = jnp.where(first, pltpu.roll(x, QKW - 16, 1), pltpu.roll(x, 16, 1))
            return x * rc + partner * rs

        q = rot(q)
        k = rot(k)
    q = q * (HALF ** -0.5 * LOG2E)
    qt_ref[0] = q.T.astype(BF16)
    k_ref[0] = k.astype(BF16)
    vt_ref[0] = v.T.astype(BF16)
    o = 3 * QKW
    hq_ref[0] = z[:, o:o + 256]
    hf_ref[0, 0] = z[:, o + 256:o + 512]
    hf_ref[0, 1] = z[:, o + 512:o + 768]
    hi_ref[0] = z[:, o + 768:o + 1024]
    hg_ref[0] = z[:, o + 1024:o + 1280]
    zp_ref[0] = z[:, o + 1280:o + 1536]


def _ffn_in(h, mod, n1, nm, w1, w3, w2, win, rc, rs, *, rope, tm):
    b, t, _ = h.shape
    grid = (b, t // tm)
    tok = lambda w: pl.BlockSpec((1, tm, w), lambda bi, i: (bi, i, 0))
    tokt = pl.BlockSpec((1, QKW, tm), lambda bi, i: (bi, 0, i))
    out_shape = (
        jax.ShapeDtypeStruct((b, t, D), F32),
        jax.ShapeDtypeStruct((b, QKW, t), BF16),
        jax.ShapeDtypeStruct((b, t, QKW), BF16),
        jax.ShapeDtypeStruct((b, QKW, t), BF16),
        jax.ShapeDtypeStruct((b, t, 256), F32),
        jax.ShapeDtypeStruct((b, 2, t, 256), F32),
        jax.ShapeDtypeStruct((b, t, 256), F32),
        jax.ShapeDtypeStruct((b, t, 256), F32),
        jax.ShapeDtypeStruct((b, t, 256), F32),
    )
    out_specs = (tok(D), tokt, tok(QKW), tokt, tok(256),
                 pl.BlockSpec((1, 2, tm, 256), lambda bi, i: (bi, 0, i, 0)),
                 tok(256), tok(256), tok(256))
    in_specs = [tok(D),
                pl.BlockSpec((1, NMOD, D), lambda bi, i: (bi, 0, 0)),
                _const_spec((1, D)), _const_spec((1, D)),
                _const_spec((D, DFF)), _const_spec((D, DFF)), _const_spec((DFF, D)),
                _const_spec((D, IN_A)),
                pl.BlockSpec((tm, 128), lambda bi, i: (i, 0)),
                pl.BlockSpec((tm, 128), lambda bi, i: (i, 0))]
    return pl.pallas_call(
        functools.partial(_ffn_in_kernel, rope=rope),
        grid=grid, in_specs=in_specs, out_specs=out_specs, out_shape=out_shape,
        compiler_params=pltpu.CompilerParams(
            dimension_semantics=("parallel", "parallel"), vmem_limit_bytes=VMEM_LIMIT),
        name="ffn_in",
    )(h, mod, n1, nm, w1, w3, w2, win, rc, rs)


def _attn_kernel(*refs, tq, tk, n_main, has_extra, lam_init):
    if has_extra:
        (qt_ref, k_ref, vt_ref, kx_ref, vtx_ref, lam_ref, gain_ref, o_ref,
         acc1, acc2, l_sc, kmax) = refs
    else:
        (qt_ref, k_ref, vt_ref, lam_ref, gain_ref, o_ref, acc1, acc2, l_sc, kmax) = refs
        kx_ref = vtx_ref = None
    qt = qt_ref[0]
    row = lax.broadcasted_iota(jnp.int32, qt.shape, 0)
    zero = jnp.zeros_like(qt)
    qbd = jnp.concatenate([jnp.where(row < HALF, qt, zero), jnp.where(row >= HALF, qt, zero)], axis=1)

    def kchunk(c):
        return k_ref[0, pl.ds(pl.multiple_of(c * tk, tk), tk), :]

    def vchunk(c):
        return vt_ref[0, :, pl.ds(pl.multiple_of(c * tk, tk), tk)]

    @pl.when(pl.program_id(2) == 0)
    def _():
        lane = lax.broadcasted_iota(jnp.int32, (16, VDIM), 1)
        r16 = lax.broadcasted_iota(jnp.int32, (16, VDIM), 0)
        sel = jnp.where((r16 == 0) & (lane < HALF) | (r16 == 1) & (lane >= HALF), 1.0, 0.0).astype(BF16)

        def sq_norms(kb):
            kf = kb.astype(F32)
            return _dot_nt(sel, (kf * kf).astype(BF16))

        n2 = lax.fori_loop(0, n_main, lambda c, mx: jnp.maximum(mx, sq_norms(kchunk(c))),
                           jnp.zeros((16, tk), F32))
        top = jnp.max(n2, axis=1, keepdims=True)
        if has_extra:
            top = jnp.maximum(top, jnp.max(sq_norms(kx_ref[0]), axis=1, keepdims=True))
        kmax[...] = jnp.broadcast_to(jnp.sqrt(top[0:8]), kmax.shape)

    qf = qt.astype(F32)
    q1 = jnp.sqrt(jnp.sum(jnp.where(row < HALF, qf * qf, 0.0), axis=0, keepdims=True))
    q2 = jnp.sqrt(jnp.sum(jnp.where(row >= HALF, qf * qf, 0.0), axis=0, keepdims=True))
    km = kmax[...]
    bound = jnp.concatenate([q1 * km[0:1, 0:1], q2 * km[1:2, 0:1]], axis=1) * BOUND_MARGIN
    acc1[...] = jnp.zeros_like(acc1)
    acc2[...] = jnp.zeros_like(acc2)

    def fast_step(kb, vtb, l):
        p = jnp.exp2(_dot(kb, qbd) - bound)
        pb = p.astype(BF16)
        acc1[...] += _dot(vtb, pb[:, :tq])
        acc2[...] += _dot(vtb, pb[:, tq:])
        return l + jnp.sum(p, axis=0, keepdims=True)

    l = lax.fori_loop(0, n_main, lambda c, l: fast_step(kchunk(c), vchunk(c), l),
                      jnp.zeros((1, 2 * tq), F32), unroll=2)
    if has_extra:
        l = fast_step(kx_ref[0], vtx_ref[0], l)
    l_sc[...] = jnp.broadcast_to(l, l_sc.shape)

    @pl.when(jnp.logical_not(jnp.min(l) >= L_FLOOR))
    def _():
        acc1[...] = jnp.zeros_like(acc1)
        acc2[...] = jnp.zeros_like(acc2)

        def safe_step(kb, vtb, carry):
            m, ls = carry
            s = _dot(kb, qbd)
            m_new = jnp.maximum(m, jnp.max(s, axis=0, keepdims=True))
            alpha = jnp.exp2(m - m_new)
            p = jnp.exp2(s - m_new)
            pb = p.astype(BF16)
            acc1[...] = acc1[...] * alpha[:, :tq] + _dot(vtb, pb[:, :tq])
            acc2[...] = acc2[...] * alpha[:, tq:] + _dot(vtb, pb[:, tq:])
            return m_new, alpha * ls + jnp.sum(p, axis=0, keepdims=True)

        carry = (jnp.full((1, 2 * tq), NEG, F32), jnp.zeros((1, 2 * tq), F32))
        carry = lax.fori_loop(0, n_main, lambda c, cr: safe_step(kchunk(c), vchunk(c), cr), carry)
        if has_extra:
            carry = safe_step(kx_ref[0], vtx_ref[0], carry)
        l_sc[...] = jnp.broadcast_to(carry[1], l_sc.shape)

    l = l_sc[0:1, :]

    lv = lam_ref[...]
    lam = (jnp.exp(jnp.sum(lv[0:1] * lv[1:2], axis=1, keepdims=True))
           - jnp.exp(jnp.sum(lv[2:3] * lv[3:4], axis=1, keepdims=True)) + lam_init)
    o = acc1[...] / l[:, :tq] - lam * (acc2[...] / l[:, tq:])
    o = o * lax.rsqrt(jnp.mean(o * o, axis=0, keepdims=True) + EPS) * gain_ref[...] * (1.0 - lam_init)
    o_ref[0] = o.T.astype(BF16)


def _attention(qt, k, vt, kx, vtx, lamv, gain, *, lam_init, tq, tk):
    b, _, t_q = qt.shape
    t_k = k.shape[1]
    has_extra = kx is not None
    assert t_k % (2 * tk) == 0
    grid = (b, HEADS, t_q // tq)
    in_specs = [pl.BlockSpec((1, VDIM, tq), lambda bi, hi, i: (bi, hi, i)),
                pl.BlockSpec((1, t_k, VDIM), lambda bi, hi, i: (bi, 0, hi)),
                pl.BlockSpec((1, VDIM, t_k), lambda bi, hi, i: (bi, hi, 0))]
    args = [qt, k, vt]
    if has_extra:
        t_x = kx.shape[1]
        in_specs += [pl.BlockSpec((1, t_x, VDIM), lambda bi, hi, i: (bi, 0, hi)),
                     pl.BlockSpec((1, VDIM, t_x), lambda bi, hi, i: (bi, hi, 0))]
        args += [kx, vtx]
    in_specs += [pl.BlockSpec((4, HALF), lambda bi, hi, i: (0, 0)),
                 pl.BlockSpec((VDIM, 1), lambda bi, hi, i: (0, 0))]
    args += [lamv, gain]
    return pl.pallas_call(
        functools.partial(_attn_kernel, tq=tq, tk=tk, n_main=t_k // tk, has_extra=has_extra,
                          lam_init=lam_init),
        grid=grid, in_specs=in_specs,
        out_specs=pl.BlockSpec((1, tq, VDIM), lambda bi, hi, i: (bi, i, hi)),
        out_shape=jax.ShapeDtypeStruct((b, t_q, HEADS * VDIM), BF16),
        scratch_shapes=[pltpu.VMEM((VDIM, tq), F32), pltpu.VMEM((VDIM, tq), F32),
                        pltpu.VMEM((8, 2 * tq), F32), pltpu.VMEM((8, 128), F32)],
        compiler_params=pltpu.CompilerParams(
            dimension_semantics=("parallel", "parallel", "arbitrary"), vmem_limit_bytes=VMEM_LIMIT),
        name="diff_attn",
    )(*args)


def _chunk_cumsum(x, rev):
    n = x.shape[0]
    pos = lax.broadcasted_iota(jnp.int32, x.shape, 0) % CHUNK
    sh = 1
    while sh < CHUNK:
        if rev:
            x = x + jnp.where(pos + sh < CHUNK, pltpu.roll(x, n - sh, 0), 0.0)
        else:
            x = x + jnp.where(pos >= sh, pltpu.roll(x, sh, 0), 0.0)
        sh *= 2
    return x


def _hgrn_gates(zq, zf, lbv):
    q = zq * _sigmoid(zq)
    kk = (1.0 - lbv) * _sigmoid(-zf)
    log_lb = jnp.log(jnp.maximum(lbv, LB_MIN))
    log_sig = jnp.minimum(zf, 0.0) - jnp.log1p(jnp.exp(-jnp.abs(zf)))
    y = jnp.log1p(-lbv) + log_sig
    lf = jnp.maximum(log_lb, y) + jnp.log1p(jnp.exp(-jnp.abs(log_lb - y)))
    return q, kk, lf


def _hgrn_token_scan(zq_ref, zf_ref, zi_ref, lbv, st_ref, o_ref, *, rev, tb):
    head_mask = (lax.broadcasted_iota(jnp.int32, (HGW, HGW), 0) // HGH
                 == lax.broadcasted_iota(jnp.int32, (HGW, HGW), 1) // HGH)
    first = lax.broadcasted_iota(jnp.int32, (16, HGW), 0) == 0

    def pad16(r):
        return jnp.where(first, jnp.broadcast_to(r, (16, HGW)), 0.0).astype(BF16)

    def body(i, carry):
        t = (tb - 1 - i) if rev else i
        q, kk, lf = _hgrn_gates(zq_ref[pl.ds(t, 1), :], zf_ref[pl.ds(t, 1), :], lbv)
        st = st_ref[...] * jnp.exp(lf) + jnp.where(
            head_mask, _dot_tn(pad16(zi_ref[pl.ds(t, 1), :]), pad16(kk)), 0.0)
        st_ref[...] = st
        o_ref[pl.ds(t, 1), :] = _dot_nt(pad16(q), st.astype(BF16))[0:1]
        return carry

    lax.fori_loop(0, tb, body, 0)


def _hgrn_direction(zq_ref, zf_ref, zi_ref, lbv, st_ref, o_ref, stall, dst, *, rev, tb):
    nc = tb // CHUNK
    zi = zi_ref[...]
    q, kk, lf = _hgrn_gates(zq_ref[...], zf_ref[...], lbv)
    a = _chunk_cumsum(lf, rev)
    a3 = a.reshape(nc, CHUNK, HGW)
    end_row = 0 if rev else CHUNK - 1
    mid_row = CHUNK // 2 if rev else CHUNK // 2 - 1
    a_end = a3[:, end_row:end_row + 1, :]
    a_mid = a3[:, mid_row:mid_row + 1, :]
    in_range = jnp.max(jnp.abs(a3 - a_mid)) <= PAIR_RANGE
    q3 = q.reshape(nc, CHUNK, HGW)
    k3 = kk.reshape(nc, CHUNK, HGW)
    qe = (q3 * jnp.exp(a3)).astype(BF16)
    ke = (k3 * jnp.exp(a_end - a3)).astype(BF16)
    qm = (q3 * jnp.exp(a3 - a_mid)).reshape(tb, HGW)
    km = (k3 * jnp.exp(a_mid - a3)).reshape(tb, HGW).astype(BF16)
    vb = zi.astype(BF16)
    v3 = vb.reshape(nc, CHUNK, HGW)
    dec = jnp.exp(a_end)

    lane_head = lax.broadcasted_iota(jnp.int32, (1, HGW), 1) // HGH
    sub = 128
    r_i = lax.broadcasted_iota(jnp.int32, (sub, sub), 0)
    c_i = lax.broadcasted_iota(jnp.int32, (sub, sub), 1)
    same = (r_i // CHUNK) == (c_i // CHUNK)
    causal = jnp.logical_and(same, (c_i >= r_i) if rev else (c_i <= r_i))
    causal4 = jnp.concatenate([causal] * HEADS, axis=0)
    intra = []
    for g in range(tb // sub):
        sl = slice(g * sub, (g + 1) * sub)
        qg = qm[sl]
        qstack = jnp.concatenate(
            [jnp.where(lane_head == hh, qg, 0.0) for hh in range(HEADS)], axis=0).astype(BF16)
        sc = _dot_nt(qstack, km[sl])
        sc = jnp.where(causal4, sc, 0.0).astype(BF16)
        r = _dot(sc, vb[sl])
        og = jnp.zeros((sub, HGW), F32)
        for hh in range(HEADS):
            og = og + jnp.where(lane_head == hh, r[hh * sub:(hh + 1) * sub], 0.0)
        intra.append(og)
    o_intra = jnp.concatenate(intra, axis=0)

    head_mask = (lax.broadcasted_iota(jnp.int32, (HGW, HGW), 0) // HGH
                 == lax.broadcasted_iota(jnp.int32, (HGW, HGW), 1) // HGH)
    for c in range(nc):
        dst[c] = jnp.where(head_mask, _dot_tn(v3[c], ke[c]), 0.0)
    st = st_ref[...]
    order = range(nc - 1, -1, -1) if rev else range(nc)
    for c in order:
        stall[c] = st.astype(BF16)
        st = dec[c] * st + dst[c]
    inter = [_dot_nt(qe[c], stall[c]) for c in range(nc)]
    o_ref[...] = o_intra + jnp.concatenate(inter, axis=0)

    @pl.when(in_range)
    def _():
        st_ref[...] = st

    @pl.when(jnp.logical_not(in_range))
    def _():
        _hgrn_token_scan(zq_ref, zf_ref, zi_ref, lbv, st_ref, o_ref, rev=rev, tb=tb)


def _hgrn_kernel(zqf_ref, zff_ref, zif_ref, zqb_ref, zfb_ref, zib_ref, lb_ref, s0_ref,
                 of_ref, ob_ref, s_ref, stall_f, dst_f, stall_b, dst_b, *, tb):
    @pl.when(pl.program_id(1) == 0)
    def _():
        s_ref[...] = s0_ref[...]

    lb = lb_ref[...]
    _hgrn_direction(zqf_ref.at[0], zff_ref.at[0, 0], zif_ref.at[0], lb[0:1], s_ref.at[0, 0], of_ref.at[0],
                    stall_f, dst_f, rev=False, tb=tb)
    _hgrn_direction(zqb_ref.at[0], zfb_ref.at[0, 0], zib_ref.at[0], lb[1:2], s_ref.at[0, 1], ob_ref.at[0],
                    stall_b, dst_b, rev=True, tb=tb)


def _hgrn(zq, zf, zi, lb, s0, *, tb):
    b, t, _ = zq.shape
    nb = t // tb
    fwd = lambda bi, i: (bi, i, 0)
    bwd = lambda bi, i: (bi, nb - 1 - i, 0)
    in_specs = [pl.BlockSpec((1, tb, HGW), fwd),
                pl.BlockSpec((1, 1, tb, HGW), lambda bi, i: (bi, 0, i, 0)),
                pl.BlockSpec((1, tb, HGW), fwd),
                pl.BlockSpec((1, tb, HGW), bwd),
                pl.BlockSpec((1, 1, tb, HGW), lambda bi, i: (bi, 1, nb - 1 - i, 0)),
                pl.BlockSpec((1, tb, HGW), bwd),
                pl.BlockSpec((2, HGW), lambda bi, i: (0, 0)),
                pl.BlockSpec((1, 2, HGW, HGW), lambda bi, i: (bi, 0, 0, 0))]
    out_specs = (pl.BlockSpec((1, tb, HGW), fwd),
                 pl.BlockSpec((1, tb, HGW), bwd),
                 pl.BlockSpec((1, 2, HGW, HGW), lambda bi, i: (bi, 0, 0, 0)))
    out_shape = (jax.ShapeDtypeStruct((b, t, HGW), F32),
                 jax.ShapeDtypeStruct((b, t, HGW), F32),
                 jax.ShapeDtypeStruct((b, 2, HGW, HGW), F32))
    nc = tb // CHUNK
    of, ob, s_fin = pl.pallas_call(
        functools.partial(_hgrn_kernel, tb=tb),
        grid=(b, nb), in_specs=in_specs, out_specs=out_specs, out_shape=out_shape,
        scratch_shapes=[pltpu.VMEM((nc, HGW, HGW), BF16), pltpu.VMEM((nc, HGW, HGW), F32),
                        pltpu.VMEM((nc, HGW, HGW), BF16), pltpu.VMEM((nc, HGW, HGW), F32)],
        compiler_params=pltpu.CompilerParams(
            dimension_semantics=("parallel", "arbitrary"), vmem_limit_bytes=VMEM_LIMIT),
        name="hgrn",
    )(zq, zf, zi, zq, zf, zi, lb, s0)
    return of, ob, s_fin


def _merge_kernel(h_ref, mod_ref, nm_ref, n2_ref, oda_ref, of_ref, ob_ref, hg_ref,
                  zp_ref, zpp_ref, zpn_ref, hgn_ref, pw_ref, ps_ref,
                  wg_ref, wpa_ref, wph_ref, wpp_ref, wo_ref, w1_ref, w3_ref, w2_ref, fn_ref,
                  out_ref, *, tm, t_total, final):
    i = pl.program_id(1)
    h1 = h_ref[0]
    m = mod_ref[0]

    x = zp_ref[0]
    prev = jnp.where(i > 0, zpp_ref[0], 0.0)
    nxt = jnp.where(i < pl.num_programs(1) - 1, zpn_ref[0], 0.0)
    e = jnp.concatenate([prev, x, nxt], axis=0)
    n = tm + 16
    a2 = e[0:n - 1] + e[1:n]
    a4 = a2[0:n - 3] + a2[2:n - 1]
    a8 = a4[0:n - 7] + a4[4:n - 3]
    a16 = a8[0:n - 15] + a8[8:n - 7]
    sums = (a2[7:7 + tm], a4[6:6 + tm], a8[4:4 + tm], a16[0:tm])
    pos = i * tm + lax.broadcasted_iota(jnp.int32, (tm, PW), 0)
    group = lax.broadcasted_iota(jnp.int32, (tm, PW), 1) // 64
    mixed = jnp.zeros((tm, PW), F32)
    for g, w in enumerate(POOL_WINDOWS):
        cnt = (jnp.minimum(pos + w // 2, t_total) - jnp.maximum(pos - w // 2, 0)).astype(F32)
        mixed = jnp.where(group == g, sums[g] / cnt - x, mixed)
    o_pool = _dot(mixed.astype(BF16), pw_ref[...]) * ps_ref[...]

    o = of_ref[0] + ob_ref[0]
    o2 = o * o
    hi = o2.astype(BF16)
    lo = (o2 - hi.astype(F32)).astype(BF16)
    ones_bd = (lax.broadcasted_iota(jnp.int32, (HGW, HGW), 0) // HGH
               == lax.broadcasted_iota(jnp.int32, (HGW, HGW), 1) // HGH).astype(BF16)
    seg = _dot(hi, ones_bd) + _dot(lo, ones_bd)
    zg = hg_ref[0]
    o_hg = (o * lax.rsqrt(seg * (1.0 / HGH) + EPS) * hgn_ref[...]) * (zg * _sigmoid(zg))

    u = _rms(h1, nm_ref[...]) * (1.0 + m[4:5]) + m[3:4]
    gate = _sigmoid(_dot(u.astype(BF16), wg_ref[...]))
    y = (gate[:, 0:D] * _dot(oda_ref[0], wpa_ref[...])
         + gate[:, D:2 * D] * _dot(o_hg.astype(BF16), wph_ref[...])
         + gate[:, 2 * D:3 * D] * _dot(o_pool.astype(BF16), wpp_ref[...]))
    mix = _dot(y.astype(BF16), wo_ref[...])
    h2 = h1 + m[5:6] * mix
    h3 = _swiglu_half(h2, m, 6, n2_ref[...], w1_ref, w3_ref, w2_ref)
    if final:
        h3 = _rms(h3, fn_ref[...])
    out_ref[0] = h3


def _merge(h1, mod, nm, n2, oda, of, ob, hg, zp, hgn, pw, ps, wg, wpa, wph, wpp, wo, w1, w3, w2, fn,
           *, tm, final):
    b, t, _ = h1.shape
    nblk8 = t // 8
    r = tm // 8
    tok = lambda w: pl.BlockSpec((1, tm, w), lambda bi, i: (bi, i, 0))
    in_specs = [tok(D),
                pl.BlockSpec((1, NMOD, D), lambda bi, i: (bi, 0, 0)),
                _const_spec((1, D)), _const_spec((1, D)),
                tok(QKW),
                tok(HGW), tok(HGW),
                tok(HGW), tok(PW),
                pl.BlockSpec((1, 8, PW), lambda bi, i: (bi, jnp.maximum(i * r - 1, 0), 0)),
                pl.BlockSpec((1, 8, PW), lambda bi, i: (bi, jnp.minimum((i + 1) * r, nblk8 - 1), 0)),
                _const_spec((1, HGW)), _const_spec((PW, PW)), _const_spec((1, PW)),
                _const_spec((D, 3 * D)), _const_spec((QKW, D)), _const_spec((HGW, D)),
                _const_spec((PW, D)), _const_spec((D, D)),
                _const_spec((D, DFF)), _const_spec((D, DFF)), _const_spec((DFF, D)),
                _const_spec((1, D))]
    return pl.pallas_call(
        functools.partial(_merge_kernel, tm=tm, t_total=t, final=final),
        grid=(b, t // tm), in_specs=in_specs, out_specs=tok(D),
        out_shape=jax.ShapeDtypeStruct((b, t, D), F32),
        compiler_params=pltpu.CompilerParams(
            dimension_semantics=("parallel", "parallel"), vmem_limit_bytes=VMEM_LIMIT),
        name="merge_ffn",
    )(h1, mod, nm, n2, oda, of, ob, hg, zp, zp, zp, hgn, pw, ps, wg, wpa, wph, wpp, wo, w1, w3, w2, fn)


def _rope_tables(n):
    rows = n // GRID_WIDTH
    row = jnp.repeat(jnp.arange(rows, dtype=jnp.int32), GRID_WIDTH).astype(F32)
    col = jnp.tile(jnp.arange(GRID_WIDTH, dtype=jnp.int32), rows).astype(F32)
    axis_dim = HALF // 2
    inv_freq = THETA ** (-jnp.arange(0, axis_dim, 2, dtype=F32) / axis_dim)
    ang_r = row[:, None] * inv_freq[None, :]
    ang_c = col[:, None] * inv_freq[None, :]
    cos64 = jnp.concatenate([jnp.cos(ang_r)] * 2 + [jnp.cos(ang_c)] * 2, axis=1)
    sin64 = jnp.concatenate([-jnp.sin(ang_r), jnp.sin(ang_r), -jnp.sin(ang_c), jnp.sin(ang_c)], axis=1)
    return jnp.tile(cos64, (1, 2)), jnp.tile(sin64, (1, 2))


def _lower_bounds(logits):
    p = jax.nn.softmax(logits.astype(F32), axis=0)
    return jnp.cumsum(p, axis=0) - p[0:1]


def _block_diag(w):
    g, a, b = w.shape
    out = jnp.zeros((g * a, g * b), w.dtype)
    for i in range(g):
        out = out.at[i * a:(i + 1) * a, i * b:(i + 1) * b].set(w[i])
    return out


def kernel(x, c, ctx, c_ctx, w_ada, b_ada, norm_ffn1, norm_mix, norm_ffn2, ffn1_w1, ffn1_w3, ffn1_w2,
           ffn2_w1, ffn2_w3, ffn2_w2, w_in, da_lambda_q1, da_lambda_k1, da_lambda_q2, da_lambda_k2,
           da_subln, hg_lb_logits, hg_norm, pool_w, pool_scale, w_proj_da, w_proj_hg, w_proj_pool,
           w_out, final_norm):
    bsz, n, _ = x.shape
    n_ctx = ctx.shape[1]
    depth = w_ada.shape[0]
    tm = 256
    tq, tk = 512, 1024
    tb = 512

    cvec = jnp.zeros((8, D), F32).at[0:bsz].set(c).at[bsz].set(c_ctx)
    mods = _ada(cvec, w_ada, b_ada).reshape(depth, 8, NMOD, D)
    rc, rs = _rope_tables(n)
    rc_c = jnp.zeros((n_ctx, 128), F32)
    lb_all = _lower_bounds(hg_lb_logits)
    row = lambda v: v.reshape(1, -1)
    bf = lambda w: w.astype(BF16)

    h, hc = x, ctx
    for l in range(depth):
        need_ctx = l < depth - 1
        lam_init = 0.8 - 0.6 * math.exp(-0.3 * l)
        m_lat = mods[l, 0:bsz]
        m_ctx = jnp.broadcast_to(mods[l, bsz][None], (bsz, NMOD, D))
        w1a, w3a, w2a = bf(ffn1_w1[l]), bf(ffn1_w3[l]), bf(ffn1_w2[l])
        w1b, w3b, w2b = bf(ffn2_w1[l]), bf(ffn2_w3[l]), bf(ffn2_w2[l])
        win_a = bf(w_in[l][:, :IN_A])
        w_gate = bf(w_in[l][:, IN_A:])
        lamv = jnp.stack([da_lambda_q1[l], da_lambda_k1[l], da_lambda_q2[l], da_lambda_k2[l]]).astype(F32)
        gain_da = da_subln[l].reshape(VDIM, 1)
        hgn = row(jnp.tile(hg_norm[l], HEADS))
        pw_bd = bf(_block_diag(pool_w[l]))
        merge_w = (hgn, pw_bd, row(pool_scale[l]), w_gate, bf(w_proj_da[l]), bf(w_proj_hg[l]),
                   bf(w_proj_pool[l]), bf(w_out[l]), w1b, w3b, w2b, row(final_norm))

        (hc1, qt_c, k_c, vt_c, hq_c, hf_c, hi_c, hg_c, zp_c) = _ffn_in(
            hc, m_ctx, row(norm_ffn1[l]), row(norm_mix[l]), w1a, w3a, w2a, win_a, rc_c, rc_c,
            rope=False, tm=min(tm, n_ctx))
        s0 = jnp.zeros((bsz, 2, HGW, HGW), F32)
        of_c, ob_c, s_ctx = _hgrn(hq_c, hf_c, hi_c, lb_all[l], s0, tb=min(tb, n_ctx))

        (h1, qt, k, vt, hq, hf, hi, hg, zp) = _ffn_in(
            h, m_lat, row(norm_ffn1[l]), row(norm_mix[l]), w1a, w3a, w2a, win_a, rc, rs,
            rope=True, tm=tm)
        o_da = _attention(qt, k, vt, k_c, vt_c, lamv, gain_da, lam_init=lam_init, tq=tq, tk=tk)
        of, ob, _ = _hgrn(hq, hf, hi, lb_all[l], s_ctx, tb=tb)
        h = _merge(h1, m_lat, row(norm_mix[l]), row(norm_ffn2[l]), o_da, of, ob, hg, zp, *merge_w,
                   tm=tm, final=not need_ctx)
        if need_ctx:
            o_dac = _attention(qt_c, k_c, vt_c, None, None, lamv, gain_da, lam_init=lam_init,
                               tq=min(tq, n_ctx), tk=min(tk, n_ctx // 2))
            hc = _merge(hc1, m_ctx, row(norm_mix[l]), row(norm_ffn2[l]), o_dac, of_c, ob_c, hg_c, zp_c,
                        *merge_w, tm=min(tm, n_ctx), final=False)
    return h
```

```python
import functools
import math
from typing import NamedTuple

import jax
import jax.numpy as jnp
from jax import lax
from jax.experimental import pallas as pl
from jax.experimental.pallas import tpu as pltpu

D = 1024
GRID_WIDTH = 64
EPS = 1e-6
LB_MIN = 1e-20
NMOD = 9
HEADS = 4
HALF = 64
VDIM = 128
QKW = HEADS * 2 * HALF
HGW = 256
HGH = 64
PW = 256
POOL_WINDOWS = (2, 4, 8, 16)
DFF = 2816
THETA = 10000.0
IN_A = 3 * QKW + 6 * 256
CHUNK = 64
PAIR_RANGE = 80.0
NEG = -1e30
LOG2E = 1.4426950408889634
BOUND_MARGIN = 1.01
L_FLOOR = 2.0 ** -86

F32 = jnp.float32
BF16 = jnp.bfloat16
VMEM_LIMIT = 56 * 1024 * 1024

TOKEN_TILE = 512
QUERY_TILE = 512
KEY_CHUNK = 2048
SCAN_BLOCK = 512


class _Tiles(NamedTuple):
    tm: int
    tq: int
    tk: int
    tb: int


def _tiles(n):
    return _Tiles(min(TOKEN_TILE, n), min(QUERY_TILE, n), min(KEY_CHUNK, n // 2), min(SCAN_BLOCK, n))


def _const_spec(shape):
    nd = len(shape)
    return pl.BlockSpec(shape, lambda *_: (0,) * nd, pipeline_mode=pl.Buffered(1))


def _rms(x, gain):
    return x * lax.rsqrt(jnp.mean(x * x, axis=-1, keepdims=True) + EPS) * gain


def _sigmoid(x):
    return 1.0 / (1.0 + jnp.exp(-x))


def _dot(a, b):
    return jnp.dot(a, b, preferred_element_type=F32)


def _dot_nt(a, b):
    return lax.dot_general(a, b, (((1,), (1,)), ((), ())), preferred_element_type=F32)


def _dot_tn(a, b):
    return lax.dot_general(a, b, (((0,), (0,)), ((), ())), preferred_element_type=F32)


def _ada_kernel(c_ref, w_ref, b_ref, o_ref):
    c = c_ref[...]
    s = c * _sigmoid(c)
    o_ref[0] = jnp.dot(s, w_ref[0], preferred_element_type=F32,
                       precision=lax.Precision.HIGHEST) + b_ref[0]


def _ada(cvec, w_ada, b_ada):
    depth = w_ada.shape[0]
    nblk = (NMOD * D) // D
    return pl.pallas_call(
        _ada_kernel,
        grid=(depth, nblk),
        in_specs=[pl.BlockSpec((8, D), lambda l, j: (0, 0)),
                  pl.BlockSpec((1, D, D), lambda l, j: (l, 0, j)),
                  pl.BlockSpec((1, 1, D), lambda l, j: (l, 0, j))],
        out_specs=pl.BlockSpec((1, 8, D), lambda l, j: (l, 0, j)),
        out_shape=jax.ShapeDtypeStruct((depth, 8, NMOD * D), F32),
        name="ada",
    )(cvec, w_ada, b_ada.reshape(depth, 1, NMOD * D))


def _swiglu_half(h, m, shift_i, gain, w1_ref, w3_ref, w2_ref):
    x = _rms(h, gain) * (1.0 + m[shift_i + 1:shift_i + 2]) + m[shift_i:shift_i + 1]
    xb = x.astype(BF16)
    a = _dot(xb, w1_ref[...])
    b = _dot(xb, w3_ref[...])
    g = (a * _sigmoid(a) * b).astype(BF16)
    y = _dot(g, w2_ref[...])
    return h + 0.5 * m[shift_i + 2:shift_i + 3] * y


def _ffn_in_kernel(h_ref, mod_ref, n1_ref, nm_ref, w1_ref, w3_ref, w2_ref, win_ref, rc_ref, rs_ref,
                   h1_ref, qt_ref, k_ref, vt_ref, hq_ref, hf_ref, hi_ref, hg_ref, zp_ref, *, rope):
    h = h_ref[0]
    m = mod_ref[0]
    h1 = _swiglu_half(h, m, 0, n1_ref[...], w1_ref, w3_ref, w2_ref)
    h1_ref[0] = h1
    u = _rms(h1, nm_ref[...]) * (1.0 + m[4:5]) + m[3:4]
    z = _dot(u.astype(BF16), win_ref[...])
    q = z[:, 0:QKW]
    k = z[:, QKW:2 * QKW]
    v = z[:, 2 * QKW:3 * QKW]
    if rope:
        lane = lax.broadcasted_iota(jnp.int32, q.shape, 1)
        first = (lane % 32) < 16
        rc = jnp.concatenate([rc_ref[...]] * (QKW // 128), axis=1)
        rs = jnp.concatenate([rs_ref[...]] * (QKW // 128), axis=1)

        def rot(x):
            partner = jnp.where(first, pltpu.roll(x, QKW - 16, 1), pltpu.roll(x, 16, 1))
            return x * rc + partner * rs

        q = rot(q)
        k = rot(k)
    q = q * (HALF ** -0.5 * LOG2E)
    qt_ref[0] = q.T.astype(BF16)
    k_ref[0] = k.astype(BF16)
    vt_ref[0] = v.T.astype(BF16)
    o = 3 * QKW
    hq_ref[0] = z[:, o:o + 256]
    hf_ref[0, 0] = z[:, o + 256:o + 512]
    hf_ref[0, 1] = z[:, o + 512:o + 768]
    hi_ref[0] = z[:, o + 768:o + 1024]
    hg_ref[0] = z[:, o + 1024:o + 1280]
    zp_ref[0] = z[:, o + 1280:o + 1536]


def _ffn_in(h, mod, n1, nm, w1, w3, w2, win, rc, rs, *, rope, tm):
    b, t, _ = h.shape
    grid = (b, t // tm)
    tok = lambda w: pl.BlockSpec((1, tm, w), lambda bi, i: (bi, i, 0))
    tokt = pl.BlockSpec((1, QKW, tm), lambda bi, i: (bi, 0, i))
    out_shape = (
        jax.ShapeDtypeStruct((b, t, D), F32),
        jax.ShapeDtypeStruct((b, QKW, t), BF16),
        jax.ShapeDtypeStruct((b, t, QKW), BF16),
        jax.ShapeDtypeStruct((b, QKW, t), BF16),
        jax.ShapeDtypeStruct((b, t, 256), F32),
        jax.ShapeDtypeStruct((b, 2, t, 256), F32),
        jax.ShapeDtypeStruct((b, t, 256), F32),
        jax.ShapeDtypeStruct((b, t, 256), F32),
        jax.ShapeDtypeStruct((b, t, 256), F32),
    )
    out_specs = (tok(D), tokt, tok(QKW), tokt, tok(256),
                 pl.BlockSpec((1, 2, tm, 256), lambda bi, i: (bi, 0, i, 0)),
                 tok(256), tok(256), tok(256))
    in_specs = [tok(D),
                pl.BlockSpec((1, NMOD, D), lambda bi, i: (bi, 0, 0)),
                _const_spec((1, D)), _const_spec((1, D)),
                _const_spec((D, DFF)), _const_spec((D, DFF)), _const_spec((DFF, D)),
                _const_spec((D, IN_A)),
                pl.BlockSpec((tm, 128), lambda bi, i: (i, 0)),
                pl.BlockSpec((tm, 128), lambda bi, i: (i, 0))]
    return pl.pallas_call(
        functools.partial(_ffn_in_kernel, rope=rope),
        grid=grid, in_specs=in_specs, out_specs=out_specs, out_shape=out_shape,
        compiler_params=pltpu.CompilerParams(
            dimension_semantics=("parallel", "parallel"), vmem_limit_bytes=VMEM_LIMIT),
        name="ffn_in",
    )(h, mod, n1, nm, w1, w3, w2, win, rc, rs)


def _attn_kernel(*refs, tq, tk, n_main, has_extra, lam_init):
    if has_extra:
        (qt_ref, k_ref, vt_ref, kx_ref, vtx_ref, lam_ref, gain_ref, o_ref,
         acc1, acc2, l_sc, kmax) = refs
    else:
        (qt_ref, k_ref, vt_ref, lam_ref, gain_ref, o_ref, acc1, acc2, l_sc, kmax) = refs
        kx_ref = vtx_ref = None
    qt = qt_ref[0]
    row = lax.broadcasted_iota(jnp.int32, qt.shape, 0)
    zero = jnp.zeros_like(qt)
    qbd = jnp.concatenate([jnp.where(row < HALF, qt, zero), jnp.where(row >= HALF, qt, zero)], axis=1)

    def kchunk(c):
        return k_ref[0, pl.ds(pl.multiple_of(c * tk, tk), tk), :]

    def vchunk(c):
        return vt_ref[0, :, pl.ds(pl.multiple_of(c * tk, tk), tk)]

    @pl.when(pl.program_id(2) == 0)
    def _():
        lane = lax.broadcasted_iota(jnp.int32, (16, VDIM), 1)
        r16 = lax.broadcasted_iota(jnp.int32, (16, VDIM), 0)
        sel = jnp.where((r16 == 0) & (lane < HALF) | (r16 == 1) & (lane >= HALF), 1.0, 0.0).astype(BF16)

        def sq_norms(kb):
            kf = kb.astype(F32)
            return _dot_nt(sel, (kf * kf).astype(BF16))

        n2 = lax.fori_loop(0, n_main, lambda c, mx: jnp.maximum(mx, sq_norms(kchunk(c))),
                           jnp.zeros((16, tk), F32))
        top = jnp.max(n2, axis=1, keepdims=True)
        if has_extra:
            top = jnp.maximum(top, jnp.max(sq_norms(kx_ref[0]), axis=1, keepdims=True))
        kmax[...] = jnp.broadcast_to(jnp.sqrt(top[0:8]), kmax.shape)

    qf = qt.astype(F32)
    q1 = jnp.sqrt(jnp.sum(jnp.where(row < HALF, qf * qf, 0.0), axis=0, keepdims=True))
    q2 = jnp.sqrt(jnp.sum(jnp.where(row >= HALF, qf * qf, 0.0), axis=0, keepdims=True))
    km = kmax[...]
    bound = jnp.concatenate([q1 * km[0:1, 0:1], q2 * km[1:2, 0:1]], axis=1) * BOUND_MARGIN
    acc1[...] = jnp.zeros_like(acc1)
    acc2[...] = jnp.zeros_like(acc2)

    def fast_step(kb, vtb, l):
        p = jnp.exp2(_dot(kb, qbd) - bound)
        pb = p.astype(BF16)
        acc1[...] += _dot(vtb, pb[:, :tq])
        acc2[...] += _dot(vtb, pb[:, tq:])
        return l + jnp.sum(p, axis=0, keepdims=True)

    l = lax.fori_loop(0, n_main, lambda c, l: fast_step(kchunk(c), vchunk(c), l),
                      jnp.zeros((1, 2 * tq), F32), unroll=2)
    if has_extra:
        l = fast_step(kx_ref[0], vtx_ref[0], l)
    l_sc[...] = jnp.broadcast_to(l, l_sc.shape)

    @pl.when(jnp.logical_not(jnp.min(l) >= L_FLOOR))
    def _():
        acc1[...] = jnp.zeros_like(acc1)
        acc2[...] = jnp.zeros_like(acc2)

        def safe_step(kb, vtb, carry):
            m, ls = carry
            s = _dot(kb, qbd)
            m_new = jnp.maximum(m, jnp.max(s, axis=0, keepdims=True))
            alpha = jnp.exp2(m - m_new)
            p = jnp.exp2(s - m_new)
            pb = p.astype(BF16)
            acc1[...] = acc1[...] * alpha[:, :tq] + _dot(vtb, pb[:, :tq])
            acc2[...] = acc2[...] * alpha[:, tq:] + _dot(vtb, pb[:, tq:])
            return m_new, alpha * ls + jnp.sum(p, axis=0, keepdims=True)

        carry = (jnp.full((1, 2 * tq), NEG, F32), jnp.zeros((1, 2 * tq), F32))
        carry = lax.fori_loop(0, n_main, lambda c, cr: safe_step(kchunk(c), vchunk(c), cr), carry)
        if has_extra:
            carry = safe_step(kx_ref[0], vtx_ref[0], carry)
        l_sc[...] = jnp.broadcast_to(carry[1], l_sc.shape)

    l = l_sc[0:1, :]

    lv = lam_ref[...]
    lam = (jnp.exp(jnp.sum(lv[0:1] * lv[1:2], axis=1, keepdims=True))
           - jnp.exp(jnp.sum(lv[2:3] * lv[3:4], axis=1, keepdims=True)) + lam_init)
    o = acc1[...] / l[:, :tq] - lam * (acc2[...] / l[:, tq:])
    o = o * lax.rsqrt(jnp.mean(o * o, axis=0, keepdims=True) + EPS) * gain_ref[...] * (1.0 - lam_init)
    o_ref[0] = o.T.astype(BF16)


def _attention(qt, k, vt, kx, vtx, lamv, gain, *, lam_init, tq, tk):
    b, _, t_q = qt.shape
    t_k = k.shape[1]
    has_extra = kx is not None
    assert t_k % (2 * tk) == 0
    grid = (b, HEADS, t_q // tq)
    in_specs = [pl.BlockSpec((1, VDIM, tq), lambda bi, hi, i: (bi, hi, i)),
                pl.BlockSpec((1, t_k, VDIM), lambda bi, hi, i: (bi, 0, hi)),
                pl.BlockSpec((1, VDIM, t_k), lambda bi, hi, i: (bi, hi, 0))]
    args = [qt, k, vt]
    if has_extra:
        t_x = kx.shape[1]
        in_specs += [pl.BlockSpec((1, t_x, VDIM), lambda bi, hi, i: (bi, 0, hi)),
                     pl.BlockSpec((1, VDIM, t_x), lambda bi, hi, i: (bi, hi, 0))]
        args += [kx, vtx]
    in_specs += [pl.BlockSpec((4, HALF), lambda bi, hi, i: (0, 0)),
                 pl.BlockSpec((VDIM, 1), lambda bi, hi, i: (0, 0))]
    args += [lamv, gain]
    return pl.pallas_call(
        functools.partial(_attn_kernel, tq=tq, tk=tk, n_main=t_k // tk, has_extra=has_extra,
                          lam_init=lam_init),
        grid=grid, in_specs=in_specs,
        out_specs=pl.BlockSpec((1, tq, VDIM), lambda bi, hi, i: (bi, i, hi)),
        out_shape=jax.ShapeDtypeStruct((b, t_q, HEADS * VDIM), BF16),
        scratch_shapes=[pltpu.VMEM((VDIM, tq), F32), pltpu.VMEM((VDIM, tq), F32),
                        pltpu.VMEM((8, 2 * tq), F32), pltpu.VMEM((8, 128), F32)],
        compiler_params=pltpu.CompilerParams(
            dimension_semantics=("parallel", "parallel", "arbitrary"), vmem_limit_bytes=VMEM_LIMIT),
        name="diff_attn",
    )(*args)


def _chunk_cumsum(x, rev):
    n = x.shape[0]
    pos = lax.broadcasted_iota(jnp.int32, x.shape, 0) % CHUNK
    sh = 1
    while sh < CHUNK:
        if rev:
            x = x + jnp.where(pos + sh < CHUNK, pltpu.roll(x, n - sh, 0), 0.0)
        else:
            x = x + jnp.where(pos >= sh, pltpu.roll(x, sh, 0), 0.0)
        sh *= 2
    return x


def _hgrn_gates(zq, zf, lbv):
    q = zq * _sigmoid(zq)
    kk = (1.0 - lbv) * _sigmoid(-zf)
    log_lb = jnp.log(jnp.maximum(lbv, LB_MIN))
    log_sig = jnp.minimum(zf, 0.0) - jnp.log1p(jnp.exp(-jnp.abs(zf)))
    y = jnp.log1p(-lbv) + log_sig
    lf = jnp.maximum(log_lb, y) + jnp.log1p(jnp.exp(-jnp.abs(log_lb - y)))
    return q, kk, lf


def _hgrn_token_scan(zq_ref, zf_ref, zi_ref, lbv, st_ref, o_ref, *, rev, tb):
    head_mask = (lax.broadcasted_iota(jnp.int32, (HGW, HGW), 0) // HGH
                 == lax.broadcasted_iota(jnp.int32, (HGW, HGW), 1) // HGH)
    first = lax.broadcasted_iota(jnp.int32, (16, HGW), 0) == 0

    def pad16(r):
        return jnp.where(first, jnp.broadcast_to(r, (16, HGW)), 0.0).astype(BF16)

    def body(i, carry):
        t = (tb - 1 - i) if rev else i
        q, kk, lf = _hgrn_gates(zq_ref[pl.ds(t, 1), :], zf_ref[pl.ds(t, 1), :], lbv)
        st = st_ref[...] * jnp.exp(lf) + jnp.where(
            head_mask, _dot_tn(pad16(zi_ref[pl.ds(t, 1), :]), pad16(kk)), 0.0)
        st_ref[...] = st
        o_ref[pl.ds(t, 1), :] = _dot_nt(pad16(q), st.astype(BF16))[0:1]
        return carry

    lax.fori_loop(0, tb, body, 0)


def _hgrn_direction(zq_ref, zf_ref, zi_ref, lbv, st_ref, o_ref, stall, dst, *, rev, tb):
    nc = tb // CHUNK
    zi = zi_ref[...]
    q, kk, lf = _hgrn_gates(zq_ref[...], zf_ref[...], lbv)
    a = _chunk_cumsum(lf, rev)
    a3 = a.reshape(nc, CHUNK, HGW)
    end_row = 0 if rev else CHUNK - 1
    mid_row = CHUNK // 2 if rev else CHUNK // 2 - 1
    a_end = a3[:, end_row:end_row + 1, :]
    a_mid = a3[:, mid_row:mid_row + 1, :]
    in_range = jnp.max(jnp.abs(a3 - a_mid)) <= PAIR_RANGE
    q3 = q.reshape(nc, CHUNK, HGW)
    k3 = kk.reshape(nc, CHUNK, HGW)
    qe = (q3 * jnp.exp(a3)).astype(BF16)
    ke = (k3 * jnp.exp(a_end - a3)).astype(BF16)
    qm = (q3 * jnp.exp(a3 - a_mid)).reshape(tb, HGW)
    km = (k3 * jnp.exp(a_mid - a3)).reshape(tb, HGW).astype(BF16)
    vb = zi.astype(BF16)
    v3 = vb.reshape(nc, CHUNK, HGW)
    dec = jnp.exp(a_end)

    lane_head = lax.broadcasted_iota(jnp.int32, (1, HGW), 1) // HGH
    sub = 128
    r_i = lax.broadcasted_iota(jnp.int32, (sub, sub), 0)
    c_i = lax.broadcasted_iota(jnp.int32, (sub, sub), 1)
    same = (r_i // CHUNK) == (c_i // CHUNK)
    causal = jnp.logical_and(same, (c_i >= r_i) if rev else (c_i <= r_i))
    causal4 = jnp.concatenate([causal] * HEADS, axis=0)
    intra = []
    for g in range(tb // sub):
        sl = slice(g * sub, (g + 1) * sub)
        qg = qm[sl]
        qstack = jnp.concatenate(
            [jnp.where(lane_head == hh, qg, 0.0) for hh in range(HEADS)], axis=0).astype(BF16)
        sc = _dot_nt(qstack, km[sl])
        sc = jnp.where(causal4, sc, 0.0).astype(BF16)
        r = _dot(sc, vb[sl])
        og = jnp.zeros((sub, HGW), F32)
        for hh in range(HEADS):
            og = og + jnp.where(lane_head == hh, r[hh * sub:(hh + 1) * sub], 0.0)
        intra.append(og)
    o_intra = jnp.concatenate(intra, axis=0)

    head_mask = (lax.broadcasted_iota(jnp.int32, (HGW, HGW), 0) // HGH
                 == lax.broadcasted_iota(jnp.int32, (HGW, HGW), 1) // HGH)
    for c in range(nc):
        dst[c] = jnp.where(head_mask, _dot_tn(v3[c], ke[c]), 0.0)
    st = st_ref[...]
    order = range(nc - 1, -1, -1) if rev else range(nc)
    for c in order:
        stall[c] = st.astype(BF16)
        st = dec[c] * st + dst[c]
    inter = [_dot_nt(qe[c], stall[c]) for c in range(nc)]
    o_ref[...] = o_intra + jnp.concatenate(inter, axis=0)

    @pl.when(in_range)
    def _():
        st_ref[...] = st

    @pl.when(jnp.logical_not(in_range))
    def _():
        _hgrn_token_scan(zq_ref, zf_ref, zi_ref, lbv, st_ref, o_ref, rev=rev, tb=tb)


def _hgrn_kernel(zqf_ref, zff_ref, zif_ref, zqb_ref, zfb_ref, zib_ref, lb_ref, s0_ref,
                 of_ref, ob_ref, s_ref, stall_f, dst_f, stall_b, dst_b, *, tb, layer):
    @pl.when(pl.program_id(1) == 0)
    def _():
        s_ref[...] = s0_ref[...]

    logits = lb_ref[...]
    e = jnp.exp(logits - jnp.max(logits, axis=0, keepdims=True))
    p = e / jnp.sum(e, axis=0, keepdims=True)
    lb = jnp.sum(p[0:layer + 1], axis=0) - p[0]
    _hgrn_direction(zqf_ref.at[0], zff_ref.at[0, 0], zif_ref.at[0], lb[0:1], s_ref.at[0, 0], of_ref.at[0],
                    stall_f, dst_f, rev=False, tb=tb)
    _hgrn_direction(zqb_ref.at[0], zfb_ref.at[0, 0], zib_ref.at[0], lb[1:2], s_ref.at[0, 1], ob_ref.at[0],
                    stall_b, dst_b, rev=True, tb=tb)


def _hgrn(zq, zf, zi, lb_logits, s0, *, layer, tb):
    b, t, _ = zq.shape
    nb = t // tb
    fwd = lambda bi, i: (bi, i, 0)
    bwd = lambda bi, i: (bi, nb - 1 - i, 0)
    in_specs = [pl.BlockSpec((1, tb, HGW), fwd),
                pl.BlockSpec((1, 1, tb, HGW), lambda bi, i: (bi, 0, i, 0)),
                pl.BlockSpec((1, tb, HGW), fwd),
                pl.BlockSpec((1, tb, HGW), bwd),
                pl.BlockSpec((1, 1, tb, HGW), lambda bi, i: (bi, 1, nb - 1 - i, 0)),
                pl.BlockSpec((1, tb, HGW), bwd),
                pl.BlockSpec(lb_logits.shape, lambda bi, i: (0, 0, 0)),
                pl.BlockSpec((1, 2, HGW, HGW), lambda bi, i: (bi, 0, 0, 0))]
    out_specs = (pl.BlockSpec((1, tb, HGW), fwd),
                 pl.BlockSpec((1, tb, HGW), bwd),
                 pl.BlockSpec((1, 2, HGW, HGW), lambda bi, i: (bi, 0, 0, 0)))
    out_shape = (jax.ShapeDtypeStruct((b, t, HGW), F32),
                 jax.ShapeDtypeStruct((b, t, HGW), F32),
                 jax.ShapeDtypeStruct((b, 2, HGW, HGW), F32))
    nc = tb // CHUNK
    of, ob, s_fin = pl.pallas_call(
        functools.partial(_hgrn_kernel, tb=tb, layer=layer),
        grid=(b, nb), in_specs=in_specs, out_specs=out_specs, out_shape=out_shape,
        scratch_shapes=[pltpu.VMEM((nc, HGW, HGW), BF16), pltpu.VMEM((nc, HGW, HGW), F32),
                        pltpu.VMEM((nc, HGW, HGW), BF16), pltpu.VMEM((nc, HGW, HGW), F32)],
        compiler_params=pltpu.CompilerParams(
            dimension_semantics=("parallel", "arbitrary"), vmem_limit_bytes=VMEM_LIMIT),
        name="hgrn",
    )(zq, zf, zi, zq, zf, zi, lb_logits, s0)
    return of, ob, s_fin


def _merge_kernel(h_ref, mod_ref, nm_ref, n2_ref, oda_ref, of_ref, ob_ref, hg_ref,
                  zp_ref, zpp_ref, zpn_ref, hgn_ref, pw_ref, ps_ref,
                  wg_ref, wpa_ref, wph_ref, wpp_ref, wo_ref, w1_ref, w3_ref, w2_ref, fn_ref,
                  out_ref, *, tm, t_total, final):
    i = pl.program_id(1)
    h1 = h_ref[0]
    m = mod_ref[0]

    x = zp_ref[0]
    prev = jnp.where(i > 0, zpp_ref[0], 0.0)
    nxt = jnp.where(i < pl.num_programs(1) - 1, zpn_ref[0], 0.0)
    e = jnp.concatenate([prev, x, nxt], axis=0)
    n = tm + 16
    a2 = e[0:n - 1] + e[1:n]
    a4 = a2[0:n - 3] + a2[2:n - 1]
    a8 = a4[0:n - 7] + a4[4:n - 3]
    a16 = a8[0:n - 15] + a8[8:n - 7]
    sums = (a2[7:7 + tm], a4[6:6 + tm], a8[4:4 + tm], a16[0:tm])
    pos = i * tm + lax.broadcasted_iota(jnp.int32, (tm, PW), 0)
    group = lax.broadcasted_iota(jnp.int32, (tm, PW), 1) // 64
    mixed = jnp.zeros((tm, PW), F32)
    for g, w in enumerate(POOL_WINDOWS):
        cnt = (jnp.minimum(pos + w // 2, t_total) - jnp.maximum(pos - w // 2, 0)).astype(F32)
        mixed = jnp.where(group == g, sums[g] / cnt - x, mixed)
    o_pool = _dot(mixed.astype(BF16), pw_ref[...]) * ps_ref[...]

    o = of_ref[0] + ob_ref[0]
    o2 = o * o
    hi = o2.astype(BF16)
    lo = (o2 - hi.astype(F32)).astype(BF16)
    ones_bd = (lax.broadcasted_iota(jnp.int32, (HGW, HGW), 0) // HGH
               == lax.broadcasted_iota(jnp.int32, (HGW, HGW), 1) // HGH).astype(BF16)
    seg = _dot(hi, ones_bd) + _dot(lo, ones_bd)
    zg = hg_ref[0]
    o_hg = (o * lax.rsqrt(seg * (1.0 / HGH) + EPS) * hgn_ref[...]) * (zg * _sigmoid(zg))

    u = _rms(h1, nm_ref[...]) * (1.0 + m[4:5]) + m[3:4]
    gate = _sigmoid(_dot(u.astype(BF16), wg_ref[...]))
    y = (gate[:, 0:D] * _dot(oda_ref[0], wpa_ref[...])
         + gate[:, D:2 * D] * _dot(o_hg.astype(BF16), wph_ref[...])
         + gate[:, 2 * D:3 * D] * _dot(o_pool.astype(BF16), wpp_ref[...]))
    mix = _dot(y.astype(BF16), wo_ref[...])
    h2 = h1 + m[5:6] * mix
    h3 = _swiglu_half(h2, m, 6, n2_ref[...], w1_ref, w3_ref, w2_ref)
    if final:
        h3 = _rms(h3, fn_ref[...])
    out_ref[0] = h3


def _merge(h1, mod, nm, n2, oda, of, ob, hg, zp, hgn, pw, ps, wg, wpa, wph, wpp, wo, w1, w3, w2, fn,
           *, tm, final):
    b, t, _ = h1.shape
    nblk8 = t // 8
    r = tm // 8
    tok = lambda w: pl.BlockSpec((1, tm, w), lambda bi, i: (bi, i, 0))
    in_specs = [tok(D),
                pl.BlockSpec((1, NMOD, D), lambda bi, i: (bi, 0, 0)),
                _const_spec((1, D)), _const_spec((1, D)),
                tok(QKW),
                tok(HGW), tok(HGW),
                tok(HGW), tok(PW),
                pl.BlockSpec((1, 8, PW), lambda bi, i: (bi, jnp.maximum(i * r - 1, 0), 0)),
                pl.BlockSpec((1, 8, PW), lambda bi, i: (bi, jnp.minimum((i + 1) * r, nblk8 - 1), 0)),
                _const_spec((1, HGW)), _const_spec((PW, PW)), _const_spec((1, PW)),
                _const_spec((D, 3 * D)), _const_spec((QKW, D)), _const_spec((HGW, D)),
                _const_spec((PW, D)), _const_spec((D, D)),
                _const_spec((D, DFF)), _const_spec((D, DFF)), _const_spec((DFF, D)),
                _const_spec((1, D))]
    return pl.pallas_call(
        functools.partial(_merge_kernel, tm=tm, t_total=t, final=final),
        grid=(b, t // tm), in_specs=in_specs, out_specs=tok(D),
        out_shape=jax.ShapeDtypeStruct((b, t, D), F32),
        compiler_params=pltpu.CompilerParams(
            dimension_semantics=("parallel", "parallel"), vmem_limit_bytes=VMEM_LIMIT),
        name="merge_ffn",
    )(h1, mod, nm, n2, oda, of, ob, hg, zp, zp, zp, hgn, pw, ps, wg, wpa, wph, wpp, wo, w1, w3, w2, fn)


def _rope_tables(n):
    rows = n // GRID_WIDTH
    row = jnp.repeat(jnp.arange(rows, dtype=jnp.int32), GRID_WIDTH).astype(F32)
    col = jnp.tile(jnp.arange(GRID_WIDTH, dtype=jnp.int32), rows).astype(F32)
    axis_dim = HALF // 2
    inv_freq = THETA ** (-jnp.arange(0, axis_dim, 2, dtype=F32) / axis_dim)
    ang_r = row[:, None] * inv_freq[None, :]
    ang_c = col[:, None] * inv_freq[None, :]
    cos64 = jnp.concatenate([jnp.cos(ang_r)] * 2 + [jnp.cos(ang_c)] * 2, axis=1)
    sin64 = jnp.concatenate([-jnp.sin(ang_r), jnp.sin(ang_r), -jnp.sin(ang_c), jnp.sin(ang_c)], axis=1)
    return jnp.tile(cos64, (1, 2)), jnp.tile(sin64, (1, 2))


def _block_diag(w):
    g, a, b = w.shape
    out = jnp.zeros((g * a, g * b), w.dtype)
    for i in range(g):
        out = out.at[i * a:(i + 1) * a, i * b:(i + 1) * b].set(w[i])
    return out


def kernel(x, c, ctx, c_ctx, w_ada, b_ada, norm_ffn1, norm_mix, norm_ffn2, ffn1_w1, ffn1_w3, ffn1_w2,
           ffn2_w1, ffn2_w3, ffn2_w2, w_in, da_lambda_q1, da_lambda_k1, da_lambda_q2, da_lambda_k2,
           da_subln, hg_lb_logits, hg_norm, pool_w, pool_scale, w_proj_da, w_proj_hg, w_proj_pool,
           w_out, final_norm):
    bsz, n, _ = x.shape
    n_ctx = ctx.shape[1]
    depth = w_ada.shape[0]
    lat, cx = _tiles(n), _tiles(n_ctx)

    cvec = jnp.zeros((8, D), F32).at[0:bsz].set(c).at[bsz].set(c_ctx)
    mods = _ada(cvec, w_ada, b_ada).reshape(depth, 8, NMOD, D)
    rc, rs = _rope_tables(n)
    rc_c = jnp.zeros((n_ctx, 128), F32)
    row = lambda v: v.reshape(1, -1)
    bf = lambda w: w.astype(BF16)

    h, hc = x, ctx
    for l in range(depth):
        need_ctx = l < depth - 1
        lam_init = 0.8 - 0.6 * math.exp(-0.3 * l)
        m_lat = mods[l, 0:bsz]
        m_ctx = jnp.broadcast_to(mods[l, bsz][None], (bsz, NMOD, D))
        w1a, w3a, w2a = bf(ffn1_w1[l]), bf(ffn1_w3[l]), bf(ffn1_w2[l])
        w1b, w3b, w2b = bf(ffn2_w1[l]), bf(ffn2_w3[l]), bf(ffn2_w2[l])
        win_a = bf(w_in[l][:, :IN_A])
        w_gate = bf(w_in[l][:, IN_A:])
        lamv = jnp.stack([da_lambda_q1[l], da_lambda_k1[l], da_lambda_q2[l], da_lambda_k2[l]]).astype(F32)
        gain_da = da_subln[l].reshape(VDIM, 1)
        hgn = row(jnp.tile(hg_norm[l], HEADS))
        pw_bd = bf(_block_diag(pool_w[l]))
        merge_w = (hgn, pw_bd, row(pool_scale[l]), w_gate, bf(w_proj_da[l]), bf(w_proj_hg[l]),
                   bf(w_proj_pool[l]), bf(w_out[l]), w1b, w3b, w2b, row(final_norm))

        (hc1, qt_c, k_c, vt_c, hq_c, hf_c, hi_c, hg_c, zp_c) = _ffn_in(
            hc, m_ctx, row(norm_ffn1[l]), row(norm_mix[l]), w1a, w3a, w2a, win_a, rc_c, rc_c,
            rope=False, tm=cx.tm)
        s0 = jnp.zeros((bsz, 2, HGW, HGW), F32)
        of_c, ob_c, s_ctx = _hgrn(hq_c, hf_c, hi_c, hg_lb_logits, s0, layer=l, tb=cx.tb)

        (h1, qt, k, vt, hq, hf, hi, hg, zp) = _ffn_in(
            h, m_lat, row(norm_ffn1[l]), row(norm_mix[l]), w1a, w3a, w2a, win_a, rc, rs,
            rope=True, tm=lat.tm)
        o_da = _attention(qt, k, vt, k_c, vt_c, lamv, gain_da, lam_init=lam_init, tq=lat.tq, tk=lat.tk)
        of, ob, _ = _hgrn(hq, hf, hi, hg_lb_logits, s_ctx, layer=l, tb=lat.tb)
        h = _merge(h1, m_lat, row(norm_mix[l]), row(norm_ffn2[l]), o_da, of, ob, hg, zp, *merge_w,
                   tm=lat.tm, final=not need_ctx)
        if need_ctx:
            o_dac = _attention(qt_c, k_c, vt_c, None, None, lamv, gain_da, lam_init=lam_init,
                               tq=cx.tq, tk=cx.tk)
            hc = _merge(hc1, m_ctx, row(norm_mix[l]), row(norm_ffn2[l]), o_dac, of_c, ob_c, hg_c, zp_c,
                        *merge_w, tm=cx.tm, final=False)
    return h
```

```python
import functools
import math
from typing import NamedTuple

import jax
import jax.numpy as jnp
from jax import lax
from jax.experimental import pallas as pl
from jax.experimental.pallas import tpu as pltpu

D = 1024
GRID_WIDTH = 64
EPS = 1e-6
LB_MIN = 1e-20
NMOD = 9
HEADS = 4
HALF = 64
VDIM = 128
QKW = HEADS * 2 * HALF
HGW = 256
HGH = 64
PW = 256
POOL_WINDOWS = (2, 4, 8, 16)
DFF = 2816
THETA = 10000.0
IN_A = 3 * QKW + 6 * 256
CHUNK = 64
PAIR_RANGE = 80.0
NEG = -1e30
LOG2E = 1.4426950408889634
BOUND_MARGIN = 1.01
L_FLOOR = 2.0 ** -86

F32 = jnp.float32
BF16 = jnp.bfloat16
VMEM_LIMIT = 56 * 1024 * 1024

TOKEN_TILE = 512
QUERY_TILE = 512
KEY_CHUNK = 2048
SCAN_BLOCK = 512


class _Tiles(NamedTuple):
    tm: int
    tq: int
    tk: int
    tb: int


def _tiles(n):
    return _Tiles(min(TOKEN_TILE, n), min(QUERY_TILE, n), min(KEY_CHUNK, n // 2), min(SCAN_BLOCK, n))


def _const_spec(shape):
    nd = len(shape)
    return pl.BlockSpec(shape, lambda *_: (0,) * nd, pipeline_mode=pl.Buffered(1))


def _rms(x, gain):
    return x * lax.rsqrt(jnp.mean(x * x, axis=-1, keepdims=True) + EPS) * gain


def _sigmoid(x):
    return 1.0 / (1.0 + jnp.exp(-x))


def _dot(a, b):
    return jnp.dot(a, b, preferred_element_type=F32)


def _dot_nt(a, b):
    return lax.dot_general(a, b, (((1,), (1,)), ((), ())), preferred_element_type=F32)


def _dot_tn(a, b):
    return lax.dot_general(a, b, (((0,), (0,)), ((), ())), preferred_element_type=F32)


def _ada_kernel(c_ref, w_ref, b_ref, o_ref):
    c = c_ref[...]
    s = c * _sigmoid(c)
    o_ref[0] = jnp.dot(s, w_ref[0], preferred_element_type=F32,
                       precision=lax.Precision.HIGHEST) + b_ref[0]


def _ada(cvec, w_ada, b_ada):
    depth = w_ada.shape[0]
    nblk = (NMOD * D) // D
    return pl.pallas_call(
        _ada_kernel,
        grid=(depth, nblk),
        in_specs=[pl.BlockSpec((8, D), lambda l, j: (0, 0)),
                  pl.BlockSpec((1, D, D), lambda l, j: (l, 0, j)),
                  pl.BlockSpec((1, 1, D), lambda l, j: (l, 0, j))],
        out_specs=pl.BlockSpec((1, 8, D), lambda l, j: (l, 0, j)),
        out_shape=jax.ShapeDtypeStruct((depth, 8, NMOD * D), F32),
        name="ada",
    )(cvec, w_ada, b_ada.reshape(depth, 1, NMOD * D))


def _swiglu_half(h, m, shift_i, gain, w1_ref, w3_ref, w2_ref):
    x = _rms(h, gain) * (1.0 + m[shift_i + 1:shift_i + 2]) + m[shift_i:shift_i + 1]
    xb = x.astype(BF16)
    a = _dot(xb, w1_ref[...])
    b = _dot(xb, w3_ref[...])
    g = (a * _sigmoid(a) * b).astype(BF16)
    y = _dot(g, w2_ref[...])
    return h + 0.5 * m[shift_i + 2:shift_i + 3] * y


def _ffn_in_kernel(h_ref, mod_ref, n1_ref, nm_ref, w1_ref, w3_ref, w2_ref, win_ref, rc_ref, rs_ref,
                   h1_ref, qt_ref, k_ref, vt_ref, hq_ref, hf_ref, hi_ref, hg_ref, zp_ref, *, rope):
    h = h_ref[0]
    m = mod_ref[0]
    h1 = _swiglu_half(h, m, 0, n1_ref[...], w1_ref, w3_ref, w2_ref)
    h1_ref[0] = h1
    u = _rms(h1, nm_ref[...]) * (1.0 + m[4:5]) + m[3:4]
    z = _dot(u.astype(BF16), win_ref[...])
    q = z[:, 0:QKW]
    k = z[:, QKW:2 * QKW]
    v = z[:, 2 * QKW:3 * QKW]
    if rope:
        lane = lax.broadcasted_iota(jnp.int32, q.shape, 1)
        first = (lane % 32) < 16
        rc = jnp.concatenate([rc_ref[...]] * (QKW // 128), axis=1)
        rs = jnp.concatenate([rs_ref[...]] * (QKW // 128), axis=1)

        def rot(x):
            partner = jnp.where(first, pltpu.roll(x, QKW - 16, 1), pltpu.roll(x, 16, 1))
            return x * rc + partner * rs

        q = rot(q)
        k = rot(k)
    q = q * (HALF ** -0.5 * LOG2E)
    qt_ref[0] = q.T.astype(BF16)
    k_ref[0] = k.astype(BF16)
    vt_ref[0] = v.T.astype(BF16)
    o = 3 * QKW
    hq_ref[0] = z[:, o:o + 256]
    hf_ref[0, 0] = z[:, o + 256:o + 512]
    hf_ref[0, 1] = z[:, o + 512:o + 768]
    hi_ref[0] = z[:, o + 768:o + 1024]
    hg_ref[0] = z[:, o + 1024:o + 1280]
    zp_ref[0] = z[:, o + 1280:o + 1536]


def _ffn_in(h, mod, n1, nm, w1, w3, w2, win, rc, rs, *, rope, tm):
    b, t, _ = h.shape
    grid = (b, t // tm)
    tok = lambda w: pl.BlockSpec((1, tm, w), lambda bi, i: (bi, i, 0))
    tokt = pl.BlockSpec((1, QKW, tm), lambda bi, i: (bi, 0, i))
    out_shape = (
        jax.ShapeDtypeStruct((b, t, D), F32),
        jax.ShapeDtypeStruct((b, QKW, t), BF16),
        jax.ShapeDtypeStruct((b, t, QKW), BF16),
        jax.ShapeDtypeStruct((b, QKW, t), BF16),
        jax.ShapeDtypeStruct((b, t, 256), F32),
        jax.ShapeDtypeStruct((b, 2, t, 256), F32),
        jax.ShapeDtypeStruct((b, t, 256), F32),
        jax.ShapeDtypeStruct((b, t, 256), F32),
        jax.ShapeDtypeStruct((b, t, 256), F32),
    )
    out_specs = (tok(D), tokt, tok(QKW), tokt, tok(256),
                 pl.BlockSpec((1, 2, tm, 256), lambda bi, i: (bi, 0, i, 0)),
                 tok(256), tok(256), tok(256))
    in_specs = [tok(D),
                pl.BlockSpec((1, NMOD, D), lambda bi, i: (bi, 0, 0)),
                _const_spec((1, D)), _const_spec((1, D)),
                _const_spec((D, DFF)), _const_spec((D, DFF)), _const_spec((DFF, D)),
                _const_spec((D, IN_A)),
                pl.BlockSpec((tm, 128), lambda bi, i: (i, 0)),
                pl.BlockSpec((tm, 128), lambda bi, i: (i, 0))]
    return pl.pallas_call(
        functools.partial(_ffn_in_kernel, rope=rope),
        grid=grid, in_specs=in_specs, out_specs=out_specs, out_shape=out_shape,
        compiler_params=pltpu.CompilerParams(
            dimension_semantics=("parallel", "parallel"), vmem_limit_bytes=VMEM_LIMIT),
        name="ffn_in",
    )(h, mod, n1, nm, w1, w3, w2, win, rc, rs)


def _attn_kernel(*refs, tq, tk, n_main, has_extra, lam_init):
    if has_extra:
        (qt_ref, k_ref, vt_ref, kx_ref, vtx_ref, lam_ref, gain_ref, o_ref,
         acc1, acc2, l_sc, kmax) = refs
    else:
        (qt_ref, k_ref, vt_ref, lam_ref, gain_ref, o_ref, acc1, acc2, l_sc, kmax) = refs
        kx_ref = vtx_ref = None
    qt = qt_ref[0]
    row = lax.broadcasted_iota(jnp.int32, qt.shape, 0)
    zero = jnp.zeros_like(qt)
    qbd = jnp.concatenate([jnp.where(row < HALF, qt, zero), jnp.where(row >= HALF, qt, zero)], axis=1)

    def kchunk(c):
        return k_ref[0, pl.ds(pl.multiple_of(c * tk, tk), tk), :]

    def vchunk(c):
        return vt_ref[0, :, pl.ds(pl.multiple_of(c * tk, tk), tk)]

    @pl.when(pl.program_id(2) == 0)
    def _():
        lane = lax.broadcasted_iota(jnp.int32, (16, VDIM), 1)
        r16 = lax.broadcasted_iota(jnp.int32, (16, VDIM), 0)
        sel = jnp.where((r16 == 0) & (lane < HALF) | (r16 == 1) & (lane >= HALF), 1.0, 0.0).astype(BF16)

        def sq_norms(kb):
            kf = kb.astype(F32)
            return _dot_nt(sel, (kf * kf).astype(BF16))

        n2 = lax.fori_loop(0, n_main, lambda c, mx: jnp.maximum(mx, sq_norms(kchunk(c))),
                           jnp.zeros((16, tk), F32))
        top = jnp.max(n2, axis=1, keepdims=True)
        if has_extra:
            top = jnp.maximum(top, jnp.max(sq_norms(kx_ref[0]), axis=1, keepdims=True))
        kmax[...] = jnp.broadcast_to(jnp.sqrt(top[0:8]), kmax.shape)

    qf = qt.astype(F32)
    q1 = jnp.sqrt(jnp.sum(jnp.where(row < HALF, qf * qf, 0.0), axis=0, keepdims=True))
    q2 = jnp.sqrt(jnp.sum(jnp.where(row >= HALF, qf * qf, 0.0), axis=0, keepdims=True))
    km = kmax[...]
    bound = jnp.concatenate([q1 * km[0:1, 0:1], q2 * km[1:2, 0:1]], axis=1) * BOUND_MARGIN
    acc1[...] = jnp.zeros_like(acc1)
    acc2[...] = jnp.zeros_like(acc2)

    def fast_step(kb, vtb, l):
        p = jnp.exp2(_dot(kb, qbd) - bound)
        pb = p.astype(BF16)
        acc1[...] += _dot(vtb, pb[:, :tq])
        acc2[...] += _dot(vtb, pb[:, tq:])
        return l + jnp.sum(p, axis=0, keepdims=True)

    l = lax.fori_loop(0, n_main, lambda c, l: fast_step(kchunk(c), vchunk(c), l),
                      jnp.zeros((1, 2 * tq), F32), unroll=2)
    if has_extra:
        l = fast_step(kx_ref[0], vtx_ref[0], l)
    l_sc[...] = jnp.broadcast_to(l, l_sc.shape)

    @pl.when(jnp.logical_not(jnp.min(l) >= L_FLOOR))
    def _():
        acc1[...] = jnp.zeros_like(acc1)
        acc2[...] = jnp.zeros_like(acc2)

        def safe_step(kb, vtb, carry):
            m, ls = carry
            s = _dot(kb, qbd)
            m_new = jnp.maximum(m, jnp.max(s, axis=0, keepdims=True))
            alpha = jnp.exp2(m - m_new)
            p = jnp.exp2(s - m_new)
            pb = p.astype(BF16)
            acc1[...] = acc1[...] * alpha[:, :tq] + _dot(vtb, pb[:, :tq])
            acc2[...] = acc2[...] * alpha[:, tq:] + _dot(vtb, pb[:, tq:])
            return m_new, alpha * ls + jnp.sum(p, axis=0, keepdims=True)

        carry = (jnp.full((1, 2 * tq), NEG, F32), jnp.zeros((1, 2 * tq), F32))
        carry = lax.fori_loop(0, n_main, lambda c, cr: safe_step(kchunk(c), vchunk(c), cr), carry)
        if has_extra:
            carry = safe_step(kx_ref[0], vtx_ref[0], carry)
        l_sc[...] = jnp.broadcast_to(carry[1], l_sc.shape)

    l = l_sc[0:1, :]

    lv = lam_ref[...]
    lam = (jnp.exp(jnp.sum(lv[0:1] * lv[1:2], axis=1, keepdims=True))
           - jnp.exp(jnp.sum(lv[2:3] * lv[3:4], axis=1, keepdims=True)) + lam_init)
    o = acc1[...] / l[:, :tq] - lam * (acc2[...] / l[:, tq:])
    o = o * lax.rsqrt(jnp.mean(o * o, axis=0, keepdims=True) + EPS) * gain_ref[...] * (1.0 - lam_init)
    o_ref[0] = o.T.astype(BF16)


def _attention(qt, k, vt, kx, vtx, lamv, gain, *, lam_init, tq, tk):
    b, _, t_q = qt.shape
    t_k = k.shape[1]
    has_extra = kx is not None
    assert t_k % (2 * tk) == 0
    grid = (b, HEADS, t_q // tq)
    in_specs = [pl.BlockSpec((1, VDIM, tq), lambda bi, hi, i: (bi, hi, i)),
                pl.BlockSpec((1, t_k, VDIM), lambda bi, hi, i: (bi, 0, hi)),
                pl.BlockSpec((1, VDIM, t_k), lambda bi, hi, i: (bi, hi, 0))]
    args = [qt, k, vt]
    if has_extra:
        t_x = kx.shape[1]
        in_specs += [pl.BlockSpec((1, t_x, VDIM), lambda bi, hi, i: (bi, 0, hi)),
                     pl.BlockSpec((1, VDIM, t_x), lambda bi, hi, i: (bi, hi, 0))]
        args += [kx, vtx]
    in_specs += [pl.BlockSpec((4, HALF), lambda bi, hi, i: (0, 0)),
                 pl.BlockSpec((VDIM, 1), lambda bi, hi, i: (0, 0))]
    args += [lamv, gain]
    return pl.pallas_call(
        functools.partial(_attn_kernel, tq=tq, tk=tk, n_main=t_k // tk, has_extra=has_extra,
                          lam_init=lam_init),
        grid=grid, in_specs=in_specs,
        out_specs=pl.BlockSpec((1, tq, VDIM), lambda bi, hi, i: (bi, i, hi)),
        out_shape=jax.ShapeDtypeStruct((b, t_q, HEADS * VDIM), BF16),
        scratch_shapes=[pltpu.VMEM((VDIM, tq), F32), pltpu.VMEM((VDIM, tq), F32),
                        pltpu.VMEM((8, 2 * tq), F32), pltpu.VMEM((8, 128), F32)],
        compiler_params=pltpu.CompilerParams(
            dimension_semantics=("parallel", "parallel", "arbitrary"), vmem_limit_bytes=VMEM_LIMIT),
        name="diff_attn",
    )(*args)


def _chunk_cumsum(x, rev):
    n = x.shape[0]
    pos = lax.broadcasted_iota(jnp.int32, x.shape, 0) % CHUNK
    sh = 1
    while sh < CHUNK:
        if rev:
            x = x + jnp.where(pos + sh < CHUNK, pltpu.roll(x, n - sh, 0), 0.0)
        else:
            x = x + jnp.where(pos >= sh, pltpu.roll(x, sh, 0), 0.0)
        sh *= 2
    return x


def _hgrn_gates(zq, zf, lbv):
    q = zq * _sigmoid(zq)
    e = jnp.exp(-jnp.abs(zf))
    r = 1.0 / (1.0 + e)
    er = e * r
    pos = zf >= 0.0
    kk = (1.0 - lbv) * jnp.where(pos, er, r)
    lf = jnp.log(jnp.maximum(lbv, LB_MIN) + (1.0 - lbv) * jnp.where(pos, r, er))
    return q, kk, lf


def _hgrn_token_scan(zq_ref, zf_ref, zi_ref, lbv, st_ref, o_ref, *, rev, tb):
    head_mask = (lax.broadcasted_iota(jnp.int32, (HGW, HGW), 0) // HGH
                 == lax.broadcasted_iota(jnp.int32, (HGW, HGW), 1) // HGH)
    first = lax.broadcasted_iota(jnp.int32, (16, HGW), 0) == 0

    def pad16(r):
        return jnp.where(first, jnp.broadcast_to(r, (16, HGW)), 0.0).astype(BF16)

    def body(i, carry):
        t = (tb - 1 - i) if rev else i
        q, kk, lf = _hgrn_gates(zq_ref[pl.ds(t, 1), :], zf_ref[pl.ds(t, 1), :], lbv)
        st = st_ref[...] * jnp.exp(lf) + jnp.where(
            head_mask, _dot_tn(pad16(zi_ref[pl.ds(t, 1), :]), pad16(kk)), 0.0)
        st_ref[...] = st
        o_ref[pl.ds(t, 1), :] = _dot_nt(pad16(q), st.astype(BF16))[0:1]
        return carry

    lax.fori_loop(0, tb, body, 0)


def _hgrn_direction(zq_ref, zf_ref, zi_ref, lbv, st_ref, o_ref, stall, dst, *, rev, tb):
    nc = tb // CHUNK
    zi = zi_ref[...]
    q, kk, lf = _hgrn_gates(zq_ref[...], zf_ref[...], lbv)
    a = _chunk_cumsum(lf, rev)
    a3 = a.reshape(nc, CHUNK, HGW)
    end_row = 0 if rev else CHUNK - 1
    mid_row = CHUNK // 2 if rev else CHUNK // 2 - 1
    a_end = a3[:, end_row:end_row + 1, :]
    a_mid = a3[:, mid_row:mid_row + 1, :]
    in_range = jnp.max(jnp.abs(a3 - a_mid)) <= PAIR_RANGE
    q3 = q.reshape(nc, CHUNK, HGW)
    k3 = kk.reshape(nc, CHUNK, HGW)
    qe = (q3 * jnp.exp(a3)).astype(BF16)
    ke = (k3 * jnp.exp(a_end - a3)).astype(BF16)
    qm = (q3 * jnp.exp(a3 - a_mid)).reshape(tb, HGW)
    km = (k3 * jnp.exp(a_mid - a3)).reshape(tb, HGW).astype(BF16)
    vb = zi.astype(BF16)
    v3 = vb.reshape(nc, CHUNK, HGW)
    dec = jnp.exp(a_end)

    lane_head = lax.broadcasted_iota(jnp.int32, (1, HGW), 1) // HGH
    sub = 128
    r_i = lax.broadcasted_iota(jnp.int32, (sub, sub), 0)
    c_i = lax.broadcasted_iota(jnp.int32, (sub, sub), 1)
    same = (r_i // CHUNK) == (c_i // CHUNK)
    causal = jnp.logical_and(same, (c_i >= r_i) if rev else (c_i <= r_i))
    causal4 = jnp.concatenate([causal] * HEADS, axis=0)
    intra = []
    for g in range(tb // sub):
        sl = slice(g * sub, (g + 1) * sub)
        qg = qm[sl]
        qstack = jnp.concatenate(
            [jnp.where(lane_head == hh, qg, 0.0) for hh in range(HEADS)], axis=0).astype(BF16)
        sc = _dot_nt(qstack, km[sl])
        sc = jnp.where(causal4, sc, 0.0).astype(BF16)
        r = _dot(sc, vb[sl])
        og = jnp.zeros((sub, HGW), F32)
        for hh in range(HEADS):
            og = og + jnp.where(lane_head == hh, r[hh * sub:(hh + 1) * sub], 0.0)
        intra.append(og)
    o_intra = jnp.concatenate(intra, axis=0)

    head_mask = (lax.broadcasted_iota(jnp.int32, (HGW, HGW), 0) // HGH
                 == lax.broadcasted_iota(jnp.int32, (HGW, HGW), 1) // HGH)
    for c in range(nc):
        dst[c] = jnp.where(head_mask, _dot_tn(v3[c], ke[c]), 0.0)
    st = st_ref[...]
    order = range(nc - 1, -1, -1) if rev else range(nc)
    for c in order:
        stall[c] = st.astype(BF16)
        st = dec[c] * st + dst[c]
    inter = [_dot_nt(qe[c], stall[c]) for c in range(nc)]
    o_ref[...] = o_intra + jnp.concatenate(inter, axis=0)

    @pl.when(in_range)
    def _():
        st_ref[...] = st

    @pl.when(jnp.logical_not(in_range))
    def _():
        _hgrn_token_scan(zq_ref, zf_ref, zi_ref, lbv, st_ref, o_ref, rev=rev, tb=tb)


def _hgrn_kernel(zqf_ref, zff_ref, zif_ref, zqb_ref, zfb_ref, zib_ref, lb_ref, s0_ref,
                 of_ref, ob_ref, s_ref, stall_f, dst_f, stall_b, dst_b, *, tb, layer):
    @pl.when(pl.program_id(1) == 0)
    def _():
        s_ref[...] = s0_ref[...]

    logits = lb_ref[...]
    e = jnp.exp(logits - jnp.max(logits, axis=0, keepdims=True))
    p = e / jnp.sum(e, axis=0, keepdims=True)
    lb = jnp.sum(p[0:layer + 1], axis=0) - p[0]
    _hgrn_direction(zqf_ref.at[0], zff_ref.at[0, 0], zif_ref.at[0], lb[0:1], s_ref.at[0, 0], of_ref.at[0],
                    stall_f, dst_f, rev=False, tb=tb)
    _hgrn_direction(zqb_ref.at[0], zfb_ref.at[0, 0], zib_ref.at[0], lb[1:2], s_ref.at[0, 1], ob_ref.at[0],
                    stall_b, dst_b, rev=True, tb=tb)


def _hgrn(zq, zf, zi, lb_logits, s0, *, layer, tb):
    b, t, _ = zq.shape
    nb = t // tb
    fwd = lambda bi, i: (bi, i, 0)
    bwd = lambda bi, i: (bi, nb - 1 - i, 0)
    in_specs = [pl.BlockSpec((1, tb, HGW), fwd),
                pl.BlockSpec((1, 1, tb, HGW), lambda bi, i: (bi, 0, i, 0)),
                pl.BlockSpec((1, tb, HGW), fwd),
                pl.BlockSpec((1, tb, HGW), bwd),
                pl.BlockSpec((1, 1, tb, HGW), lambda bi, i: (bi, 1, nb - 1 - i, 0)),
                pl.BlockSpec((1, tb, HGW), bwd),
                pl.BlockSpec(lb_logits.shape, lambda bi, i: (0, 0, 0)),
                pl.BlockSpec((1, 2, HGW, HGW), lambda bi, i: (bi, 0, 0, 0))]
    out_specs = (pl.BlockSpec((1, tb, HGW), fwd),
                 pl.BlockSpec((1, tb, HGW), bwd),
                 pl.BlockSpec((1, 2, HGW, HGW), lambda bi, i: (bi, 0, 0, 0)))
    out_shape = (jax.ShapeDtypeStruct((b, t, HGW), F32),
                 jax.ShapeDtypeStruct((b, t, HGW), F32),
                 jax.ShapeDtypeStruct((b, 2, HGW, HGW), F32))
    nc = tb // CHUNK
    of, ob, s_fin = pl.pallas_call(
        functools.partial(_hgrn_kernel, tb=tb, layer=layer),
        grid=(b, nb), in_specs=in_specs, out_specs=out_specs, out_shape=out_shape,
        scratch_shapes=[pltpu.VMEM((nc, HGW, HGW), BF16), pltpu.VMEM((nc, HGW, HGW), F32),
                        pltpu.VMEM((nc, HGW, HGW), BF16), pltpu.VMEM((nc, HGW, HGW), F32)],
        compiler_params=pltpu.CompilerParams(
            dimension_semantics=("parallel", "arbitrary"), vmem_limit_bytes=VMEM_LIMIT),
        name="hgrn",
    )(zq, zf, zi, zq, zf, zi, lb_logits, s0)
    return of, ob, s_fin


def _merge_kernel(h_ref, mod_ref, nm_ref, n2_ref, oda_ref, of_ref, ob_ref, hg_ref,
                  zp_ref, zpp_ref, zpn_ref, hgn_ref, pw_ref, ps_ref,
                  wg_ref, wpa_ref, wph_ref, wpp_ref, wo_ref, w1_ref, w3_ref, w2_ref, fn_ref,
                  out_ref, *, tm, t_total, final):
    i = pl.program_id(1)
    h1 = h_ref[0]
    m = mod_ref[0]

    x = zp_ref[0]
    prev = jnp.where(i > 0, zpp_ref[0], 0.0)
    nxt = jnp.where(i < pl.num_programs(1) - 1, zpn_ref[0], 0.0)
    e = jnp.concatenate([prev, x, nxt], axis=0)
    n = tm + 16
    a2 = e[0:n - 1] + e[1:n]
    a4 = a2[0:n - 3] + a2[2:n - 1]
    a8 = a4[0:n - 7] + a4[4:n - 3]
    a16 = a8[0:n - 15] + a8[8:n - 7]
    sums = (a2[7:7 + tm], a4[6:6 + tm], a8[4:4 + tm], a16[0:tm])
    pos = i * tm + lax.broadcasted_iota(jnp.int32, (tm, PW), 0)
    group = lax.broadcasted_iota(jnp.int32, (tm, PW), 1) // 64
    mixed = jnp.zeros((tm, PW), F32)
    for g, w in enumerate(POOL_WINDOWS):
        cnt = (jnp.minimum(pos + w // 2, t_total) - jnp.maximum(pos - w // 2, 0)).astype(F32)
        mixed = jnp.where(group == g, sums[g] / cnt - x, mixed)
    o_pool = _dot(mixed.astype(BF16), pw_ref[...]) * ps_ref[...]

    o = of_ref[0] + ob_ref[0]
    o2 = o * o
    hi = o2.astype(BF16)
    lo = (o2 - hi.astype(F32)).astype(BF16)
    ones_bd = (lax.broadcasted_iota(jnp.int32, (HGW, HGW), 0) // HGH
               == lax.broadcasted_iota(jnp.int32, (HGW, HGW), 1) // HGH).astype(BF16)
    seg = _dot(hi, ones_bd) + _dot(lo, ones_bd)
    zg = hg_ref[0]
    o_hg = (o * lax.rsqrt(seg * (1.0 / HGH) + EPS) * hgn_ref[...]) * (zg * _sigmoid(zg))

    u = _rms(h1, nm_ref[...]) * (1.0 + m[4:5]) + m[3:4]
    gate = _sigmoid(_dot(u.astype(BF16), wg_ref[...]))
    y = (gate[:, 0:D] * _dot(oda_ref[0], wpa_ref[...])
         + gate[:, D:2 * D] * _dot(o_hg.astype(BF16), wph_ref[...])
         + gate[:, 2 * D:3 * D] * _dot(o_pool.astype(BF16), wpp_ref[...]))
    mix = _dot(y.astype(BF16), wo_ref[...])
    h2 = h1 + m[5:6] * mix
    h3 = _swiglu_half(h2, m, 6, n2_ref[...], w1_ref, w3_ref, w2_ref)
    if final:
        h3 = _rms(h3, fn_ref[...])
    out_ref[0] = h3


def _merge(h1, mod, nm, n2, oda, of, ob, hg, zp, hgn, pw, ps, wg, wpa, wph, wpp, wo, w1, w3, w2, fn,
           *, tm, final):
    b, t, _ = h1.shape
    nblk8 = t // 8
    r = tm // 8
    tok = lambda w: pl.BlockSpec((1, tm, w), lambda bi, i: (bi, i, 0))
    in_specs = [tok(D),
                pl.BlockSpec((1, NMOD, D), lambda bi, i: (bi, 0, 0)),
                _const_spec((1, D)), _const_spec((1, D)),
                tok(QKW),
                tok(HGW), tok(HGW),
                tok(HGW), tok(PW),
                pl.BlockSpec((1, 8, PW), lambda bi, i: (bi, jnp.maximum(i * r - 1, 0), 0)),
                pl.BlockSpec((1, 8, PW), lambda bi, i: (bi, jnp.minimum((i + 1) * r, nblk8 - 1), 0)),
                _const_spec((1, HGW)), _const_spec((PW, PW)), _const_spec((1, PW)),
                _const_spec((D, 3 * D)), _const_spec((QKW, D)), _const_spec((HGW, D)),
                _const_spec((PW, D)), _const_spec((D, D)),
                _const_spec((D, DFF)), _const_spec((D, DFF)), _const_spec((DFF, D)),
                _const_spec((1, D))]
    return pl.pallas_call(
        functools.partial(_merge_kernel, tm=tm, t_total=t, final=final),
        grid=(b, t // tm), in_specs=in_specs, out_specs=tok(D),
        out_shape=jax.ShapeDtypeStruct((b, t, D), F32),
        compiler_params=pltpu.CompilerParams(
            dimension_semantics=("parallel", "parallel"), vmem_limit_bytes=VMEM_LIMIT),
        name="merge_ffn",
    )(h1, mod, nm, n2, oda, of, ob, hg, zp, zp, zp, hgn, pw, ps, wg, wpa, wph, wpp, wo, w1, w3, w2, fn)


def _rope_tables(n):
    rows = n // GRID_WIDTH
    row = jnp.repeat(jnp.arange(rows, dtype=jnp.int32), GRID_WIDTH).astype(F32)
    col = jnp.tile(jnp.arange(GRID_WIDTH, dtype=jnp.int32), rows).astype(F32)
    axis_dim = HALF // 2
    inv_freq = THETA ** (-jnp.arange(0, axis_dim, 2, dtype=F32) / axis_dim)
    ang_r = row[:, None] * inv_freq[None, :]
    ang_c = col[:, None] * inv_freq[None, :]
    cos64 = jnp.concatenate([jnp.cos(ang_r)] * 2 + [jnp.cos(ang_c)] * 2, axis=1)
    sin64 = jnp.concatenate([-jnp.sin(ang_r), jnp.sin(ang_r), -jnp.sin(ang_c), jnp.sin(ang_c)], axis=1)
    return jnp.tile(cos64, (1, 2)), jnp.tile(sin64, (1, 2))


def _block_diag(w):
    g, a, b = w.shape
    out = jnp.zeros((g * a, g * b), w.dtype)
    for i in range(g):
        out = out.at[i * a:(i + 1) * a, i * b:(i + 1) * b].set(w[i])
    return out


def kernel(x, c, ctx, c_ctx, w_ada, b_ada, norm_ffn1, norm_mix, norm_ffn2, ffn1_w1, ffn1_w3, ffn1_w2,
           ffn2_w1, ffn2_w3, ffn2_w2, w_in, da_lambda_q1, da_lambda_k1, da_lambda_q2, da_lambda_k2,
           da_subln, hg_lb_logits, hg_norm, pool_w, pool_scale, w_proj_da, w_proj_hg, w_proj_pool,
           w_out, final_norm):
    bsz, n, _ = x.shape
    n_ctx = ctx.shape[1]
    depth = w_ada.shape[0]
    lat, cx = _tiles(n), _tiles(n_ctx)

    cvec = jnp.zeros((8, D), F32).at[0:bsz].set(c).at[bsz].set(c_ctx)
    mods = _ada(cvec, w_ada, b_ada).reshape(depth, 8, NMOD, D)
    rc, rs = _rope_tables(n)
    rc_c = jnp.zeros((n_ctx, 128), F32)
    row = lambda v: v.reshape(1, -1)
    bf = lambda w: w.astype(BF16)

    h, hc = x, ctx
    for l in range(depth):
        need_ctx = l < depth - 1
        lam_init = 0.8 - 0.6 * math.exp(-0.3 * l)
        m_lat = mods[l, 0:bsz]
        m_ctx = jnp.broadcast_to(mods[l, bsz][None], (bsz, NMOD, D))
        w1a, w3a, w2a = bf(ffn1_w1[l]), bf(ffn1_w3[l]), bf(ffn1_w2[l])
        w1b, w3b, w2b = bf(ffn2_w1[l]), bf(ffn2_w3[l]), bf(ffn2_w2[l])
        win_a = bf(w_in[l][:, :IN_A])
        w_gate = bf(w_in[l][:, IN_A:])
        lamv = jnp.stack([da_lambda_q1[l], da_lambda_k1[l], da_lambda_q2[l], da_lambda_k2[l]]).astype(F32)
        gain_da = da_subln[l].reshape(VDIM, 1)
        hgn = row(jnp.tile(hg_norm[l], HEADS))
        pw_bd = bf(_block_diag(pool_w[l]))
        merge_w = (hgn, pw_bd, row(pool_scale[l]), w_gate, bf(w_proj_da[l]), bf(w_proj_hg[l]),
                   bf(w_proj_pool[l]), bf(w_out[l]), w1b, w3b, w2b, row(final_norm))

        (hc1, qt_c, k_c, vt_c, hq_c, hf_c, hi_c, hg_c, zp_c) = _ffn_in(
            hc, m_ctx, row(norm_ffn1[l]), row(norm_mix[l]), w1a, w3a, w2a, win_a, rc_c, rc_c,
            rope=False, tm=cx.tm)
        s0 = jnp.zeros((bsz, 2, HGW, HGW), F32)
        of_c, ob_c, s_ctx = _hgrn(hq_c, hf_c, hi_c, hg_lb_logits, s0, layer=l, tb=cx.tb)

        (h1, qt, k, vt, hq, hf, hi, hg, zp) = _ffn_in(
            h, m_lat, row(norm_ffn1[l]), row(norm_mix[l]), w1a, w3a, w2a, win_a, rc, rs,
            rope=True, tm=lat.tm)
        o_da = _attention(qt, k, vt, k_c, vt_c, lamv, gain_da, lam_init=lam_init, tq=lat.tq, tk=lat.tk)
        of, ob, _ = _hgrn(hq, hf, hi, hg_lb_logits, s_ctx, layer=l, tb=lat.tb)
        h = _merge(h1, m_lat, row(norm_mix[l]), row(norm_ffn2[l]), o_da, of, ob, hg, zp, *merge_w,
                   tm=lat.tm, final=not need_ctx)
        if need_ctx:
            o_dac = _attention(qt_c, k_c, vt_c, None, None, lamv, gain_da, lam_init=lam_init,
                               tq=cx.tq, tk=cx.tk)
            hc = _merge(hc1, m_ctx, row(norm_mix[l]), row(norm_ffn2[l]), o_dac, of_c, ob_c, hg_c, zp_c,
                        *merge_w, tm=cx.tm, final=False)
    return h
```

```python
import functools
import math
from typing import NamedTuple

import jax
import jax.numpy as jnp
from jax import lax
from jax.experimental import pallas as pl
from jax.experimental.pallas import tpu as pltpu

D = 1024
GRID_WIDTH = 64
EPS = 1e-6
LB_MIN = 1e-20
NMOD = 9
HEADS = 4
HALF = 64
VDIM = 128
QKW = HEADS * 2 * HALF
HGW = 256
HGH = 64
PW = 256
POOL_WINDOWS = (2, 4, 8, 16)
DFF = 2816
THETA = 10000.0
IN_A = 3 * QKW + 6 * 256
CHUNK = 64
PAIR_RANGE = 80.0
NEG = -1e30
LOG2E = 1.4426950408889634
BOUND_MARGIN = 1.01
L_FLOOR = 2.0 ** -86

F32 = jnp.float32
BF16 = jnp.bfloat16
VMEM_LIMIT = 56 * 1024 * 1024

TOKEN_TILE = 512
QUERY_TILE = 512
KEY_CHUNK = 2048
SCAN_BLOCK = 512


class _Tiles(NamedTuple):
    tm: int
    tq: int
    tk: int
    tb: int


def _tiles(n):
    return _Tiles(min(TOKEN_TILE, n), min(QUERY_TILE, n), min(KEY_CHUNK, n // 2), min(SCAN_BLOCK, n))


def _const_spec(shape):
    nd = len(shape)
    return pl.BlockSpec(shape, lambda *_: (0,) * nd, pipeline_mode=pl.Buffered(1))


def _rms(x, gain):
    return x * lax.rsqrt(jnp.mean(x * x, axis=-1, keepdims=True) + EPS) * gain


def _sigmoid(x):
    return 1.0 / (1.0 + jnp.exp(-x))


def _dot(a, b):
    return jnp.dot(a, b, preferred_element_type=F32)


def _dot_nt(a, b):
    return lax.dot_general(a, b, (((1,), (1,)), ((), ())), preferred_element_type=F32)


def _dot_tn(a, b):
    return lax.dot_general(a, b, (((0,), (0,)), ((), ())), preferred_element_type=F32)


def _ada_kernel(c_ref, w_ref, b_ref, o_ref):
    c = c_ref[...]
    s = c * _sigmoid(c)
    o_ref[0] = jnp.dot(s, w_ref[0], preferred_element_type=F32,
                       precision=lax.Precision.HIGHEST) + b_ref[0]


def _ada(cvec, w_ada, b_ada):
    depth = w_ada.shape[0]
    nblk = (NMOD * D) // D
    return pl.pallas_call(
        _ada_kernel,
        grid=(depth, nblk),
        in_specs=[pl.BlockSpec((8, D), lambda l, j: (0, 0)),
                  pl.BlockSpec((1, D, D), lambda l, j: (l, 0, j)),
                  pl.BlockSpec((1, 1, D), lambda l, j: (l, 0, j))],
        out_specs=pl.BlockSpec((1, 8, D), lambda l, j: (l, 0, j)),
        out_shape=jax.ShapeDtypeStruct((depth, 8, NMOD * D), F32),
        name="ada",
    )(cvec, w_ada, b_ada.reshape(depth, 1, NMOD * D))


def _swiglu_half(h, m, shift_i, gain, w1_ref, w3_ref, w2_ref):
    x = _rms(h, gain) * (1.0 + m[shift_i + 1:shift_i + 2]) + m[shift_i:shift_i + 1]
    xb = x.astype(BF16)
    a = _dot(xb, w1_ref[...])
    b = _dot(xb, w3_ref[...])
    g = (a * _sigmoid(a) * b).astype(BF16)
    y = _dot(g, w2_ref[...])
    return h + 0.5 * m[shift_i + 2:shift_i + 3] * y


def _ffn_in_kernel(h_ref, mod_ref, n1_ref, nm_ref, w1_ref, w3_ref, w2_ref, win_ref, rc_ref, rs_ref,
                   h1_ref, qt_ref, k_ref, vt_ref, hq_ref, hf_ref, hi_ref, hg_ref, zp_ref, *, rope):
    h = h_ref[0]
    m = mod_ref[0]
    h1 = _swiglu_half(h, m, 0, n1_ref[...], w1_ref, w3_ref, w2_ref)
    h1_ref[0] = h1
    u = _rms(h1, nm_ref[...]) * (1.0 + m[4:5]) + m[3:4]
    z = _dot(u.astype(BF16), win_ref[...])
    q = z[:, 0:QKW]
    k = z[:, QKW:2 * QKW]
    v = z[:, 2 * QKW:3 * QKW]
    if rope:
        lane = lax.broadcasted_iota(jnp.int32, q.shape, 1)
        first = (lane % 32) < 16
        rc = jnp.concatenate([rc_ref[...]] * (QKW // 128), axis=1)
        rs = jnp.concatenate([rs_ref[...]] * (QKW // 128), axis=1)

        def rot(x):
            partner = jnp.where(first, pltpu.roll(x, QKW - 16, 1), pltpu.roll(x, 16, 1))
            return x * rc + partner * rs

        q = rot(q)
        k = rot(k)
    q = q * (HALF ** -0.5 * LOG2E)
    qt_ref[0] = q.T.astype(BF16)
    k_ref[0] = k.astype(BF16)
    vt_ref[0] = v.T.astype(BF16)
    o = 3 * QKW
    hq_ref[0] = z[:, o:o + 256]
    hf_ref[0, 0] = z[:, o + 256:o + 512]
    hf_ref[0, 1] = z[:, o + 512:o + 768]
    hi_ref[0] = z[:, o + 768:o + 1024]
    hg_ref[0] = z[:, o + 1024:o + 1280]
    zp_ref[0] = z[:, o + 1280:o + 1536]


def _ffn_in(h, mod, n1, nm, w1, w3, w2, win, rc, rs, *, rope, tm):
    b, t, _ = h.shape
    grid = (b, t // tm)
    tok = lambda w: pl.BlockSpec((1, tm, w), lambda bi, i: (bi, i, 0))
    tokt = pl.BlockSpec((1, QKW, tm), lambda bi, i: (bi, 0, i))
    out_shape = (
        jax.ShapeDtypeStruct((b, t, D), F32),
        jax.ShapeDtypeStruct((b, QKW, t), BF16),
        jax.ShapeDtypeStruct((b, t, QKW), BF16),
        jax.ShapeDtypeStruct((b, QKW, t), BF16),
        jax.ShapeDtypeStruct((b, t, 256), F32),
        jax.ShapeDtypeStruct((b, 2, t, 256), F32),
        jax.ShapeDtypeStruct((b, t, 256), F32),
        jax.ShapeDtypeStruct((b, t, 256), F32),
        jax.ShapeDtypeStruct((b, t, 256), F32),
    )
    out_specs = (tok(D), tokt, tok(QKW), tokt, tok(256),
                 pl.BlockSpec((1, 2, tm, 256), lambda bi, i: (bi, 0, i, 0)),
                 tok(256), tok(256), tok(256))
    in_specs = [tok(D),
                pl.BlockSpec((1, NMOD, D), lambda bi, i: (bi, 0, 0)),
                _const_spec((1, D)), _const_spec((1, D)),
                _const_spec((D, DFF)), _const_spec((D, DFF)), _const_spec((DFF, D)),
                _const_spec((D, IN_A)),
                pl.BlockSpec((tm, 128), lambda bi, i: (i, 0)),
                pl.BlockSpec((tm, 128), lambda bi, i: (i, 0))]
    return pl.pallas_call(
        functools.partial(_ffn_in_kernel, rope=rope),
        grid=grid, in_specs=in_specs, out_specs=out_specs, out_shape=out_shape,
        compiler_params=pltpu.CompilerParams(
            dimension_semantics=("parallel", "parallel"), vmem_limit_bytes=VMEM_LIMIT),
        name="ffn_in",
    )(h, mod, n1, nm, w1, w3, w2, win, rc, rs)


def _attn_kernel(*refs, tq, tk, n_main, has_extra, lam_init):
    if has_extra:
        (qt_ref, k_ref, vt_ref, kx_ref, vtx_ref, lam_ref, gain_ref, o_ref,
         acc1, acc2, l_sc, kmax) = refs
    else:
        (qt_ref, k_ref, vt_ref, lam_ref, gain_ref, o_ref, acc1, acc2, l_sc, kmax) = refs
        kx_ref = vtx_ref = None
    qt = qt_ref[0]
    row = lax.broadcasted_iota(jnp.int32, qt.shape, 0)
    zero = jnp.zeros_like(qt)
    qbd = jnp.concatenate([jnp.where(row < HALF, qt, zero), jnp.where(row >= HALF, qt, zero)], axis=1)

    def kchunk(c):
        return k_ref[0, pl.ds(pl.multiple_of(c * tk, tk), tk), :]

    def vchunk(c):
        return vt_ref[0, :, pl.ds(pl.multiple_of(c * tk, tk), tk)]

    @pl.when(pl.program_id(2) == 0)
    def _():
        lane = lax.broadcasted_iota(jnp.int32, (16, VDIM), 1)
        r16 = lax.broadcasted_iota(jnp.int32, (16, VDIM), 0)
        sel = jnp.where((r16 == 0) & (lane < HALF) | (r16 == 1) & (lane >= HALF), 1.0, 0.0).astype(BF16)

        def sq_norms(kb):
            kf = kb.astype(F32)
            return _dot_nt(sel, (kf * kf).astype(BF16))

        n2 = lax.fori_loop(0, n_main, lambda c, mx: jnp.maximum(mx, sq_norms(kchunk(c))),
                           jnp.zeros((16, tk), F32))
        top = jnp.max(n2, axis=1, keepdims=True)
        if has_extra:
            top = jnp.maximum(top, jnp.max(sq_norms(kx_ref[0]), axis=1, keepdims=True))
        kmax[...] = jnp.broadcast_to(jnp.sqrt(top[0:8]), kmax.shape)

    qf = qt.astype(F32)
    q1 = jnp.sqrt(jnp.sum(jnp.where(row < HALF, qf * qf, 0.0), axis=0, keepdims=True))
    q2 = jnp.sqrt(jnp.sum(jnp.where(row >= HALF, qf * qf, 0.0), axis=0, keepdims=True))
    km = kmax[...]
    bound = jnp.concatenate([q1 * km[0:1, 0:1], q2 * km[1:2, 0:1]], axis=1) * BOUND_MARGIN
    acc1[...] = jnp.zeros_like(acc1)
    acc2[...] = jnp.zeros_like(acc2)

    def fast_step(kb, vtb, l):
        p = jnp.exp2(_dot(kb, qbd) - bound)
        pb = p.astype(BF16)
        acc1[...] += _dot(vtb, pb[:, :tq])
        acc2[...] += _dot(vtb, pb[:, tq:])
        return l + jnp.sum(p, axis=0, keepdims=True)

    l = lax.fori_loop(0, n_main - 2, lambda c, l: fast_step(kchunk(c), vchunk(c), l),
                      jnp.zeros((1, 2 * tq), F32), unroll=2)
    for c in (n_main - 2, n_main - 1):
        l = fast_step(kchunk(c), vchunk(c), l)
    if has_extra:
        l = fast_step(kx_ref[0], vtx_ref[0], l)
    l_sc[...] = jnp.broadcast_to(l, l_sc.shape)

    @pl.when(jnp.logical_not(jnp.min(l) >= L_FLOOR))
    def _():
        acc1[...] = jnp.zeros_like(acc1)
        acc2[...] = jnp.zeros_like(acc2)

        def safe_step(kb, vtb, carry):
            m, ls = carry
            s = _dot(kb, qbd)
            m_new = jnp.maximum(m, jnp.max(s, axis=0, keepdims=True))
            alpha = jnp.exp2(m - m_new)
            p = jnp.exp2(s - m_new)
            pb = p.astype(BF16)
            acc1[...] = acc1[...] * alpha[:, :tq] + _dot(vtb, pb[:, :tq])
            acc2[...] = acc2[...] * alpha[:, tq:] + _dot(vtb, pb[:, tq:])
            return m_new, alpha * ls + jnp.sum(p, axis=0, keepdims=True)

        carry = (jnp.full((1, 2 * tq), NEG, F32), jnp.zeros((1, 2 * tq), F32))
        carry = lax.fori_loop(0, n_main, lambda c, cr: safe_step(kchunk(c), vchunk(c), cr), carry)
        if has_extra:
            carry = safe_step(kx_ref[0], vtx_ref[0], carry)
        l_sc[...] = jnp.broadcast_to(carry[1], l_sc.shape)

    l = l_sc[0:1, :]

    lv = lam_ref[...]
    lam = (jnp.exp(jnp.sum(lv[0:1] * lv[1:2], axis=1, keepdims=True))
           - jnp.exp(jnp.sum(lv[2:3] * lv[3:4], axis=1, keepdims=True)) + lam_init)
    o = acc1[...] / l[:, :tq] - lam * (acc2[...] / l[:, tq:])
    o = o * lax.rsqrt(jnp.mean(o * o, axis=0, keepdims=True) + EPS) * gain_ref[...] * (1.0 - lam_init)
    o_ref[0] = o.T.astype(BF16)


def _attention(qt, k, vt, kx, vtx, lamv, gain, *, lam_init, tq, tk):
    b, _, t_q = qt.shape
    t_k = k.shape[1]
    has_extra = kx is not None
    assert t_k % (2 * tk) == 0
    grid = (b, HEADS, t_q // tq)
    in_specs = [pl.BlockSpec((1, VDIM, tq), lambda bi, hi, i: (bi, hi, i)),
                pl.BlockSpec((1, t_k, VDIM), lambda bi, hi, i: (bi, 0, hi)),
                pl.BlockSpec((1, VDIM, t_k), lambda bi, hi, i: (bi, hi, 0))]
    args = [qt, k, vt]
    if has_extra:
        t_x = kx.shape[1]
        in_specs += [pl.BlockSpec((1, t_x, VDIM), lambda bi, hi, i: (bi, 0, hi)),
                     pl.BlockSpec((1, VDIM, t_x), lambda bi, hi, i: (bi, hi, 0))]
        args += [kx, vtx]
    in_specs += [pl.BlockSpec((4, HALF), lambda bi, hi, i: (0, 0)),
                 pl.BlockSpec((VDIM, 1), lambda bi, hi, i: (0, 0))]
    args += [lamv, gain]
    return pl.pallas_call(
        functools.partial(_attn_kernel, tq=tq, tk=tk, n_main=t_k // tk, has_extra=has_extra,
                          lam_init=lam_init),
        grid=grid, in_specs=in_specs,
        out_specs=pl.BlockSpec((1, tq, VDIM), lambda bi, hi, i: (bi, i, hi)),
        out_shape=jax.ShapeDtypeStruct((b, t_q, HEADS * VDIM), BF16),
        scratch_shapes=[pltpu.VMEM((VDIM, tq), F32), pltpu.VMEM((VDIM, tq), F32),
                        pltpu.VMEM((8, 2 * tq), F32), pltpu.VMEM((8, 128), F32)],
        compiler_params=pltpu.CompilerParams(
            dimension_semantics=("parallel", "parallel", "arbitrary"), vmem_limit_bytes=VMEM_LIMIT),
        name="diff_attn",
    )(*args)


def _chunk_cumsum(x, rev):
    n = x.shape[0]
    pos = lax.broadcasted_iota(jnp.int32, x.shape, 0) % CHUNK
    sh = 1
    while sh < CHUNK:
        if rev:
            x = x + jnp.where(pos + sh < CHUNK, pltpu.roll(x, n - sh, 0), 0.0)
        else:
            x = x + jnp.where(pos >= sh, pltpu.roll(x, sh, 0), 0.0)
        sh *= 2
    return x


def _hgrn_gates(zq, zf, lbv):
    q = zq * _sigmoid(zq)
    e = jnp.exp(-jnp.abs(zf))
    r = 1.0 / (1.0 + e)
    er = e * r
    pos = zf >= 0.0
    kk = (1.0 - lbv) * jnp.where(pos, er, r)
    lf = jnp.log(jnp.maximum(lbv, LB_MIN) + (1.0 - lbv) * jnp.where(pos, r, er))
    return q, kk, lf


def _hgrn_token_scan(zq_ref, zf_ref, zi_ref, lbv, st_ref, o_ref, *, rev, tb):
    head_mask = (lax.broadcasted_iota(jnp.int32, (HGW, HGW), 0) // HGH
                 == lax.broadcasted_iota(jnp.int32, (HGW, HGW), 1) // HGH)
    first = lax.broadcasted_iota(jnp.int32, (16, HGW), 0) == 0

    def pad16(r):
        return jnp.where(first, jnp.broadcast_to(r, (16, HGW)), 0.0).astype(BF16)

    def body(i, carry):
        t = (tb - 1 - i) if rev else i
        q, kk, lf = _hgrn_gates(zq_ref[pl.ds(t, 1), :], zf_ref[pl.ds(t, 1), :], lbv)
        st = st_ref[...] * jnp.exp(lf) + jnp.where(
            head_mask, _dot_tn(pad16(zi_ref[pl.ds(t, 1), :]), pad16(kk)), 0.0)
        st_ref[...] = st
        o_ref[pl.ds(t, 1), :] = _dot_nt(pad16(q), st.astype(BF16))[0:1]
        return carry

    lax.fori_loop(0, tb, body, 0)


def _hgrn_direction(zq_ref, zf_ref, zi_ref, lbv, st_ref, o_ref, stall, dst, *, rev, tb):
    nc = tb // CHUNK
    zi = zi_ref[...]
    q, kk, lf = _hgrn_gates(zq_ref[...], zf_ref[...], lbv)
    a = _chunk_cumsum(lf, rev)
    a3 = a.reshape(nc, CHUNK, HGW)
    end_row = 0 if rev else CHUNK - 1
    mid_row = CHUNK // 2 if rev else CHUNK // 2 - 1
    a_end = a3[:, end_row:end_row + 1, :]
    a_mid = a3[:, mid_row:mid_row + 1, :]
    in_range = jnp.max(jnp.abs(a3 - a_mid)) <= PAIR_RANGE
    q3 = q.reshape(nc, CHUNK, HGW)
    k3 = kk.reshape(nc, CHUNK, HGW)
    qe = (q3 * jnp.exp(a3)).astype(BF16)
    ke = (k3 * jnp.exp(a_end - a3)).astype(BF16)
    qm = (q3 * jnp.exp(a3 - a_mid)).reshape(tb, HGW)
    km = (k3 * jnp.exp(a_mid - a3)).reshape(tb, HGW).astype(BF16)
    vb = zi.astype(BF16)
    v3 = vb.reshape(nc, CHUNK, HGW)
    dec = jnp.exp(a_end)

    lane_head = lax.broadcasted_iota(jnp.int32, (1, HGW), 1) // HGH
    sub = 128
    r_i = lax.broadcasted_iota(jnp.int32, (sub, sub), 0)
    c_i = lax.broadcasted_iota(jnp.int32, (sub, sub), 1)
    same = (r_i // CHUNK) == (c_i // CHUNK)
    causal = jnp.logical_and(same, (c_i >= r_i) if rev else (c_i <= r_i))
    causal4 = jnp.concatenate([causal] * HEADS, axis=0)
    intra = []
    for g in range(tb // sub):
        sl = slice(g * sub, (g + 1) * sub)
        qg = qm[sl]
        qstack = jnp.concatenate(
            [jnp.where(lane_head == hh, qg, 0.0) for hh in range(HEADS)], axis=0).astype(BF16)
        sc = _dot_nt(qstack, km[sl])
        sc = jnp.where(causal4, sc, 0.0).astype(BF16)
        r = _dot(sc, vb[sl])
        og = jnp.zeros((sub, HGW), F32)
        for hh in range(HEADS):
            og = og + jnp.where(lane_head == hh, r[hh * sub:(hh + 1) * sub], 0.0)
        intra.append(og)
    o_intra = jnp.concatenate(intra, axis=0)

    head_mask = (lax.broadcasted_iota(jnp.int32, (HGW, HGW), 0) // HGH
                 == lax.broadcasted_iota(jnp.int32, (HGW, HGW), 1) // HGH)
    for c in range(nc):
        dst[c] = jnp.where(head_mask, _dot_tn(v3[c], ke[c]), 0.0)
    st = st_ref[...]
    order = range(nc - 1, -1, -1) if rev else range(nc)
    for c in order:
        stall[c] = st.astype(BF16)
        st = dec[c] * st + dst[c]
    inter = [_dot_nt(qe[c], stall[c]) for c in range(nc)]
    o_ref[...] = o_intra + jnp.concatenate(inter, axis=0)

    @pl.when(in_range)
    def _():
        st_ref[...] = st

    @pl.when(jnp.logical_not(in_range))
    def _():
        _hgrn_token_scan(zq_ref, zf_ref, zi_ref, lbv, st_ref, o_ref, rev=rev, tb=tb)


def _hgrn_kernel(zqf_ref, zff_ref, zif_ref, zqb_ref, zfb_ref, zib_ref, lb_ref, s0_ref,
                 of_ref, ob_ref, s_ref, stall_f, dst_f, stall_b, dst_b, *, tb, layer):
    @pl.when(pl.program_id(1) == 0)
    def _():
        s_ref[...] = s0_ref[...]

    logits = lb_ref[...]
    e = jnp.exp(logits - jnp.max(logits, axis=0, keepdims=True))
    p = e / jnp.sum(e, axis=0, keepdims=True)
    lb = jnp.sum(p[0:layer + 1], axis=0) - p[0]
    _hgrn_direction(zqf_ref.at[0], zff_ref.at[0, 0], zif_ref.at[0], lb[0:1], s_ref.at[0, 0], of_ref.at[0],
                    stall_f, dst_f, rev=False, tb=tb)
    _hgrn_direction(zqb_ref.at[0], zfb_ref.at[0, 0], zib_ref.at[0], lb[1:2], s_ref.at[0, 1], ob_ref.at[0],
                    stall_b, dst_b, rev=True, tb=tb)


def _hgrn(zq, zf, zi, lb_logits, s0, *, layer, tb):
    b, t, _ = zq.shape
    nb = t // tb
    fwd = lambda bi, i: (bi, i, 0)
    bwd = lambda bi, i: (bi, nb - 1 - i, 0)
    in_specs = [pl.BlockSpec((1, tb, HGW), fwd),
                pl.BlockSpec((1, 1, tb, HGW), lambda bi, i: (bi, 0, i, 0)),
                pl.BlockSpec((1, tb, HGW), fwd),
                pl.BlockSpec((1, tb, HGW), bwd),
                pl.BlockSpec((1, 1, tb, HGW), lambda bi, i: (bi, 1, nb - 1 - i, 0)),
                pl.BlockSpec((1, tb, HGW), bwd),
                pl.BlockSpec(lb_logits.shape, lambda bi, i: (0, 0, 0)),
                pl.BlockSpec((1, 2, HGW, HGW), lambda bi, i: (bi, 0, 0, 0))]
    out_specs = (pl.BlockSpec((1, tb, HGW), fwd),
                 pl.BlockSpec((1, tb, HGW), bwd),
                 pl.BlockSpec((1, 2, HGW, HGW), lambda bi, i: (bi, 0, 0, 0)))
    out_shape = (jax.ShapeDtypeStruct((b, t, HGW), F32),
                 jax.ShapeDtypeStruct((b, t, HGW), F32),
                 jax.ShapeDtypeStruct((b, 2, HGW, HGW), F32))
    nc = tb // CHUNK
    of, ob, s_fin = pl.pallas_call(
        functools.partial(_hgrn_kernel, tb=tb, layer=layer),
        grid=(b, nb), in_specs=in_specs, out_specs=out_specs, out_shape=out_shape,
        scratch_shapes=[pltpu.VMEM((nc, HGW, HGW), BF16), pltpu.VMEM((nc, HGW, HGW), F32),
                        pltpu.VMEM((nc, HGW, HGW), BF16), pltpu.VMEM((nc, HGW, HGW), F32)],
        compiler_params=pltpu.CompilerParams(
            dimension_semantics=("parallel", "arbitrary"), vmem_limit_bytes=VMEM_LIMIT),
        name="hgrn",
    )(zq, zf, zi, zq, zf, zi, lb_logits, s0)
    return of, ob, s_fin


def _merge_kernel(h_ref, mod_ref, nm_ref, n2_ref, oda_ref, of_ref, ob_ref, hg_ref,
                  zp_ref, zpp_ref, zpn_ref, hgn_ref, pw_ref, ps_ref,
                  wg_ref, wpa_ref, wph_ref, wpp_ref, wo_ref, w1_ref, w3_ref, w2_ref, fn_ref,
                  out_ref, *, tm, t_total, final):
    i = pl.program_id(1)
    h1 = h_ref[0]
    m = mod_ref[0]

    x = zp_ref[0]
    prev = jnp.where(i > 0, zpp_ref[0], 0.0)
    nxt = jnp.where(i < pl.num_programs(1) - 1, zpn_ref[0], 0.0)
    e = jnp.concatenate([prev, x, nxt], axis=0)
    n = tm + 16
    a2 = e[0:n - 1] + e[1:n]
    a4 = a2[0:n - 3] + a2[2:n - 1]
    a8 = a4[0:n - 7] + a4[4:n - 3]
    a16 = a8[0:n - 15] + a8[8:n - 7]
    sums = (a2[7:7 + tm], a4[6:6 + tm], a8[4:4 + tm], a16[0:tm])
    group = lax.broadcasted_iota(jnp.int32, (1, PW), 1) // 64
    total, half = sums[-1], jnp.full((1, PW), POOL_WINDOWS[-1] // 2, jnp.int32)
    for g in range(len(POOL_WINDOWS) - 2, -1, -1):
        total = jnp.where(group == g, sums[g], total)
        half = jnp.where(group == g, POOL_WINDOWS[g] // 2, half)
    mean = total * (0.5 / half.astype(F32))

    def clipped(rows, r0):
        pos = i * tm + r0 + lax.broadcasted_iota(jnp.int32, (8, PW), 0)
        cnt = jnp.minimum(pos + half, t_total) - jnp.maximum(pos - half, 0)
        return rows / cnt.astype(F32)

    mean = jnp.concatenate([clipped(total[0:8], 0), mean[8:tm - 8], clipped(total[tm - 8:tm], tm - 8)], axis=0)
    o_pool = _dot((mean - x).astype(BF16), pw_ref[...]) * ps_ref[...]

    o = of_ref[0] + ob_ref[0]
    o2 = o * o
    hi = o2.astype(BF16)
    lo = (o2 - hi.astype(F32)).astype(BF16)
    ones_bd = (lax.broadcasted_iota(jnp.int32, (HGW, HGW), 0) // HGH
               == lax.broadcasted_iota(jnp.int32, (HGW, HGW), 1) // HGH).astype(BF16)
    seg = _dot(hi, ones_bd) + _dot(lo, ones_bd)
    zg = hg_ref[0]
    o_hg = (o * lax.rsqrt(seg * (1.0 / HGH) + EPS) * hgn_ref[...]) * (zg * _sigmoid(zg))

    u = _rms(h1, nm_ref[...]) * (1.0 + m[4:5]) + m[3:4]
    gate = _sigmoid(_dot(u.astype(BF16), wg_ref[...]))
    y = (gate[:, 0:D] * _dot(oda_ref[0], wpa_ref[...])
         + gate[:, D:2 * D] * _dot(o_hg.astype(BF16), wph_ref[...])
         + gate[:, 2 * D:3 * D] * _dot(o_pool.astype(BF16), wpp_ref[...]))
    mix = _dot(y.astype(BF16), wo_ref[...])
    h2 = h1 + m[5:6] * mix
    h3 = _swiglu_half(h2, m, 6, n2_ref[...], w1_ref, w3_ref, w2_ref)
    if final:
        h3 = _rms(h3, fn_ref[...])
    out_ref[0] = h3


def _merge(h1, mod, nm, n2, oda, of, ob, hg, zp, hgn, pw, ps, wg, wpa, wph, wpp, wo, w1, w3, w2, fn,
           *, tm, final):
    b, t, _ = h1.shape
    nblk8 = t // 8
    r = tm // 8
    tok = lambda w: pl.BlockSpec((1, tm, w), lambda bi, i: (bi, i, 0))
    in_specs = [tok(D),
                pl.BlockSpec((1, NMOD, D), lambda bi, i: (bi, 0, 0)),
                _const_spec((1, D)), _const_spec((1, D)),
                tok(QKW),
                tok(HGW), tok(HGW),
                tok(HGW), tok(PW),
                pl.BlockSpec((1, 8, PW), lambda bi, i: (bi, jnp.maximum(i * r - 1, 0), 0)),
                pl.BlockSpec((1, 8, PW), lambda bi, i: (bi, jnp.minimum((i + 1) * r, nblk8 - 1), 0)),
                _const_spec((1, HGW)), _const_spec((PW, PW)), _const_spec((1, PW)),
                _const_spec((D, 3 * D)), _const_spec((QKW, D)), _const_spec((HGW, D)),
                _const_spec((PW, D)), _const_spec((D, D)),
                _const_spec((D, DFF)), _const_spec((D, DFF)), _const_spec((DFF, D)),
                _const_spec((1, D))]
    return pl.pallas_call(
        functools.partial(_merge_kernel, tm=tm, t_total=t, final=final),
        grid=(b, t // tm), in_specs=in_specs, out_specs=tok(D),
        out_shape=jax.ShapeDtypeStruct((b, t, D), F32),
        compiler_params=pltpu.CompilerParams(
            dimension_semantics=("parallel", "parallel"), vmem_limit_bytes=VMEM_LIMIT),
        name="merge_ffn",
    )(h1, mod, nm, n2, oda, of, ob, hg, zp, zp, zp, hgn, pw, ps, wg, wpa, wph, wpp, wo, w1, w3, w2, fn)


def _rope_tables(n):
    rows = n // GRID_WIDTH
    row = jnp.repeat(jnp.arange(rows, dtype=jnp.int32), GRID_WIDTH).astype(F32)
    col = jnp.tile(jnp.arange(GRID_WIDTH, dtype=jnp.int32), rows).astype(F32)
    axis_dim = HALF // 2
    inv_freq = THETA ** (-jnp.arange(0, axis_dim, 2, dtype=F32) / axis_dim)
    ang_r = row[:, None] * inv_freq[None, :]
    ang_c = col[:, None] * inv_freq[None, :]
    cos64 = jnp.concatenate([jnp.cos(ang_r)] * 2 + [jnp.cos(ang_c)] * 2, axis=1)
    sin64 = jnp.concatenate([-jnp.sin(ang_r), jnp.sin(ang_r), -jnp.sin(ang_c), jnp.sin(ang_c)], axis=1)
    return jnp.tile(cos64, (1, 2)), jnp.tile(sin64, (1, 2))


def _block_diag(w):
    g, a, b = w.shape
    out = jnp.zeros((g * a, g * b), w.dtype)
    for i in range(g):
        out = out.at[i * a:(i + 1) * a, i * b:(i + 1) * b].set(w[i])
    return out


def kernel(x, c, ctx, c_ctx, w_ada, b_ada, norm_ffn1, norm_mix, norm_ffn2, ffn1_w1, ffn1_w3, ffn1_w2,
           ffn2_w1, ffn2_w3, ffn2_w2, w_in, da_lambda_q1, da_lambda_k1, da_lambda_q2, da_lambda_k2,
           da_subln, hg_lb_logits, hg_norm, pool_w, pool_scale, w_proj_da, w_proj_hg, w_proj_pool,
           w_out, final_norm):
    bsz, n, _ = x.shape
    n_ctx = ctx.shape[1]
    depth = w_ada.shape[0]
    lat, cx = _tiles(n), _tiles(n_ctx)

    cvec = jnp.zeros((8, D), F32).at[0:bsz].set(c).at[bsz].set(c_ctx)
    mods = _ada(cvec, w_ada, b_ada).reshape(depth, 8, NMOD, D)
    rc, rs = _rope_tables(n)
    rc_c = jnp.zeros((n_ctx, 128), F32)
    row = lambda v: v.reshape(1, -1)
    bf = lambda w: w.astype(BF16)

    h, hc = x, ctx
    for l in range(depth):
        need_ctx = l < depth - 1
        lam_init = 0.8 - 0.6 * math.exp(-0.3 * l)
        m_lat = mods[l, 0:bsz]
        m_ctx = jnp.broadcast_to(mods[l, bsz][None], (bsz, NMOD, D))
        w1a, w3a, w2a = bf(ffn1_w1[l]), bf(ffn1_w3[l]), bf(ffn1_w2[l])
        w1b, w3b, w2b = bf(ffn2_w1[l]), bf(ffn2_w3[l]), bf(ffn2_w2[l])
        win_a = bf(w_in[l][:, :IN_A])
        w_gate = bf(w_in[l][:, IN_A:])
        lamv = jnp.stack([da_lambda_q1[l], da_lambda_k1[l], da_lambda_q2[l], da_lambda_k2[l]]).astype(F32)
        gain_da = da_subln[l].reshape(VDIM, 1)
        hgn = row(jnp.tile(hg_norm[l], HEADS))
        pw_bd = bf(_block_diag(pool_w[l]))
        merge_w = (hgn, pw_bd, row(pool_scale[l]), w_gate, bf(w_proj_da[l]), bf(w_proj_hg[l]),
                   bf(w_proj_pool[l]), bf(w_out[l]), w1b, w3b, w2b, row(final_norm))

        (hc1, qt_c, k_c, vt_c, hq_c, hf_c, hi_c, hg_c, zp_c) = _ffn_in(
            hc, m_ctx, row(norm_ffn1[l]), row(norm_mix[l]), w1a, w3a, w2a, win_a, rc_c, rc_c,
            rope=False, tm=cx.tm)
        s0 = jnp.zeros((bsz, 2, HGW, HGW), F32)
        of_c, ob_c, s_ctx = _hgrn(hq_c, hf_c, hi_c, hg_lb_logits, s0, layer=l, tb=cx.tb)

        (h1, qt, k, vt, hq, hf, hi, hg, zp) = _ffn_in(
            h, m_lat, row(norm_ffn1[l]), row(norm_mix[l]), w1a, w3a, w2a, win_a, rc, rs,
            rope=True, tm=lat.tm)
        o_da = _attention(qt, k, vt, k_c, vt_c, lamv, gain_da, lam_init=lam_init, tq=lat.tq, tk=lat.tk)
        of, ob, _ = _hgrn(hq, hf, hi, hg_lb_logits, s_ctx, layer=l, tb=lat.tb)
        h = _merge(h1, m_lat, row(norm_mix[l]), row(norm_ffn2[l]), o_da, of, ob, hg, zp, *merge_w,
                   tm=lat.tm, final=not need_ctx)
        if need_ctx:
            o_dac = _attention(qt_c, k_c, vt_c, None, None, lamv, gain_da, lam_init=lam_init,
                               tq=cx.tq, tk=cx.tk)
            hc = _merge(hc1, m_ctx, row(norm_mix[l]), row(norm_ffn2[l]), o_dac, of_c, ob_c, hg_c, zp_c,
                        *merge_w, tm=cx.tm, final=False)
    return h
```

```python
import functools
import math
from typing import NamedTuple

import jax
import jax.numpy as jnp
from jax import lax
from jax.experimental import pallas as pl
from jax.experimental.pallas import tpu as pltpu

D = 1024
GRID_WIDTH = 64
EPS = 1e-6
LB_MIN = 1e-20
NMOD = 9
HEADS = 4
HALF = 64
VDIM = 128
QKW = HEADS * 2 * HALF
HGW = 256
HGH = 64
PW = 256
POOL_WINDOWS = (2, 4, 8, 16)
DFF = 2816
THETA = 10000.0
IN_A = 3 * QKW + 6 * 256
CHUNK = 64
PAIR_RANGE = 80.0
NEG = -1e30
LOG2E = 1.4426950408889634
BOUND_MARGIN = 1.01
L_FLOOR = 2.0 ** -86

F32 = jnp.float32
BF16 = jnp.bfloat16
VMEM_LIMIT = 56 * 1024 * 1024

TOKEN_TILE = 512
QUERY_TILE = 512
KEY_CHUNK = 2048
SCAN_BLOCK = 512


class _Tiles(NamedTuple):
    tm: int
    tq: int
    tk: int
    tb: int


def _tiles(n):
    return _Tiles(min(TOKEN_TILE, n), min(QUERY_TILE, n), min(KEY_CHUNK, n), min(SCAN_BLOCK, n))


def _const_spec(shape):
    nd = len(shape)
    return pl.BlockSpec(shape, lambda *_: (0,) * nd, pipeline_mode=pl.Buffered(1))


def _rms(x, gain):
    return x * lax.rsqrt(jnp.mean(x * x, axis=-1, keepdims=True) + EPS) * gain


def _sigmoid(x):
    return 1.0 / (1.0 + jnp.exp(-x))


def _dot(a, b):
    return jnp.dot(a, b, preferred_element_type=F32)


def _dot_nt(a, b):
    return lax.dot_general(a, b, (((1,), (1,)), ((), ())), preferred_element_type=F32)


def _dot_tn(a, b):
    return lax.dot_general(a, b, (((0,), (0,)), ((), ())), preferred_element_type=F32)


def _ada_kernel(c_ref, w_ref, b_ref, o_ref):
    c = c_ref[...]
    s = c * _sigmoid(c)
    o_ref[0] = jnp.dot(s, w_ref[0], preferred_element_type=F32,
                       precision=lax.Precision.HIGHEST) + b_ref[0]


def _ada(cvec, w_ada, b_ada):
    depth = w_ada.shape[0]
    nblk = (NMOD * D) // D
    return pl.pallas_call(
        _ada_kernel,
        grid=(depth, nblk),
        in_specs=[pl.BlockSpec((8, D), lambda l, j: (0, 0)),
                  pl.BlockSpec((1, D, D), lambda l, j: (l, 0, j)),
                  pl.BlockSpec((1, 1, D), lambda l, j: (l, 0, j))],
        out_specs=pl.BlockSpec((1, 8, D), lambda l, j: (l, 0, j)),
        out_shape=jax.ShapeDtypeStruct((depth, 8, NMOD * D), F32),
        name="ada",
    )(cvec, w_ada, b_ada.reshape(depth, 1, NMOD * D))


def _swiglu_half(h, m, shift_i, gain, w1_ref, w3_ref, w2_ref):
    x = _rms(h, gain) * (1.0 + m[shift_i + 1:shift_i + 2]) + m[shift_i:shift_i + 1]
    xb = x.astype(BF16)
    a = _dot(xb, w1_ref[...])
    b = _dot(xb, w3_ref[...])
    g = (a * _sigmoid(a) * b).astype(BF16)
    y = _dot(g, w2_ref[...])
    return h + 0.5 * m[shift_i + 2:shift_i + 3] * y


def _ffn_in_kernel(h_ref, mod_ref, n1_ref, nm_ref, w1_ref, w3_ref, w2_ref, win_ref, rc_ref, rs_ref,
                   h1_ref, u_ref, qt_ref, k_ref, vt_ref, hq_ref, hf_ref, hi_ref, hg_ref, zp_ref, *, rope):
    h = h_ref[0]
    m = mod_ref[0]
    h1 = _swiglu_half(h, m, 0, n1_ref[...], w1_ref, w3_ref, w2_ref)
    h1_ref[0] = h1
    u = (_rms(h1, nm_ref[...]) * (1.0 + m[4:5]) + m[3:4]).astype(BF16)
    u_ref[0] = u
    z = _dot(u, win_ref[...])
    q = z[:, 0:QKW]
    k = z[:, QKW:2 * QKW]
    v = z[:, 2 * QKW:3 * QKW]
    if rope:
        lane = lax.broadcasted_iota(jnp.int32, q.shape, 1)
        first = (lane % 32) < 16
        rc = jnp.concatenate([rc_ref[...]] * (QKW // 128), axis=1)
        rs = jnp.concatenate([rs_ref[...]] * (QKW // 128), axis=1)

        def rot(x):
            partner = jnp.where(first, pltpu.roll(x, QKW - 16, 1), pltpu.roll(x, 16, 1))
            return x * rc + partner * rs

        q = rot(q)
        k = rot(k)
    q = q * (HALF ** -0.5 * LOG2E)
    qt_ref[0] = q.T.astype(BF16)
    k_ref[0] = k.astype(BF16)
    vt_ref[0] = v.T.astype(BF16)
    o = 3 * QKW
    hq_ref[0] = z[:, o:o + 256]
    hf_ref[0, 0] = z[:, o + 256:o + 512]
    hf_ref[0, 1] = z[:, o + 512:o + 768]
    hi_ref[0] = z[:, o + 768:o + 1024]
    hg_ref[0] = z[:, o + 1024:o + 1280]
    zp_ref[0] = z[:, o + 1280:o + 1536]


def _ffn_in(h, mod, n1, nm, w1, w3, w2, win, rc, rs, *, rope, tm):
    b, t, _ = h.shape
    grid = (b, t // tm)
    tok = lambda w: pl.BlockSpec((1, tm, w), lambda bi, i: (bi, i, 0))
    tokt = pl.BlockSpec((1, QKW, tm), lambda bi, i: (bi, 0, i))
    out_shape = (
        jax.ShapeDtypeStruct((b, t, D), F32),
        jax.ShapeDtypeStruct((b, t, D), BF16),
        jax.ShapeDtypeStruct((b, QKW, t), BF16),
        jax.ShapeDtypeStruct((b, t, QKW), BF16),
        jax.ShapeDtypeStruct((b, QKW, t), BF16),
        jax.ShapeDtypeStruct((b, t, 256), F32),
        jax.ShapeDtypeStruct((b, 2, t, 256), F32),
        jax.ShapeDtypeStruct((b, t, 256), F32),
        jax.ShapeDtypeStruct((b, t, 256), F32),
        jax.ShapeDtypeStruct((b, t, 256), F32),
    )
    out_specs = (tok(D), tok(D), tokt, tok(QKW), tokt, tok(256),
                 pl.BlockSpec((1, 2, tm, 256), lambda bi, i: (bi, 0, i, 0)),
                 tok(256), tok(256), tok(256))
    in_specs = [tok(D),
                pl.BlockSpec((1, NMOD, D), lambda bi, i: (bi, 0, 0)),
                _const_spec((1, D)), _const_spec((1, D)),
                _const_spec((D, DFF)), _const_spec((D, DFF)), _const_spec((DFF, D)),
                _const_spec((D, IN_A)),
                pl.BlockSpec((tm, 128), lambda bi, i: (i, 0)),
                pl.BlockSpec((tm, 128), lambda bi, i: (i, 0))]
    return pl.pallas_call(
        functools.partial(_ffn_in_kernel, rope=rope),
        grid=grid, in_specs=in_specs, out_specs=out_specs, out_shape=out_shape,
        compiler_params=pltpu.CompilerParams(
            dimension_semantics=("parallel", "parallel"), vmem_limit_bytes=VMEM_LIMIT),
        name="ffn_in",
    )(h, mod, n1, nm, w1, w3, w2, win, rc, rs)


def _attn_kernel(*refs, tq, tk, n_main, has_extra, lam_init):
    if has_extra:
        (qt_ref, k_ref, vt_ref, kx_ref, vtx_ref, lam_ref, gain_ref, o_ref,
         acc1, acc2, l_sc, kmax) = refs
    else:
        (qt_ref, k_ref, vt_ref, lam_ref, gain_ref, o_ref, acc1, acc2, l_sc, kmax) = refs
        kx_ref = vtx_ref = None
    qt = qt_ref[0]
    row = lax.broadcasted_iota(jnp.int32, qt.shape, 0)
    zero = jnp.zeros_like(qt)
    qbd = jnp.concatenate([jnp.where(row < HALF, qt, zero), jnp.where(row >= HALF, qt, zero)], axis=1)

    def rows(c):
        return pl.ds(c * tk if isinstance(c, int) else pl.multiple_of(c * tk, tk), tk)

    def kchunk(c):
        return k_ref[0, rows(c), :]

    def vchunk(c):
        return vt_ref[0, :, rows(c)]

    @pl.when(pl.program_id(2) == 0)
    def _():
        lane = lax.broadcasted_iota(jnp.int32, (16, VDIM), 1)
        r16 = lax.broadcasted_iota(jnp.int32, (16, VDIM), 0)
        sel = jnp.where((r16 == 0) & (lane < HALF) | (r16 == 1) & (lane >= HALF), 1.0, 0.0).astype(BF16)

        def sq_norms(kb):
            kf = kb.astype(F32)
            return _dot_nt(sel, (kf * kf).astype(BF16))

        n2 = lax.fori_loop(0, n_main, lambda c, mx: jnp.maximum(mx, sq_norms(kchunk(c))),
                           jnp.zeros((16, tk), F32))
        top = jnp.max(n2, axis=1, keepdims=True)
        if has_extra:
            top = jnp.maximum(top, jnp.max(sq_norms(kx_ref[0]), axis=1, keepdims=True))
        kmax[...] = jnp.broadcast_to(jnp.sqrt(top[0:8]), kmax.shape)

    qf = qt.astype(F32)
    q1 = jnp.sqrt(jnp.sum(jnp.where(row < HALF, qf * qf, 0.0), axis=0, keepdims=True))
    q2 = jnp.sqrt(jnp.sum(jnp.where(row >= HALF, qf * qf, 0.0), axis=0, keepdims=True))
    km = kmax[...]
    bound = jnp.concatenate([q1 * km[0:1, 0:1], q2 * km[1:2, 0:1]], axis=1) * BOUND_MARGIN
    acc1[...] = jnp.zeros_like(acc1)
    acc2[...] = jnp.zeros_like(acc2)

    def fast_step(kb, vtb, l):
        p = jnp.exp2(_dot(kb, qbd) - bound)
        pb = p.astype(BF16)
        acc1[...] += _dot(vtb, pb[:, :tq])
        acc2[...] += _dot(vtb, pb[:, tq:])
        return l + jnp.sum(p, axis=0, keepdims=True)

    l = jnp.zeros((1, 2 * tq), F32)
    for c in range(n_main):
        l = fast_step(kchunk(c), vchunk(c), l)
    if has_extra:
        l = fast_step(kx_ref[0], vtx_ref[0], l)
    l_sc[...] = jnp.broadcast_to(l, l_sc.shape)

    @pl.when(jnp.logical_not(jnp.min(l) >= L_FLOOR))
    def _():
        acc1[...] = jnp.zeros_like(acc1)
        acc2[...] = jnp.zeros_like(acc2)

        def safe_step(kb, vtb, carry):
            m, ls = carry
            s = _dot(kb, qbd)
            m_new = jnp.maximum(m, jnp.max(s, axis=0, keepdims=True))
            alpha = jnp.exp2(m - m_new)
            p = jnp.exp2(s - m_new)
            pb = p.astype(BF16)
            acc1[...] = acc1[...] * alpha[:, :tq] + _dot(vtb, pb[:, :tq])
            acc2[...] = acc2[...] * alpha[:, tq:] + _dot(vtb, pb[:, tq:])
            return m_new, alpha * ls + jnp.sum(p, axis=0, keepdims=True)

        carry = (jnp.full((1, 2 * tq), NEG, F32), jnp.zeros((1, 2 * tq), F32))
        carry = lax.fori_loop(0, n_main, lambda c, cr: safe_step(kchunk(c), vchunk(c), cr), carry)
        if has_extra:
            carry = safe_step(kx_ref[0], vtx_ref[0], carry)
        l_sc[...] = jnp.broadcast_to(carry[1], l_sc.shape)

    l = l_sc[0:1, :]

    lv = lam_ref[...]
    lam = (jnp.exp(jnp.sum(lv[0:1] * lv[1:2], axis=1, keepdims=True))
           - jnp.exp(jnp.sum(lv[2:3] * lv[3:4], axis=1, keepdims=True)) + lam_init)
    o = acc1[...] / l[:, :tq] - lam * (acc2[...] / l[:, tq:])
    o = o * lax.rsqrt(jnp.mean(o * o, axis=0, keepdims=True) + EPS) * gain_ref[...] * (1.0 - lam_init)
    o_ref[0] = o.T.astype(BF16)


def _attention(qt, k, vt, kx, vtx, lamv, gain, *, lam_init, tq, tk):
    b, _, t_q = qt.shape
    t_k = k.shape[1]
    has_extra = kx is not None
    assert t_k % tk == 0
    grid = (b, HEADS, t_q // tq)
    in_specs = [pl.BlockSpec((1, VDIM, tq), lambda bi, hi, i: (bi, hi, i)),
                pl.BlockSpec((1, t_k, VDIM), lambda bi, hi, i: (bi, 0, hi)),
                pl.BlockSpec((1, VDIM, t_k), lambda bi, hi, i: (bi, hi, 0))]
    args = [qt, k, vt]
    if has_extra:
        t_x = kx.shape[1]
        in_specs += [pl.BlockSpec((1, t_x, VDIM), lambda bi, hi, i: (bi, 0, hi)),
                     pl.BlockSpec((1, VDIM, t_x), lambda bi, hi, i: (bi, hi, 0))]
        args += [kx, vtx]
    in_specs += [pl.BlockSpec((4, HALF), lambda bi, hi, i: (0, 0)),
                 pl.BlockSpec((VDIM, 1), lambda bi, hi, i: (0, 0))]
    args += [lamv, gain]
    return pl.pallas_call(
        functools.partial(_attn_kernel, tq=tq, tk=tk, n_main=t_k // tk, has_extra=has_extra,
                          lam_init=lam_init),
        grid=grid, in_specs=in_specs,
        out_specs=pl.BlockSpec((1, tq, VDIM), lambda bi, hi, i: (bi, i, hi)),
        out_shape=jax.ShapeDtypeStruct((b, t_q, HEADS * VDIM), BF16),
        scratch_shapes=[pltpu.VMEM((VDIM, tq), F32), pltpu.VMEM((VDIM, tq), F32),
                        pltpu.VMEM((8, 2 * tq), F32), pltpu.VMEM((8, 128), F32)],
        compiler_params=pltpu.CompilerParams(
            dimension_semantics=("parallel", "parallel", "arbitrary"), vmem_limit_bytes=VMEM_LIMIT),
        name="diff_attn",
    )(*args)


def _chunk_cumsum(x, rev):
    n = x.shape[0]
    pos = lax.broadcasted_iota(jnp.int32, x.shape, 0) % CHUNK
    sh = 1
    while sh < CHUNK:
        if rev:
            x = x + jnp.where(pos + sh < CHUNK, pltpu.roll(x, n - sh, 0), 0.0)
        else:
            x = x + jnp.where(pos >= sh, pltpu.roll(x, sh, 0), 0.0)
        sh *= 2
    return x


def _hgrn_gates(zq, zf, lbv):
    q = zq * _sigmoid(zq)
    e = jnp.exp(-jnp.abs(zf))
    r = 1.0 / (1.0 + e)
    er = e * r
    pos = zf >= 0.0
    kk = (1.0 - lbv) * jnp.where(pos, er, r)
    lf = jnp.log(jnp.maximum(lbv, LB_MIN) + (1.0 - lbv) * jnp.where(pos, r, er))
    return q, kk, lf


def _hgrn_token_scan(zq_ref, zf_ref, zi_ref, lbv, st_ref, o_ref, *, rev, tb):
    head_mask = (lax.broadcasted_iota(jnp.int32, (HGW, HGW), 0) // HGH
                 == lax.broadcasted_iota(jnp.int32, (HGW, HGW), 1) // HGH)
    first = lax.broadcasted_iota(jnp.int32, (16, HGW), 0) == 0

    def pad16(r):
        return jnp.where(first, jnp.broadcast_to(r, (16, HGW)), 0.0).astype(BF16)

    def body(i, carry):
        t = (tb - 1 - i) if rev else i
        q, kk, lf = _hgrn_gates(zq_ref[pl.ds(t, 1), :], zf_ref[pl.ds(t, 1), :], lbv)
        st = st_ref[...] * jnp.exp(lf) + jnp.where(
            head_mask, _dot_tn(pad16(zi_ref[pl.ds(t, 1), :]), pad16(kk)), 0.0)
        st_ref[...] = st
        o_ref[pl.ds(t, 1), :] = _dot_nt(pad16(q), st.astype(BF16))[0:1]
        return carry

    lax.fori_loop(0, tb, body, 0)


def _hgrn_direction(zq_ref, zf_ref, zi_ref, lbv, st_ref, o_ref, stall, dst, *, rev, tb):
    nc = tb // CHUNK
    zi = zi_ref[...]
    q, kk, lf = _hgrn_gates(zq_ref[...], zf_ref[...], lbv)
    a = _chunk_cumsum(lf, rev)
    a3 = a.reshape(nc, CHUNK, HGW)
    end_row = 0 if rev else CHUNK - 1
    mid_row = CHUNK // 2 if rev else CHUNK // 2 - 1
    a_end = a3[:, end_row:end_row + 1, :]
    a_mid = a3[:, mid_row:mid_row + 1, :]
    in_range = jnp.max(jnp.abs(a3 - a_mid)) <= PAIR_RANGE
    q3 = q.reshape(nc, CHUNK, HGW)
    k3 = kk.reshape(nc, CHUNK, HGW)
    qe = (q3 * jnp.exp(a3)).astype(BF16)
    ke = (k3 * jnp.exp(a_end - a3)).astype(BF16)
    qm = (q3 * jnp.exp(a3 - a_mid)).reshape(tb, HGW)
    km = (k3 * jnp.exp(a_mid - a3)).reshape(tb, HGW).astype(BF16)
    vb = zi.astype(BF16)
    v3 = vb.reshape(nc, CHUNK, HGW)
    dec = jnp.exp(a_end)

    lane_head = lax.broadcasted_iota(jnp.int32, (1, HGW), 1) // HGH
    sub = 128
    r_i = lax.broadcasted_iota(jnp.int32, (sub, sub), 0)
    c_i = lax.broadcasted_iota(jnp.int32, (sub, sub), 1)
    same = (r_i // CHUNK) == (c_i // CHUNK)
    causal = jnp.logical_and(same, (c_i >= r_i) if rev else (c_i <= r_i))
    causal4 = jnp.concatenate([causal] * HEADS, axis=0)
    intra = []
    for g in range(tb // sub):
        sl = slice(g * sub, (g + 1) * sub)
        qg = qm[sl]
        qstack = jnp.concatenate(
            [jnp.where(lane_head == hh, qg, 0.0) for hh in range(HEADS)], axis=0).astype(BF16)
        sc = _dot_nt(qstack, km[sl])
        sc = jnp.where(causal4, sc, 0.0).astype(BF16)
        r = _dot(sc, vb[sl])
        og = jnp.zeros((sub, HGW), F32)
        for hh in range(HEADS):
            og = og + jnp.where(lane_head == hh, r[hh * sub:(hh + 1) * sub], 0.0)
        intra.append(og)
    o_intra = jnp.concatenate(intra, axis=0)

    head_mask = (lax.broadcasted_iota(jnp.int32, (HGW, HGW), 0) // HGH
                 == lax.broadcasted_iota(jnp.int32, (HGW, HGW), 1) // HGH)
    for c in range(nc):
        dst[c] = jnp.where(head_mask, _dot_tn(v3[c], ke[c]), 0.0)
    st = st_ref[...]
    order = range(nc - 1, -1, -1) if rev else range(nc)
    for c in order:
        stall[c] = st.astype(BF16)
        st = dec[c] * st + dst[c]
    inter = [_dot_nt(qe[c], stall[c]) for c in range(nc)]
    o_ref[...] = o_intra + jnp.concatenate(inter, axis=0)

    @pl.when(in_range)
    def _():
        st_ref[...] = st

    @pl.when(jnp.logical_not(in_range))
    def _():
        _hgrn_token_scan(zq_ref, zf_ref, zi_ref, lbv, st_ref, o_ref, rev=rev, tb=tb)


def _hgrn_kernel(zqf_ref, zff_ref, zif_ref, zqb_ref, zfb_ref, zib_ref, lb_ref, s0_ref,
                 of_ref, ob_ref, s_ref, stall_f, dst_f, stall_b, dst_b, *, tb, layer):
    @pl.when(pl.program_id(1) == 0)
    def _():
        s_ref[...] = s0_ref[...]

    logits = lb_ref[...]
    e = jnp.exp(logits - jnp.max(logits, axis=0, keepdims=True))
    p = e / jnp.sum(e, axis=0, keepdims=True)
    lb = jnp.sum(p[0:layer + 1], axis=0) - p[0]
    _hgrn_direction(zqf_ref.at[0], zff_ref.at[0, 0], zif_ref.at[0], lb[0:1], s_ref.at[0, 0], of_ref.at[0],
                    stall_f, dst_f, rev=False, tb=tb)
    _hgrn_direction(zqb_ref.at[0], zfb_ref.at[0, 0], zib_ref.at[0], lb[1:2], s_ref.at[0, 1], ob_ref.at[0],
                    stall_b, dst_b, rev=True, tb=tb)


def _hgrn(zq, zf, zi, lb_logits, s0, *, layer, tb):
    b, t, _ = zq.shape
    nb = t // tb
    fwd = lambda bi, i: (bi, i, 0)
    bwd = lambda bi, i: (bi, nb - 1 - i, 0)
    in_specs = [pl.BlockSpec((1, tb, HGW), fwd),
                pl.BlockSpec((1, 1, tb, HGW), lambda bi, i: (bi, 0, i, 0)),
                pl.BlockSpec((1, tb, HGW), fwd),
                pl.BlockSpec((1, tb, HGW), bwd),
                pl.BlockSpec((1, 1, tb, HGW), lambda bi, i: (bi, 1, nb - 1 - i, 0)),
                pl.BlockSpec((1, tb, HGW), bwd),
                pl.BlockSpec(lb_logits.shape, lambda bi, i: (0, 0, 0)),
                pl.BlockSpec((1, 2, HGW, HGW), lambda bi, i: (bi, 0, 0, 0))]
    out_specs = (pl.BlockSpec((1, tb, HGW), fwd),
                 pl.BlockSpec((1, tb, HGW), bwd),
                 pl.BlockSpec((1, 2, HGW, HGW), lambda bi, i: (bi, 0, 0, 0)))
    out_shape = (jax.ShapeDtypeStruct((b, t, HGW), F32),
                 jax.ShapeDtypeStruct((b, t, HGW), F32),
                 jax.ShapeDtypeStruct((b, 2, HGW, HGW), F32))
    nc = tb // CHUNK
    of, ob, s_fin = pl.pallas_call(
        functools.partial(_hgrn_kernel, tb=tb, layer=layer),
        grid=(b, nb), in_specs=in_specs, out_specs=out_specs, out_shape=out_shape,
        scratch_shapes=[pltpu.VMEM((nc, HGW, HGW), BF16), pltpu.VMEM((nc, HGW, HGW), F32),
                        pltpu.VMEM((nc, HGW, HGW), BF16), pltpu.VMEM((nc, HGW, HGW), F32)],
        compiler_params=pltpu.CompilerParams(
            dimension_semantics=("parallel", "arbitrary"), vmem_limit_bytes=VMEM_LIMIT),
        name="hgrn",
    )(zq, zf, zi, zq, zf, zi, lb_logits, s0)
    return of, ob, s_fin


def _merge_kernel(h_ref, u_ref, mod_ref, n2_ref, oda_ref, of_ref, ob_ref, hg_ref,
                  zp_ref, zpp_ref, zpn_ref, hgn_ref, pw_ref, ps_ref,
                  wg_ref, wpa_ref, wph_ref, wpp_ref, wo_ref, w1_ref, w3_ref, w2_ref, fn_ref,
                  out_ref, *, tm, t_total, final):
    i = pl.program_id(1)
    h1 = h_ref[0]
    m = mod_ref[0]

    x = zp_ref[0]
    prev = jnp.where(i > 0, zpp_ref[0], 0.0)
    nxt = jnp.where(i < pl.num_programs(1) - 1, zpn_ref[0], 0.0)
    e = jnp.concatenate([prev, x, nxt], axis=0)
    n = tm + 16
    a2 = e[0:n - 1] + e[1:n]
    a4 = a2[0:n - 3] + a2[2:n - 1]
    a8 = a4[0:n - 7] + a4[4:n - 3]
    a16 = a8[0:n - 15] + a8[8:n - 7]
    sums = (a2[7:7 + tm], a4[6:6 + tm], a8[4:4 + tm], a16[0:tm])
    group = lax.broadcasted_iota(jnp.int32, (1, PW), 1) // 64
    total, half = sums[-1], jnp.full((1, PW), POOL_WINDOWS[-1] // 2, jnp.int32)
    for g in range(len(POOL_WINDOWS) - 2, -1, -1):
        total = jnp.where(group == g, sums[g], total)
        half = jnp.where(group == g, POOL_WINDOWS[g] // 2, half)
    mean = total * (0.5 / half.astype(F32))

    def clipped(rows, r0):
        pos = i * tm + r0 + lax.broadcasted_iota(jnp.int32, (8, PW), 0)
        cnt = jnp.minimum(pos + half, t_total) - jnp.maximum(pos - half, 0)
        return rows / cnt.astype(F32)

    mean = jnp.concatenate([clipped(total[0:8], 0), mean[8:tm - 8], clipped(total[tm - 8:tm], tm - 8)], axis=0)
    o_pool = _dot((mean - x).astype(BF16), pw_ref[...]) * ps_ref[...]

    o = of_ref[0] + ob_ref[0]
    o2 = o * o
    hi = o2.astype(BF16)
    lo = (o2 - hi.astype(F32)).astype(BF16)
    ones_bd = (lax.broadcasted_iota(jnp.int32, (HGW, HGW), 0) // HGH
               == lax.broadcasted_iota(jnp.int32, (HGW, HGW), 1) // HGH).astype(BF16)
    seg = _dot(hi, ones_bd) + _dot(lo, ones_bd)
    zg = hg_ref[0]
    o_hg = (o * lax.rsqrt(seg * (1.0 / HGH) + EPS) * hgn_ref[...]) * (zg * _sigmoid(zg))

    gate = _sigmoid(_dot(u_ref[0], wg_ref[...]))
    y = (gate[:, 0:D] * _dot(oda_ref[0], wpa_ref[...])
         + gate[:, D:2 * D] * _dot(o_hg.astype(BF16), wph_ref[...])
         + gate[:, 2 * D:3 * D] * _dot(o_pool.astype(BF16), wpp_ref[...]))
    mix = _dot(y.astype(BF16), wo_ref[...])
    h2 = h1 + m[5:6] * mix
    h3 = _swiglu_half(h2, m, 6, n2_ref[...], w1_ref, w3_ref, w2_ref)
    if final:
        h3 = _rms(h3, fn_ref[...])
    out_ref[0] = h3


def _merge(h1, u, mod, n2, oda, of, ob, hg, zp, hgn, pw, ps, wg, wpa, wph, wpp, wo, w1, w3, w2, fn,
           *, tm, final):
    b, t, _ = h1.shape
    nblk8 = t // 8
    r = tm // 8
    tok = lambda w: pl.BlockSpec((1, tm, w), lambda bi, i: (bi, i, 0))
    in_specs = [tok(D), tok(D),
                pl.BlockSpec((1, NMOD, D), lambda bi, i: (bi, 0, 0)),
                _const_spec((1, D)),
                tok(QKW),
                tok(HGW), tok(HGW),
                tok(HGW), tok(PW),
                pl.BlockSpec((1, 8, PW), lambda bi, i: (bi, jnp.maximum(i * r - 1, 0), 0)),
                pl.BlockSpec((1, 8, PW), lambda bi, i: (bi, jnp.minimum((i + 1) * r, nblk8 - 1), 0)),
                _const_spec((1, HGW)), _const_spec((PW, PW)), _const_spec((1, PW)),
                _const_spec((D, 3 * D)), _const_spec((QKW, D)), _const_spec((HGW, D)),
                _const_spec((PW, D)), _const_spec((D, D)),
                _const_spec((D, DFF)), _const_spec((D, DFF)), _const_spec((DFF, D)),
                _const_spec((1, D))]
    return pl.pallas_call(
        functools.partial(_merge_kernel, tm=tm, t_total=t, final=final),
        grid=(b, t // tm), in_specs=in_specs, out_specs=tok(D),
        out_shape=jax.ShapeDtypeStruct((b, t, D), F32),
        compiler_params=pltpu.CompilerParams(
            dimension_semantics=("parallel", "parallel"), vmem_limit_bytes=VMEM_LIMIT),
        name="merge_ffn",
    )(h1, u, mod, n2, oda, of, ob, hg, zp, zp, zp, hgn, pw, ps, wg, wpa, wph, wpp, wo, w1, w3, w2, fn)


def _rope_tables(n):
    rows = n // GRID_WIDTH
    row = jnp.repeat(jnp.arange(rows, dtype=jnp.int32), GRID_WIDTH).astype(F32)
    col = jnp.tile(jnp.arange(GRID_WIDTH, dtype=jnp.int32), rows).astype(F32)
    axis_dim = HALF // 2
    inv_freq = THETA ** (-jnp.arange(0, axis_dim, 2, dtype=F32) / axis_dim)
    ang_r = row[:, None] * inv_freq[None, :]
    ang_c = col[:, None] * inv_freq[None, :]
    cos64 = jnp.concatenate([jnp.cos(ang_r)] * 2 + [jnp.cos(ang_c)] * 2, axis=1)
    sin64 = jnp.concatenate([-jnp.sin(ang_r), jnp.sin(ang_r), -jnp.sin(ang_c), jnp.sin(ang_c)], axis=1)
    return jnp.tile(cos64, (1, 2)), jnp.tile(sin64, (1, 2))


def _block_diag(w):
    g, a, b = w.shape
    out = jnp.zeros((g * a, g * b), w.dtype)
    for i in range(g):
        out = out.at[i * a:(i + 1) * a, i * b:(i + 1) * b].set(w[i])
    return out


def kernel(x, c, ctx, c_ctx, w_ada, b_ada, norm_ffn1, norm_mix, norm_ffn2, ffn1_w1, ffn1_w3, ffn1_w2,
           ffn2_w1, ffn2_w3, ffn2_w2, w_in, da_lambda_q1, da_lambda_k1, da_lambda_q2, da_lambda_k2,
           da_subln, hg_lb_logits, hg_norm, pool_w, pool_scale, w_proj_da, w_proj_hg, w_proj_pool,
           w_out, final_norm):
    bsz, n, _ = x.shape
    n_ctx = ctx.shape[1]
    depth = w_ada.shape[0]
    lat, cx = _tiles(n), _tiles(n_ctx)

    cvec = jnp.zeros((8, D), F32).at[0:bsz].set(c).at[bsz].set(c_ctx)
    mods = _ada(cvec, w_ada, b_ada).reshape(depth, 8, NMOD, D)
    rc, rs = _rope_tables(n)
    rc_c = jnp.zeros((n_ctx, 128), F32)
    row = lambda v: v.reshape(1, -1)
    bf = lambda w: w.astype(BF16)

    h, hc = x, ctx
    for l in range(depth):
        need_ctx = l < depth - 1
        lam_init = 0.8 - 0.6 * math.exp(-0.3 * l)
        m_lat = mods[l, 0:bsz]
        m_ctx = jnp.broadcast_to(mods[l, bsz][None], (bsz, NMOD, D))
        w1a, w3a, w2a = bf(ffn1_w1[l]), bf(ffn1_w3[l]), bf(ffn1_w2[l])
        w1b, w3b, w2b = bf(ffn2_w1[l]), bf(ffn2_w3[l]), bf(ffn2_w2[l])
        win_a = bf(w_in[l][:, :IN_A])
        w_gate = bf(w_in[l][:, IN_A:])
        lamv = jnp.stack([da_lambda_q1[l], da_lambda_k1[l], da_lambda_q2[l], da_lambda_k2[l]]).astype(F32)
        gain_da = da_subln[l].reshape(VDIM, 1)
        hgn = row(jnp.tile(hg_norm[l], HEADS))
        pw_bd = bf(_block_diag(pool_w[l]))
        merge_w = (hgn, pw_bd, row(pool_scale[l]), w_gate, bf(w_proj_da[l]), bf(w_proj_hg[l]),
                   bf(w_proj_pool[l]), bf(w_out[l]), w1b, w3b, w2b, row(final_norm))

        (hc1, u_c, qt_c, k_c, vt_c, hq_c, hf_c, hi_c, hg_c, zp_c) = _ffn_in(
            hc, m_ctx, row(norm_ffn1[l]), row(norm_mix[l]), w1a, w3a, w2a, win_a, rc_c, rc_c,
            rope=False, tm=cx.tm)
        s0 = jnp.zeros((bsz, 2, HGW, HGW), F32)
        of_c, ob_c, s_ctx = _hgrn(hq_c, hf_c, hi_c, hg_lb_logits, s0, layer=l, tb=cx.tb)

        (h1, u, qt, k, vt, hq, hf, hi, hg, zp) = _ffn_in(
            h, m_lat, row(norm_ffn1[l]), row(norm_mix[l]), w1a, w3a, w2a, win_a, rc, rs,
            rope=True, tm=lat.tm)
        o_da = _attention(qt, k, vt, k_c, vt_c, lamv, gain_da, lam_init=lam_init, tq=lat.tq, tk=lat.tk)
        of, ob, _ = _hgrn(hq, hf, hi, hg_lb_logits, s_ctx, layer=l, tb=lat.tb)
        h = _merge(h1, u, m_lat, row(norm_ffn2[l]), o_da, of, ob, hg, zp, *merge_w,
                   tm=lat.tm, final=not need_ctx)
        if need_ctx:
            o_dac = _attention(qt_c, k_c, vt_c, None, None, lamv, gain_da, lam_init=lam_init,
                               tq=cx.tq, tk=cx.tk)
            hc = _merge(hc1, u_c, m_ctx, row(norm_ffn2[l]), o_dac, of_c, ob_c, hg_c, zp_c,
                        *merge_w, tm=cx.tm, final=False)
    return h
```

```python
import functools
import math
from typing import NamedTuple

import jax
import jax.numpy as jnp
from jax import lax
from jax.experimental import pallas as pl
from jax.experimental.pallas import tpu as pltpu

D = 1024
GRID_WIDTH = 64
EPS = 1e-6
LB_MIN = 1e-20
NMOD = 9
HEADS = 4
HALF = 64
VDIM = 128
QKW = HEADS * 2 * HALF
HGW = 256
HGH = 64
PW = 256
POOL_WINDOWS = (2, 4, 8, 16)
DFF = 2816
THETA = 10000.0
IN_A = 3 * QKW + 6 * 256
CHUNK = 64
PAIR_RANGE = 80.0
NEG = -1e30
LOG2E = 1.4426950408889634
BOUND_MARGIN = 1.01
L_FLOOR = 2.0 ** -86

F32 = jnp.float32
BF16 = jnp.bfloat16
VMEM_LIMIT = 56 * 1024 * 1024

TOKEN_TILE = 512
QUERY_TILE = 1024
KEY_CHUNK = 1024
SCAN_BLOCK = 512


class _Tiles(NamedTuple):
    tm: int
    tq: int
    tk: int
    tb: int


def _tiles(n):
    return _Tiles(min(TOKEN_TILE, n), min(QUERY_TILE, n), min(KEY_CHUNK, n), min(SCAN_BLOCK, n))


def _const_spec(shape):
    nd = len(shape)
    return pl.BlockSpec(shape, lambda *_: (0,) * nd, pipeline_mode=pl.Buffered(1))


def _layer_spec(shape, layer, col=0):
    return pl.BlockSpec((1,) + shape, lambda *_: (layer, 0, col), pipeline_mode=pl.Buffered(1))


def _rms(x, gain):
    return x * lax.rsqrt(jnp.mean(x * x, axis=-1, keepdims=True) + EPS) * gain


def _sigmoid(x):
    return 1.0 / (1.0 + jnp.exp(-x))


def _dot(a, b):
    return jnp.dot(a, b, preferred_element_type=F32)


def _dot_nt(a, b):
    return lax.dot_general(a, b, (((1,), (1,)), ((), ())), preferred_element_type=F32)


def _dot_tn(a, b):
    return lax.dot_general(a, b, (((0,), (0,)), ((), ())), preferred_element_type=F32)


def _ada_kernel(c_ref, w_ref, b_ref, o_ref):
    c = c_ref[...]
    s = c * _sigmoid(c)
    o_ref[0] = jnp.dot(s, w_ref[0], preferred_element_type=F32,
                       precision=lax.Precision.HIGHEST) + b_ref[0]


def _ada(cvec, w_ada, b_ada):
    depth = w_ada.shape[0]
    nblk = (NMOD * D) // D
    return pl.pallas_call(
        _ada_kernel,
        grid=(depth, nblk),
        in_specs=[pl.BlockSpec((8, D), lambda l, j: (0, 0)),
                  pl.BlockSpec((1, D, D), lambda l, j: (l, 0, j)),
                  pl.BlockSpec((1, 1, D), lambda l, j: (l, 0, j))],
        out_specs=pl.BlockSpec((1, 8, D), lambda l, j: (l, 0, j)),
        out_shape=jax.ShapeDtypeStruct((depth, 8, NMOD * D), F32),
        name="ada",
    )(cvec, w_ada, b_ada.reshape(depth, 1, NMOD * D))


def _swiglu_half(h, m, shift_i, gain, w1_ref, w3_ref, w2_ref):
    x = _rms(h, gain) * (1.0 + m[shift_i + 1:shift_i + 2]) + m[shift_i:shift_i + 1]
    xb = x.astype(BF16)
    a = _dot(xb, w1_ref[0])
    b = _dot(xb, w3_ref[0])
    g = (a * _sigmoid(a) * b).astype(BF16)
    y = _dot(g, w2_ref[0])
    return h + 0.5 * m[shift_i + 2:shift_i + 3] * y


def _ffn_in_kernel(h_ref, mod_ref, n1_ref, nm_ref, w1_ref, w3_ref, w2_ref, win_ref, rc_ref, rs_ref,
                   h1_ref, u_ref, qt_ref, k_ref, vt_ref, hq_ref, hf_ref, hi_ref, hg_ref, zp_ref, *, rope):
    h = h_ref[0]
    m = mod_ref[0]
    h1 = _swiglu_half(h, m, 0, n1_ref[...], w1_ref, w3_ref, w2_ref)
    h1_ref[0] = h1
    u = (_rms(h1, nm_ref[...]) * (1.0 + m[4:5]) + m[3:4]).astype(BF16)
    u_ref[0] = u
    z = _dot(u, win_ref[0])
    q = z[:, 0:QKW]
    k = z[:, QKW:2 * QKW]
    v = z[:, 2 * QKW:3 * QKW]
    if rope:
        lane = lax.broadcasted_iota(jnp.int32, q.shape, 1)
        first = (lane % 32) < 16
        rc = jnp.concatenate([rc_ref[...]] * (QKW // 128), axis=1)
        rs = jnp.concatenate([rs_ref[...]] * (QKW // 128), axis=1)

        def rot(x):
            partner = jnp.where(first, pltpu.roll(x, QKW - 16, 1), pltpu.roll(x, 16, 1))
            return x * rc + partner * rs

        q = rot(q)
        k = rot(k)
    q = q * (HALF ** -0.5 * LOG2E)
    qt_ref[0] = q.T.astype(BF16)
    k_ref[0] = k.astype(BF16)
    vt_ref[0] = v.T.astype(BF16)
    o = 3 * QKW
    hq_ref[0] = z[:, o:o + 256]
    hf_ref[0, 0] = z[:, o + 256:o + 512]
    hf_ref[0, 1] = z[:, o + 512:o + 768]
    hi_ref[0] = z[:, o + 768:o + 1024]
    hg_ref[0] = z[:, o + 1024:o + 1280]
    zp_ref[0] = z[:, o + 1280:o + 1536]


def _ffn_in(h, mod, n1, nm, w1, w3, w2, win, rc, rs, *, layer, rope, tm):
    b, t, _ = h.shape
    grid = (b, t // tm)
    tok = lambda w: pl.BlockSpec((1, tm, w), lambda bi, i: (bi, i, 0))
    tokt = pl.BlockSpec((1, QKW, tm), lambda bi, i: (bi, 0, i))
    out_shape = (
        jax.ShapeDtypeStruct((b, t, D), F32),
        jax.ShapeDtypeStruct((b, t, D), BF16),
        jax.ShapeDtypeStruct((b, QKW, t), BF16),
        jax.ShapeDtypeStruct((b, t, QKW), BF16),
        jax.ShapeDtypeStruct((b, QKW, t), BF16),
        jax.ShapeDtypeStruct((b, t, 256), F32),
        jax.ShapeDtypeStruct((b, 2, t, 256), F32),
        jax.ShapeDtypeStruct((b, t, 256), F32),
        jax.ShapeDtypeStruct((b, t, 256), F32),
        jax.ShapeDtypeStruct((b, t, 256), F32),
    )
    out_specs = (tok(D), tok(D), tokt, tok(QKW), tokt, tok(256),
                 pl.BlockSpec((1, 2, tm, 256), lambda bi, i: (bi, 0, i, 0)),
                 tok(256), tok(256), tok(256))
    in_specs = [tok(D),
                pl.BlockSpec((1, NMOD, D), lambda bi, i: (bi, 0, 0)),
                _const_spec((1, D)), _const_spec((1, D)),
                _layer_spec((D, DFF), layer), _layer_spec((D, DFF), layer), _layer_spec((DFF, D), layer),
                _layer_spec((D, IN_A), layer, 0),
                pl.BlockSpec((tm, 128), lambda bi, i: (i, 0)),
                pl.BlockSpec((tm, 128), lambda bi, i: (i, 0))]
    return pl.pallas_call(
        functools.partial(_ffn_in_kernel, rope=rope),
        grid=grid, in_specs=in_specs, out_specs=out_specs, out_shape=out_shape,
        compiler_params=pltpu.CompilerParams(
            dimension_semantics=("parallel", "parallel"), vmem_limit_bytes=VMEM_LIMIT),
        name="ffn_in",
    )(h, mod, n1, nm, w1, w3, w2, win, rc, rs)


def _attn_kernel(*refs, tq, tk, n_main, has_extra, lam_init):
    if has_extra:
        (qt_ref, k_ref, vt_ref, kx_ref, vtx_ref, lam_ref, gain_ref, o_ref,
         acc1, acc2, l_sc, kmax) = refs
    else:
        (qt_ref, k_ref, vt_ref, lam_ref, gain_ref, o_ref, acc1, acc2, l_sc, kmax) = refs
        kx_ref = vtx_ref = None
    qt = qt_ref[0]
    row = lax.broadcasted_iota(jnp.int32, qt.shape, 0)
    zero = jnp.zeros_like(qt)
    qbd = jnp.concatenate([jnp.where(row < HALF, qt, zero), jnp.where(row >= HALF, qt, zero)], axis=1)

    def rows(c):
        return pl.ds(c * tk if isinstance(c, int) else pl.multiple_of(c * tk, tk), tk)

    def kchunk(c):
        return k_ref[0, rows(c), :]

    def vchunk(c):
        return vt_ref[0, :, rows(c)]

    @pl.when(pl.program_id(2) == 0)
    def _():
        lane = lax.broadcasted_iota(jnp.int32, (16, VDIM), 1)
        r16 = lax.broadcasted_iota(jnp.int32, (16, VDIM), 0)
        sel = jnp.where((r16 == 0) & (lane < HALF) | (r16 == 1) & (lane >= HALF), 1.0, 0.0).astype(BF16)

        def sq_norms(kb):
            kf = kb.astype(F32)
            return _dot_nt(sel, (kf * kf).astype(BF16))

        n2 = lax.fori_loop(0, n_main, lambda c, mx: jnp.maximum(mx, sq_norms(kchunk(c))),
                           jnp.zeros((16, tk), F32))
        top = jnp.max(n2, axis=1, keepdims=True)
        if has_extra:
            top = jnp.maximum(top, jnp.max(sq_norms(kx_ref[0]), axis=1, keepdims=True))
        kmax[...] = jnp.broadcast_to(jnp.sqrt(top[0:8]), kmax.shape)

    qf = qt.astype(F32)
    q1 = jnp.sqrt(jnp.sum(jnp.where(row < HALF, qf * qf, 0.0), axis=0, keepdims=True))
    q2 = jnp.sqrt(jnp.sum(jnp.where(row >= HALF, qf * qf, 0.0), axis=0, keepdims=True))
    km = kmax[...]
    bound = jnp.concatenate([q1 * km[0:1, 0:1], q2 * km[1:2, 0:1]], axis=1) * BOUND_MARGIN
    acc1[...] = jnp.zeros_like(acc1)
    acc2[...] = jnp.zeros_like(acc2)

    def fast_step(kb, vtb, l):
        p = jnp.exp2(_dot(kb, qbd) - bound)
        pb = p.astype(BF16)
        acc1[...] += _dot(vtb, pb[:, :tq])
        acc2[...] += _dot(vtb, pb[:, tq:])
        return l + jnp.sum(p, axis=0, keepdims=True)

    l = jnp.zeros((1, 2 * tq), F32)
    for c in range(n_main):
        l = fast_step(kchunk(c), vchunk(c), l)
    if has_extra:
        l = fast_step(kx_ref[0], vtx_ref[0], l)
    l_sc[...] = jnp.broadcast_to(l, l_sc.shape)

    @pl.when(jnp.logical_not(jnp.min(l) >= L_FLOOR))
    def _():
        acc1[...] = jnp.zeros_like(acc1)
        acc2[...] = jnp.zeros_like(acc2)

        def safe_step(kb, vtb, carry):
            m, ls = carry
            s = _dot(kb, qbd)
            m_new = jnp.maximum(m, jnp.max(s, axis=0, keepdims=True))
            alpha = jnp.exp2(m - m_new)
            p = jnp.exp2(s - m_new)
            pb = p.astype(BF16)
            acc1[...] = acc1[...] * alpha[:, :tq] + _dot(vtb, pb[:, :tq])
            acc2[...] = acc2[...] * alpha[:, tq:] + _dot(vtb, pb[:, tq:])
            return m_new, alpha * ls + jnp.sum(p, axis=0, keepdims=True)

        carry = (jnp.full((1, 2 * tq), NEG, F32), jnp.zeros((1, 2 * tq), F32))
        carry = lax.fori_loop(0, n_main, lambda c, cr: safe_step(kchunk(c), vchunk(c), cr), carry)
        if has_extra:
            carry = safe_step(kx_ref[0], vtx_ref[0], carry)
        l_sc[...] = jnp.broadcast_to(carry[1], l_sc.shape)

    l = l_sc[0:1, :]

    lv = lam_ref[...]
    lam = (jnp.exp(jnp.sum(lv[0:1] * lv[1:2], axis=1, keepdims=True))
           - jnp.exp(jnp.sum(lv[2:3] * lv[3:4], axis=1, keepdims=True)) + lam_init)
    o = acc1[...] / l[:, :tq] - lam * (acc2[...] / l[:, tq:])
    o = o * lax.rsqrt(jnp.mean(o * o, axis=0, keepdims=True) + EPS) * gain_ref[...] * (1.0 - lam_init)
    o_ref[0] = o.T.astype(BF16)


def _attention(qt, k, vt, kx, vtx, lamv, gain, *, lam_init, tq, tk):
    b, _, t_q = qt.shape
    t_k = k.shape[1]
    has_extra = kx is not None
    assert t_k % tk == 0
    grid = (b, HEADS, t_q // tq)
    in_specs = [pl.BlockSpec((1, VDIM, tq), lambda bi, hi, i: (bi, hi, i)),
                pl.BlockSpec((1, t_k, VDIM), lambda bi, hi, i: (bi, 0, hi)),
                pl.BlockSpec((1, VDIM, t_k), lambda bi, hi, i: (bi, hi, 0))]
    args = [qt, k, vt]
    if has_extra:
        t_x = kx.shape[1]
        in_specs += [pl.BlockSpec((1, t_x, VDIM), lambda bi, hi, i: (bi, 0, hi)),
                     pl.BlockSpec((1, VDIM, t_x), lambda bi, hi, i: (bi, hi, 0))]
        args += [kx, vtx]
    in_specs += [pl.BlockSpec((4, HALF), lambda bi, hi, i: (0, 0)),
                 pl.BlockSpec((VDIM, 1), lambda bi, hi, i: (0, 0))]
    args += [lamv, gain]
    return pl.pallas_call(
        functools.partial(_attn_kernel, tq=tq, tk=tk, n_main=t_k // tk, has_extra=has_extra,
                          lam_init=lam_init),
        grid=grid, in_specs=in_specs,
        out_specs=pl.BlockSpec((1, tq, VDIM), lambda bi, hi, i: (bi, i, hi)),
        out_shape=jax.ShapeDtypeStruct((b, t_q, HEADS * VDIM), BF16),
        scratch_shapes=[pltpu.VMEM((VDIM, tq), F32), pltpu.VMEM((VDIM, tq), F32),
                        pltpu.VMEM((8, 2 * tq), F32), pltpu.VMEM((8, 128), F32)],
        compiler_params=pltpu.CompilerParams(
            dimension_semantics=("parallel", "parallel", "arbitrary"), vmem_limit_bytes=VMEM_LIMIT),
        name="diff_attn",
    )(*args)


def _chunk_cumsum(x, rev):
    n = x.shape[0]
    pos = lax.broadcasted_iota(jnp.int32, x.shape, 0) % CHUNK
    sh = 1
    while sh < CHUNK:
        if rev:
            x = x + jnp.where(pos + sh < CHUNK, pltpu.roll(x, n - sh, 0), 0.0)
        else:
            x = x + jnp.where(pos >= sh, pltpu.roll(x, sh, 0), 0.0)
        sh *= 2
    return x


def _hgrn_gates(zq, zf, lbv):
    q = zq * _sigmoid(zq)
    e = jnp.exp(-jnp.abs(zf))
    r = 1.0 / (1.0 + e)
    er = e * r
    pos = zf >= 0.0
    kk = (1.0 - lbv) * jnp.where(pos, er, r)
    lf = jnp.log(jnp.maximum(lbv, LB_MIN) + (1.0 - lbv) * jnp.where(pos, r, er))
    return q, kk, lf


def _hgrn_token_scan(zq_ref, zf_ref, zi_ref, lbv, st_ref, o_ref, *, rev, tb):
    head_mask = (lax.broadcasted_iota(jnp.int32, (HGW, HGW), 0) // HGH
                 == lax.broadcasted_iota(jnp.int32, (HGW, HGW), 1) // HGH)
    first = lax.broadcasted_iota(jnp.int32, (16, HGW), 0) == 0

    def pad16(r):
        return jnp.where(first, jnp.broadcast_to(r, (16, HGW)), 0.0).astype(BF16)

    def body(i, carry):
        t = (tb - 1 - i) if rev else i
        q, kk, lf = _hgrn_gates(zq_ref[pl.ds(t, 1), :], zf_ref[pl.ds(t, 1), :], lbv)
        st = st_ref[...] * jnp.exp(lf) + jnp.where(
            head_mask, _dot_tn(pad16(zi_ref[pl.ds(t, 1), :]), pad16(kk)), 0.0)
        st_ref[...] = st
        o_ref[pl.ds(t, 1), :] = _dot_nt(pad16(q), st.astype(BF16))[0:1]
        return carry

    lax.fori_loop(0, tb, body, 0)


def _hgrn_direction(zq_ref, zf_ref, zi_ref, lbv, st_ref, o_ref, stall, dst, *, rev, tb):
    nc = tb // CHUNK
    zi = zi_ref[...]
    q, kk, lf = _hgrn_gates(zq_ref[...], zf_ref[...], lbv)
    a = _chunk_cumsum(lf, rev)
    a3 = a.reshape(nc, CHUNK, HGW)
    end_row = 0 if rev else CHUNK - 1
    mid_row = CHUNK // 2 if rev else CHUNK // 2 - 1
    a_end = a3[:, end_row:end_row + 1, :]
    a_mid = a3[:, mid_row:mid_row + 1, :]
    in_range = jnp.max(jnp.abs(a3 - a_mid)) <= PAIR_RANGE
    q3 = q.reshape(nc, CHUNK, HGW)
    k3 = kk.reshape(nc, CHUNK, HGW)
    qe = (q3 * jnp.exp(a3)).astype(BF16)
    ke = (k3 * jnp.exp(a_end - a3)).astype(BF16)
    qm = (q3 * jnp.exp(a3 - a_mid)).reshape(tb, HGW)
    km = (k3 * jnp.exp(a_mid - a3)).reshape(tb, HGW).astype(BF16)
    vb = zi.astype(BF16)
    v3 = vb.reshape(nc, CHUNK, HGW)
    dec = jnp.exp(a_end)

    lane_head = lax.broadcasted_iota(jnp.int32, (1, HGW), 1) // HGH
    sub = 128
    r_i = lax.broadcasted_iota(jnp.int32, (sub, sub), 0)
    c_i = lax.broadcasted_iota(jnp.int32, (sub, sub), 1)
    same = (r_i // CHUNK) == (c_i // CHUNK)
    causal = jnp.logical_and(same, (c_i >= r_i) if rev else (c_i <= r_i))
    causal4 = jnp.concatenate([causal] * HEADS, axis=0)
    intra = []
    for g in range(tb // sub):
        sl = slice(g * sub, (g + 1) * sub)
        qg = qm[sl]
        qstack = jnp.concatenate(
            [jnp.where(lane_head == hh, qg, 0.0) for hh in range(HEADS)], axis=0).astype(BF16)
        sc = _dot_nt(qstack, km[sl])
        sc = jnp.where(causal4, sc, 0.0).astype(BF16)
        r = _dot(sc, vb[sl])
        og = jnp.zeros((sub, HGW), F32)
        for hh in range(HEADS):
            og = og + jnp.where(lane_head == hh, r[hh * sub:(hh + 1) * sub], 0.0)
        intra.append(og)
    o_intra = jnp.concatenate(intra, axis=0)

    head_mask = (lax.broadcasted_iota(jnp.int32, (HGW, HGW), 0) // HGH
                 == lax.broadcasted_iota(jnp.int32, (HGW, HGW), 1) // HGH)
    for c in range(nc):
        dst[c] = jnp.where(head_mask, _dot_tn(v3[c], ke[c]), 0.0)
    st = st_ref[...]
    order = range(nc - 1, -1, -1) if rev else range(nc)
    for c in order:
        stall[c] = st.astype(BF16)
        st = dec[c] * st + dst[c]
    inter = [_dot_nt(qe[c], stall[c]) for c in range(nc)]
    o_ref[...] = o_intra + jnp.concatenate(inter, axis=0)

    @pl.when(in_range)
    def _():
        st_ref[...] = st

    @pl.when(jnp.logical_not(in_range))
    def _():
        _hgrn_token_scan(zq_ref, zf_ref, zi_ref, lbv, st_ref, o_ref, rev=rev, tb=tb)


def _hgrn_kernel(zqf_ref, zff_ref, zif_ref, zqb_ref, zfb_ref, zib_ref, lb_ref, s0_ref,
                 of_ref, ob_ref, s_ref, stall_f, dst_f, stall_b, dst_b, *, tb, layer):
    @pl.when(pl.program_id(1) == 0)
    def _():
        s_ref[...] = s0_ref[...]

    logits = lb_ref[...]
    e = jnp.exp(logits - jnp.max(logits, axis=0, keepdims=True))
    p = e / jnp.sum(e, axis=0, keepdims=True)
    lb = jnp.sum(p[0:layer + 1], axis=0) - p[0]
    _hgrn_direction(zqf_ref.at[0], zff_ref.at[0, 0], zif_ref.at[0], lb[0:1], s_ref.at[0, 0], of_ref.at[0],
                    stall_f, dst_f, rev=False, tb=tb)
    _hgrn_direction(zqb_ref.at[0], zfb_ref.at[0, 0], zib_ref.at[0], lb[1:2], s_ref.at[0, 1], ob_ref.at[0],
                    stall_b, dst_b, rev=True, tb=tb)


def _hgrn(zq, zf, zi, lb_logits, s0, *, layer, tb):
    b, t, _ = zq.shape
    nb = t // tb
    fwd = lambda bi, i: (bi, i, 0)
    bwd = lambda bi, i: (bi, nb - 1 - i, 0)
    in_specs = [pl.BlockSpec((1, tb, HGW), fwd),
                pl.BlockSpec((1, 1, tb, HGW), lambda bi, i: (bi, 0, i, 0)),
                pl.BlockSpec((1, tb, HGW), fwd),
                pl.BlockSpec((1, tb, HGW), bwd),
                pl.BlockSpec((1, 1, tb, HGW), lambda bi, i: (bi, 1, nb - 1 - i, 0)),
                pl.BlockSpec((1, tb, HGW), bwd),
                pl.BlockSpec(lb_logits.shape, lambda bi, i: (0, 0, 0)),
                pl.BlockSpec((1, 2, HGW, HGW), lambda bi, i: (bi, 0, 0, 0))]
    out_specs = (pl.BlockSpec((1, tb, HGW), fwd),
                 pl.BlockSpec((1, tb, HGW), bwd),
                 pl.BlockSpec((1, 2, HGW, HGW), lambda bi, i: (bi, 0, 0, 0)))
    out_shape = (jax.ShapeDtypeStruct((b, t, HGW), F32),
                 jax.ShapeDtypeStruct((b, t, HGW), F32),
                 jax.ShapeDtypeStruct((b, 2, HGW, HGW), F32))
    nc = tb // CHUNK
    of, ob, s_fin = pl.pallas_call(
        functools.partial(_hgrn_kernel, tb=tb, layer=layer),
        grid=(b, nb), in_specs=in_specs, out_specs=out_specs, out_shape=out_shape,
        scratch_shapes=[pltpu.VMEM((nc, HGW, HGW), BF16), pltpu.VMEM((nc, HGW, HGW), F32),
                        pltpu.VMEM((nc, HGW, HGW), BF16), pltpu.VMEM((nc, HGW, HGW), F32)],
        compiler_params=pltpu.CompilerParams(
            dimension_semantics=("parallel", "arbitrary"), vmem_limit_bytes=VMEM_LIMIT),
        name="hgrn",
    )(zq, zf, zi, zq, zf, zi, lb_logits, s0)
    return of, ob, s_fin


def _merge_kernel(h_ref, u_ref, mod_ref, n2_ref, oda_ref, of_ref, ob_ref, hg_ref,
                  zp_ref, zpp_ref, zpn_ref, hgn_ref, pw_ref, ps_ref,
                  wg_ref, wpa_ref, wph_ref, wpp_ref, wo_ref, w1_ref, w3_ref, w2_ref, fn_ref,
                  out_ref, *, tm, t_total, final):
    i = pl.program_id(1)
    h1 = h_ref[0]
    m = mod_ref[0]

    x = zp_ref[0]
    prev = jnp.where(i > 0, zpp_ref[0], 0.0)
    nxt = jnp.where(i < pl.num_programs(1) - 1, zpn_ref[0], 0.0)
    e = jnp.concatenate([prev, x, nxt], axis=0)
    n = tm + 16
    a2 = e[0:n - 1] + e[1:n]
    a4 = a2[0:n - 3] + a2[2:n - 1]
    a8 = a4[0:n - 7] + a4[4:n - 3]
    a16 = a8[0:n - 15] + a8[8:n - 7]
    sums = (a2[7:7 + tm], a4[6:6 + tm], a8[4:4 + tm], a16[0:tm])
    group = lax.broadcasted_iota(jnp.int32, (1, PW), 1) // 64
    total, half = sums[-1], jnp.full((1, PW), POOL_WINDOWS[-1] // 2, jnp.int32)
    for g in range(len(POOL_WINDOWS) - 2, -1, -1):
        total = jnp.where(group == g, sums[g], total)
        half = jnp.where(group == g, POOL_WINDOWS[g] // 2, half)
    mean = total * (0.5 / half.astype(F32))

    def clipped(rows, r0):
        pos = i * tm + r0 + lax.broadcasted_iota(jnp.int32, (8, PW), 0)
        cnt = jnp.minimum(pos + half, t_total) - jnp.maximum(pos - half, 0)
        return rows / cnt.astype(F32)

    mean = jnp.concatenate([clipped(total[0:8], 0), mean[8:tm - 8], clipped(total[tm - 8:tm], tm - 8)], axis=0)
    o_pool = _dot((mean - x).astype(BF16), pw_ref[...]) * ps_ref[...]

    o = of_ref[0] + ob_ref[0]
    o2 = o * o
    hi = o2.astype(BF16)
    lo = (o2 - hi.astype(F32)).astype(BF16)
    ones_bd = (lax.broadcasted_iota(jnp.int32, (HGW, HGW), 0) // HGH
               == lax.broadcasted_iota(jnp.int32, (HGW, HGW), 1) // HGH).astype(BF16)
    seg = _dot(hi, ones_bd) + _dot(lo, ones_bd)
    zg = hg_ref[0]
    o_hg = (o * lax.rsqrt(seg * (1.0 / HGH) + EPS) * hgn_ref[...]) * (zg * _sigmoid(zg))

    gate = _sigmoid(_dot(u_ref[0], wg_ref[0]))
    y = (gate[:, 0:D] * _dot(oda_ref[0], wpa_ref[0])
         + gate[:, D:2 * D] * _dot(o_hg.astype(BF16), wph_ref[0])
         + gate[:, 2 * D:3 * D] * _dot(o_pool.astype(BF16), wpp_ref[0]))
    mix = _dot(y.astype(BF16), wo_ref[0])
    h2 = h1 + m[5:6] * mix
    h3 = _swiglu_half(h2, m, 6, n2_ref[...], w1_ref, w3_ref, w2_ref)
    if final:
        h3 = _rms(h3, fn_ref[...])
    out_ref[0] = h3


def _merge(h1, u, mod, n2, oda, of, ob, hg, zp, hgn, pw, ps, wg, wpa, wph, wpp, wo, w1, w3, w2, fn,
           *, layer, tm, final):
    b, t, _ = h1.shape
    nblk8 = t // 8
    r = tm // 8
    tok = lambda w: pl.BlockSpec((1, tm, w), lambda bi, i: (bi, i, 0))
    in_specs = [tok(D), tok(D),
                pl.BlockSpec((1, NMOD, D), lambda bi, i: (bi, 0, 0)),
                _const_spec((1, D)),
                tok(QKW),
                tok(HGW), tok(HGW),
                tok(HGW), tok(PW),
                pl.BlockSpec((1, 8, PW), lambda bi, i: (bi, jnp.maximum(i * r - 1, 0), 0)),
                pl.BlockSpec((1, 8, PW), lambda bi, i: (bi, jnp.minimum((i + 1) * r, nblk8 - 1), 0)),
                _const_spec((1, HGW)), _const_spec((PW, PW)), _const_spec((1, PW)),
                _layer_spec((D, 3 * D), layer, 1), _layer_spec((QKW, D), layer), _layer_spec((HGW, D), layer),
                _layer_spec((PW, D), layer), _layer_spec((D, D), layer),
                _layer_spec((D, DFF), layer), _layer_spec((D, DFF), layer), _layer_spec((DFF, D), layer),
                _const_spec((1, D))]
    return pl.pallas_call(
        functools.partial(_merge_kernel, tm=tm, t_total=t, final=final),
        grid=(b, t // tm), in_specs=in_specs, out_specs=tok(D),
        out_shape=jax.ShapeDtypeStruct((b, t, D), F32),
        compiler_params=pltpu.CompilerParams(
            dimension_semantics=("parallel", "parallel"), vmem_limit_bytes=VMEM_LIMIT),
        name="merge_ffn",
    )(h1, u, mod, n2, oda, of, ob, hg, zp, zp, zp, hgn, pw, ps, wg, wpa, wph, wpp, wo, w1, w3, w2, fn)


def _rope_tables(n):
    rows = n // GRID_WIDTH
    row = jnp.repeat(jnp.arange(rows, dtype=jnp.int32), GRID_WIDTH).astype(F32)
    col = jnp.tile(jnp.arange(GRID_WIDTH, dtype=jnp.int32), rows).astype(F32)
    axis_dim = HALF // 2
    inv_freq = THETA ** (-jnp.arange(0, axis_dim, 2, dtype=F32) / axis_dim)
    ang_r = row[:, None] * inv_freq[None, :]
    ang_c = col[:, None] * inv_freq[None, :]
    cos64 = jnp.concatenate([jnp.cos(ang_r)] * 2 + [jnp.cos(ang_c)] * 2, axis=1)
    sin64 = jnp.concatenate([-jnp.sin(ang_r), jnp.sin(ang_r), -jnp.sin(ang_c), jnp.sin(ang_c)], axis=1)
    return jnp.tile(cos64, (1, 2)), jnp.tile(sin64, (1, 2))


def _block_diag(w):
    g, a, b = w.shape
    out = jnp.zeros((g * a, g * b), w.dtype)
    for i in range(g):
        out = out.at[i * a:(i + 1) * a, i * b:(i + 1) * b].set(w[i])
    return out


def kernel(x, c, ctx, c_ctx, w_ada, b_ada, norm_ffn1, norm_mix, norm_ffn2, ffn1_w1, ffn1_w3, ffn1_w2,
           ffn2_w1, ffn2_w3, ffn2_w2, w_in, da_lambda_q1, da_lambda_k1, da_lambda_q2, da_lambda_k2,
           da_subln, hg_lb_logits, hg_norm, pool_w, pool_scale, w_proj_da, w_proj_hg, w_proj_pool,
           w_out, final_norm):
    bsz, n, _ = x.shape
    n_ctx = ctx.shape[1]
    depth = w_ada.shape[0]
    lat, cx = _tiles(n), _tiles(n_ctx)

    cvec = jnp.zeros((8, D), F32).at[0:bsz].set(c).at[bsz].set(c_ctx)
    mods = _ada(cvec, w_ada, b_ada).reshape(depth, 8, NMOD, D)
    rc, rs = _rope_tables(n)
    rc_c = jnp.zeros((n_ctx, 128), F32)
    row = lambda v: v.reshape(1, -1)
    bf = lambda w: w.astype(BF16)

    ffn1 = (bf(ffn1_w1), bf(ffn1_w3), bf(ffn1_w2))
    ffn2 = (bf(ffn2_w1), bf(ffn2_w3), bf(ffn2_w2))
    w_in_b = bf(w_in)
    proj = (bf(w_proj_da), bf(w_proj_hg), bf(w_proj_pool), bf(w_out))

    h, hc = x, ctx
    for l in range(depth):
        need_ctx = l < depth - 1
        lam_init = 0.8 - 0.6 * math.exp(-0.3 * l)
        m_lat = mods[l, 0:bsz]
        m_ctx = jnp.broadcast_to(mods[l, bsz][None], (bsz, NMOD, D))
        lamv = jnp.stack([da_lambda_q1[l], da_lambda_k1[l], da_lambda_q2[l], da_lambda_k2[l]]).astype(F32)
        gain_da = da_subln[l].reshape(VDIM, 1)
        hgn = row(jnp.tile(hg_norm[l], HEADS))
        pw_bd = bf(_block_diag(pool_w[l]))
        merge_w = (hgn, pw_bd, row(pool_scale[l]), w_in_b, *proj, *ffn2, row(final_norm))

        (hc1, u_c, qt_c, k_c, vt_c, hq_c, hf_c, hi_c, hg_c, zp_c) = _ffn_in(
            hc, m_ctx, row(norm_ffn1[l]), row(norm_mix[l]), *ffn1, w_in_b, rc_c, rc_c,
            layer=l, rope=False, tm=cx.tm)
        s0 = jnp.zeros((bsz, 2, HGW, HGW), F32)
        of_c, ob_c, s_ctx = _hgrn(hq_c, hf_c, hi_c, hg_lb_logits, s0, layer=l, tb=cx.tb)

        (h1, u, qt, k, vt, hq, hf, hi, hg, zp) = _ffn_in(
            h, m_lat, row(norm_ffn1[l]), row(norm_mix[l]), *ffn1, w_in_b, rc, rs,
            layer=l, rope=True, tm=lat.tm)
        o_da = _attention(qt, k, vt, k_c, vt_c, lamv, gain_da, lam_init=lam_init, tq=lat.tq, tk=lat.tk)
        of, ob, _ = _hgrn(hq, hf, hi, hg_lb_logits, s_ctx, layer=l, tb=lat.tb)
        h = _merge(h1, u, m_lat, row(norm_ffn2[l]), o_da, of, ob, hg, zp, *merge_w,
                   layer=l, tm=lat.tm, final=not need_ctx)
        if need_ctx:
            o_dac = _attention(qt_c, k_c, vt_c, None, None, lamv, gain_da, lam_init=lam_init,
                               tq=cx.tq, tk=cx.tk)
            hc = _merge(hc1, u_c, m_ctx, row(norm_ffn2[l]), o_dac, of_c, ob_c, hg_c, zp_c,
                        *merge_w, layer=l, tm=cx.tm, final=False)
    return h
```

```python
import functools
import math
from typing import NamedTuple

import jax
import jax.numpy as jnp
from jax import lax
from jax.experimental import pallas as pl
from jax.experimental.pallas import tpu as pltpu

D = 1024
GRID_WIDTH = 64
EPS = 1e-6
LB_MIN = 1e-20
NMOD = 9
HEADS = 4
HALF = 64
VDIM = 128
QKW = HEADS * 2 * HALF
HGW = 256
HGH = 64
PW = 256
POOL_WINDOWS = (2, 4, 8, 16)
DFF = 2816
THETA = 10000.0
IN_A = 3 * QKW + 6 * 256
CHUNK = 64
PAIR_RANGE = 80.0
NEG = -1e30
LOG2E = 1.4426950408889634
BOUND_MARGIN = 1.01
L_FLOOR = 2.0 ** -86

F32 = jnp.float32
BF16 = jnp.bfloat16
VMEM_LIMIT = 56 * 1024 * 1024

TOKEN_TILE = 512
QUERY_TILE = 1024
KEY_CHUNK = 1024
SCAN_BLOCK = 1024


class _Tiles(NamedTuple):
    tm: int
    tq: int
    tk: int
    tb: int


def _tiles(n):
    return _Tiles(min(TOKEN_TILE, n), min(QUERY_TILE, n), min(KEY_CHUNK, n), min(SCAN_BLOCK, n))


def _const_spec(shape):
    nd = len(shape)
    return pl.BlockSpec(shape, lambda *_: (0,) * nd, pipeline_mode=pl.Buffered(1))


def _layer_spec(shape, layer, col=0):
    return pl.BlockSpec((1,) + shape, lambda *_: (layer, 0, col), pipeline_mode=pl.Buffered(1))


def _rms(x, gain):
    return x * lax.rsqrt(jnp.mean(x * x, axis=-1, keepdims=True) + EPS) * gain


def _sigmoid(x):
    return 1.0 / (1.0 + jnp.exp(-x))


def _dot(a, b):
    return jnp.dot(a, b, preferred_element_type=F32)


def _dot_row_groups(a, b):
    g = 256
    if a.shape[0] <= g:
        return _dot(a, b)
    return jnp.concatenate([_dot(a[r:r + g], b) for r in range(0, a.shape[0], g)], axis=0)


def _dot_nt(a, b):
    return lax.dot_general(a, b, (((1,), (1,)), ((), ())), preferred_element_type=F32)


def _dot_tn(a, b):
    return lax.dot_general(a, b, (((0,), (0,)), ((), ())), preferred_element_type=F32)


def _ada_kernel(c_ref, w_ref, b_ref, o_ref):
    c = c_ref[...]
    s = c * _sigmoid(c)
    o_ref[0] = jnp.dot(s, w_ref[0], preferred_element_type=F32,
                       precision=lax.Precision.HIGHEST) + b_ref[0]


def _ada(cvec, w_ada, b_ada):
    depth = w_ada.shape[0]
    nblk = (NMOD * D) // D
    return pl.pallas_call(
        _ada_kernel,
        grid=(depth, nblk),
        in_specs=[pl.BlockSpec((8, D), lambda l, j: (0, 0)),
                  pl.BlockSpec((1, D, D), lambda l, j: (l, 0, j)),
                  pl.BlockSpec((1, 1, D), lambda l, j: (l, 0, j))],
        out_specs=pl.BlockSpec((1, 8, D), lambda l, j: (l, 0, j)),
        out_shape=jax.ShapeDtypeStruct((depth, 8, NMOD * D), F32),
        name="ada",
    )(cvec, w_ada, b_ada.reshape(depth, 1, NMOD * D))


def _swiglu_half(h, m, shift_i, gain, w1_ref, w3_ref, w2_ref):
    x = _rms(h, gain) * (1.0 + m[shift_i + 1:shift_i + 2]) + m[shift_i:shift_i + 1]
    xb = x.astype(BF16)
    a = _dot(xb, w1_ref[0])
    b = _dot(xb, w3_ref[0])
    g = (a * _sigmoid(a) * b).astype(BF16)
    y = _dot_row_groups(g, w2_ref[0])
    return h + 0.5 * m[shift_i + 2:shift_i + 3] * y


def _ffn_in_kernel(h_ref, mod_ref, n1_ref, nm_ref, w1_ref, w3_ref, w2_ref, win_ref, rc_ref, rs_ref,
                   h1_ref, u_ref, qt_ref, k_ref, vt_ref, hq_ref, hf_ref, hi_ref, hg_ref, zp_ref, *, rope):
    h = h_ref[0]
    m = mod_ref[0]
    h1 = _swiglu_half(h, m, 0, n1_ref[...], w1_ref, w3_ref, w2_ref)
    h1_ref[0] = h1
    u = (_rms(h1, nm_ref[...]) * (1.0 + m[4:5]) + m[3:4]).astype(BF16)
    u_ref[0] = u
    z = _dot(u, win_ref[0])
    q = z[:, 0:QKW]
    k = z[:, QKW:2 * QKW]
    v = z[:, 2 * QKW:3 * QKW]
    if rope:
        lane = lax.broadcasted_iota(jnp.int32, q.shape, 1)
        first = (lane % 32) < 16
        rc = jnp.concatenate([rc_ref[...]] * (QKW // 128), axis=1)
        rs = jnp.concatenate([rs_ref[...]] * (QKW // 128), axis=1)

        def rot(x):
            partner = jnp.where(first, pltpu.roll(x, QKW - 16, 1), pltpu.roll(x, 16, 1))
            return x * rc + partner * rs

        q = rot(q)
        k = rot(k)
    q = q * (HALF ** -0.5 * LOG2E)
    qt_ref[0] = q.T.astype(BF16)
    k_ref[0] = k.astype(BF16)
    vt_ref[0] = v.T.astype(BF16)
    o = 3 * QKW
    hq_ref[0] = z[:, o:o + 256]
    hf_ref[0, 0] = z[:, o + 256:o + 512]
    hf_ref[0, 1] = z[:, o + 512:o + 768]
    hi_ref[0] = z[:, o + 768:o + 1024]
    hg_ref[0] = z[:, o + 1024:o + 1280]
    zp_ref[0] = z[:, o + 1280:o + 1536]


def _ffn_in(h, mod, n1, nm, w1, w3, w2, win, rc, rs, *, layer, rope, tm):
    b, t, _ = h.shape
    grid = (b, t // tm)
    tok = lambda w: pl.BlockSpec((1, tm, w), lambda bi, i: (bi, i, 0))
    tokt = pl.BlockSpec((1, QKW, tm), lambda bi, i: (bi, 0, i))
    out_shape = (
        jax.ShapeDtypeStruct((b, t, D), F32),
        jax.ShapeDtypeStruct((b, t, D), BF16),
        jax.ShapeDtypeStruct((b, QKW, t), BF16),
        jax.ShapeDtypeStruct((b, t, QKW), BF16),
        jax.ShapeDtypeStruct((b, QKW, t), BF16),
        jax.ShapeDtypeStruct((b, t, 256), F32),
        jax.ShapeDtypeStruct((b, 2, t, 256), F32),
        jax.ShapeDtypeStruct((b, t, 256), F32),
        jax.ShapeDtypeStruct((b, t, 256), F32),
        jax.ShapeDtypeStruct((b, t, 256), F32),
    )
    out_specs = (tok(D), tok(D), tokt, tok(QKW), tokt, tok(256),
                 pl.BlockSpec((1, 2, tm, 256), lambda bi, i: (bi, 0, i, 0)),
                 tok(256), tok(256), tok(256))
    in_specs = [tok(D),
                pl.BlockSpec((1, NMOD, D), lambda bi, i: (bi, 0, 0)),
                _const_spec((1, D)), _const_spec((1, D)),
                _layer_spec((D, DFF), layer), _layer_spec((D, DFF), layer), _layer_spec((DFF, D), layer),
                _layer_spec((D, IN_A), layer, 0),
                pl.BlockSpec((tm, 128), lambda bi, i: (i, 0)),
                pl.BlockSpec((tm, 128), lambda bi, i: (i, 0))]
    return pl.pallas_call(
        functools.partial(_ffn_in_kernel, rope=rope),
        grid=grid, in_specs=in_specs, out_specs=out_specs, out_shape=out_shape,
        compiler_params=pltpu.CompilerParams(
            dimension_semantics=("parallel", "parallel"), vmem_limit_bytes=VMEM_LIMIT),
        name="ffn_in",
    )(h, mod, n1, nm, w1, w3, w2, win, rc, rs)


def _attn_kernel(*refs, tq, tk, n_main, has_extra, lam_init):
    if has_extra:
        (qt_ref, k_ref, vt_ref, kx_ref, vtx_ref, lam_ref, gain_ref, o_ref,
         acc1, acc2, l_sc, kmax) = refs
    else:
        (qt_ref, k_ref, vt_ref, lam_ref, gain_ref, o_ref, acc1, acc2, l_sc, kmax) = refs
        kx_ref = vtx_ref = None
    qt = qt_ref[0]
    row = lax.broadcasted_iota(jnp.int32, qt.shape, 0)
    zero = jnp.zeros_like(qt)
    qbd = jnp.concatenate([jnp.where(row < HALF, qt, zero), jnp.where(row >= HALF, qt, zero)], axis=1)

    def rows(c):
        return pl.ds(c * tk if isinstance(c, int) else pl.multiple_of(c * tk, tk), tk)

    def kchunk(c):
        return k_ref[0, rows(c), :]

    def vchunk(c):
        return vt_ref[0, :, rows(c)]

    @pl.when(pl.program_id(2) == 0)
    def _():
        lane = lax.broadcasted_iota(jnp.int32, (16, VDIM), 1)
        r16 = lax.broadcasted_iota(jnp.int32, (16, VDIM), 0)
        sel = jnp.where((r16 == 0) & (lane < HALF) | (r16 == 1) & (lane >= HALF), 1.0, 0.0).astype(BF16)

        def sq_norms(kb):
            kf = kb.astype(F32)
            return _dot_nt(sel, (kf * kf).astype(BF16))

        n2 = lax.fori_loop(0, n_main, lambda c, mx: jnp.maximum(mx, sq_norms(kchunk(c))),
                           jnp.zeros((16, tk), F32))
        top = jnp.max(n2, axis=1, keepdims=True)
        if has_extra:
            top = jnp.maximum(top, jnp.max(sq_norms(kx_ref[0]), axis=1, keepdims=True))
        kmax[...] = jnp.broadcast_to(jnp.sqrt(top[0:8]), kmax.shape)

    qf = qt.astype(F32)
    q1 = jnp.sqrt(jnp.sum(jnp.where(row < HALF, qf * qf, 0.0), axis=0, keepdims=True))
    q2 = jnp.sqrt(jnp.sum(jnp.where(row >= HALF, qf * qf, 0.0), axis=0, keepdims=True))
    km = kmax[...]
    bound = jnp.concatenate([q1 * km[0:1, 0:1], q2 * km[1:2, 0:1]], axis=1) * BOUND_MARGIN
    acc1[...] = jnp.zeros_like(acc1)
    acc2[...] = jnp.zeros_like(acc2)

    def fast_step(kb, vtb, l):
        p = jnp.exp2(_dot(kb, qbd) - bound)
        pb = p.astype(BF16)
        acc1[...] += _dot(vtb, pb[:, :tq])
        acc2[...] += _dot(vtb, pb[:, tq:])
        return l + jnp.sum(p, axis=0, keepdims=True)

    l = jnp.zeros((1, 2 * tq), F32)
    for c in range(n_main):
        l = fast_step(kchunk(c), vchunk(c), l)
    if has_extra:
        l = fast_step(kx_ref[0], vtx_ref[0], l)
    l_sc[...] = jnp.broadcast_to(l, l_sc.shape)

    @pl.when(jnp.logical_not(jnp.min(l) >= L_FLOOR))
    def _():
        acc1[...] = jnp.zeros_like(acc1)
        acc2[...] = jnp.zeros_like(acc2)

        def safe_step(kb, vtb, carry):
            m, ls = carry
            s = _dot(kb, qbd)
            m_new = jnp.maximum(m, jnp.max(s, axis=0, keepdims=True))
            alpha = jnp.exp2(m - m_new)
            p = jnp.exp2(s - m_new)
            pb = p.astype(BF16)
            acc1[...] = acc1[...] * alpha[:, :tq] + _dot(vtb, pb[:, :tq])
            acc2[...] = acc2[...] * alpha[:, tq:] + _dot(vtb, pb[:, tq:])
            return m_new, alpha * ls + jnp.sum(p, axis=0, keepdims=True)

        carry = (jnp.full((1, 2 * tq), NEG, F32), jnp.zeros((1, 2 * tq), F32))
        carry = lax.fori_loop(0, n_main, lambda c, cr: safe_step(kchunk(c), vchunk(c), cr), carry)
        if has_extra:
            carry = safe_step(kx_ref[0], vtx_ref[0], carry)
        l_sc[...] = jnp.broadcast_to(carry[1], l_sc.shape)

    l = l_sc[0:1, :]

    lv = lam_ref[...]
    lam = (jnp.exp(jnp.sum(lv[0:1] * lv[1:2], axis=1, keepdims=True))
           - jnp.exp(jnp.sum(lv[2:3] * lv[3:4], axis=1, keepdims=True)) + lam_init)
    o = acc1[...] / l[:, :tq] - lam * (acc2[...] / l[:, tq:])
    o = o * lax.rsqrt(jnp.mean(o * o, axis=0, keepdims=True) + EPS) * gain_ref[...] * (1.0 - lam_init)
    o_ref[0] = o.T.astype(BF16)


def _attention(qt, k, vt, kx, vtx, lamv, gain, *, lam_init, tq, tk):
    b, _, t_q = qt.shape
    t_k = k.shape[1]
    has_extra = kx is not None
    assert t_k % tk == 0
    grid = (b, HEADS, t_q // tq)
    in_specs = [pl.BlockSpec((1, VDIM, tq), lambda bi, hi, i: (bi, hi, i)),
                pl.BlockSpec((1, t_k, VDIM), lambda bi, hi, i: (bi, 0, hi)),
                pl.BlockSpec((1, VDIM, t_k), lambda bi, hi, i: (bi, hi, 0))]
    args = [qt, k, vt]
    if has_extra:
        t_x = kx.shape[1]
        in_specs += [pl.BlockSpec((1, t_x, VDIM), lambda bi, hi, i: (bi, 0, hi)),
                     pl.BlockSpec((1, VDIM, t_x), lambda bi, hi, i: (bi, hi, 0))]
        args += [kx, vtx]
    in_specs += [pl.BlockSpec((4, HALF), lambda bi, hi, i: (0, 0)),
                 pl.BlockSpec((VDIM, 1), lambda bi, hi, i: (0, 0))]
    args += [lamv, gain]
    return pl.pallas_call(
        functools.partial(_attn_kernel, tq=tq, tk=tk, n_main=t_k // tk, has_extra=has_extra,
                          lam_init=lam_init),
        grid=grid, in_specs=in_specs,
        out_specs=pl.BlockSpec((1, tq, VDIM), lambda bi, hi, i: (bi, i, hi)),
        out_shape=jax.ShapeDtypeStruct((b, t_q, HEADS * VDIM), BF16),
        scratch_shapes=[pltpu.VMEM((VDIM, tq), F32), pltpu.VMEM((VDIM, tq), F32),
                        pltpu.VMEM((8, 2 * tq), F32), pltpu.VMEM((8, 128), F32)],
        compiler_params=pltpu.CompilerParams(
            dimension_semantics=("parallel", "parallel", "arbitrary"), vmem_limit_bytes=VMEM_LIMIT),
        name="diff_attn",
    )(*args)


def _chunk_cumsum(x, rev):
    n = x.shape[0]
    pos = lax.broadcasted_iota(jnp.int32, x.shape, 0) % CHUNK
    sh = 1
    while sh < CHUNK:
        if rev:
            x = x + jnp.where(pos + sh < CHUNK, pltpu.roll(x, n - sh, 0), 0.0)
        else:
            x = x + jnp.where(pos >= sh, pltpu.roll(x, sh, 0), 0.0)
        sh *= 2
    return x


def _hgrn_gates(zq, zf, lbv):
    q = zq * _sigmoid(zq)
    e = jnp.exp(-jnp.abs(zf))
    r = 1.0 / (1.0 + e)
    er = e * r
    pos = zf >= 0.0
    kk = (1.0 - lbv) * jnp.where(pos, er, r)
    lf = jnp.log(jnp.maximum(lbv, LB_MIN) + (1.0 - lbv) * jnp.where(pos, r, er))
    return q, kk, lf


def _hgrn_token_scan(zq_ref, zf_ref, zi_ref, lbv, st_ref, o_ref, *, rev, tb):
    head_mask = (lax.broadcasted_iota(jnp.int32, (HGW, HGW), 0) // HGH
                 == lax.broadcasted_iota(jnp.int32, (HGW, HGW), 1) // HGH)
    first = lax.broadcasted_iota(jnp.int32, (16, HGW), 0) == 0

    def pad16(r):
        return jnp.where(first, jnp.broadcast_to(r, (16, HGW)), 0.0).astype(BF16)

    def body(i, carry):
        t = (tb - 1 - i) if rev else i
        q, kk, lf = _hgrn_gates(zq_ref[pl.ds(t, 1), :], zf_ref[pl.ds(t, 1), :], lbv)
        st = st_ref[...] * jnp.exp(lf) + jnp.where(
            head_mask, _dot_tn(pad16(zi_ref[pl.ds(t, 1), :]), pad16(kk)), 0.0)
        st_ref[...] = st
        o_ref[pl.ds(t, 1), :] = _dot_nt(pad16(q), st.astype(BF16))[0:1]
        return carry

    lax.fori_loop(0, tb, body, 0)


def _hgrn_direction(zq_ref, zf_ref, zi_ref, lbv, st_ref, o_ref, stall, dst, *, rev, tb):
    nc = tb // CHUNK
    zi = zi_ref[...]
    q, kk, lf = _hgrn_gates(zq_ref[...], zf_ref[...], lbv)
    a = _chunk_cumsum(lf, rev)
    a3 = a.reshape(nc, CHUNK, HGW)
    end_row = 0 if rev else CHUNK - 1
    mid_row = CHUNK // 2 if rev else CHUNK // 2 - 1
    a_end = a3[:, end_row:end_row + 1, :]
    a_mid = a3[:, mid_row:mid_row + 1, :]
    in_range = jnp.max(jnp.abs(a3 - a_mid)) <= PAIR_RANGE
    q3 = q.reshape(nc, CHUNK, HGW)
    k3 = kk.reshape(nc, CHUNK, HGW)
    qe = (q3 * jnp.exp(a3)).astype(BF16)
    ke = (k3 * jnp.exp(a_end - a3)).astype(BF16)
    qm = (q3 * jnp.exp(a3 - a_mid)).reshape(tb, HGW)
    km = (k3 * jnp.exp(a_mid - a3)).reshape(tb, HGW).astype(BF16)
    vb = zi.astype(BF16)
    v3 = vb.reshape(nc, CHUNK, HGW)
    dec = jnp.exp(a_end)

    lane_head = lax.broadcasted_iota(jnp.int32, (1, HGW), 1) // HGH
    sub = 128
    r_i = lax.broadcasted_iota(jnp.int32, (sub, sub), 0)
    c_i = lax.broadcasted_iota(jnp.int32, (sub, sub), 1)
    same = (r_i // CHUNK) == (c_i // CHUNK)
    causal = jnp.logical_and(same, (c_i >= r_i) if rev else (c_i <= r_i))
    causal4 = jnp.concatenate([causal] * HEADS, axis=0)
    intra = []
    for g in range(tb // sub):
        sl = slice(g * sub, (g + 1) * sub)
        qg = qm[sl]
        qstack = jnp.concatenate(
            [jnp.where(lane_head == hh, qg, 0.0) for hh in range(HEADS)], axis=0).astype(BF16)
        sc = _dot_nt(qstack, km[sl])
        sc = jnp.where(causal4, sc, 0.0).astype(BF16)
        r = _dot(sc, vb[sl])
        og = jnp.zeros((sub, HGW), F32)
        for hh in range(HEADS):
            og = og + jnp.where(lane_head == hh, r[hh * sub:(hh + 1) * sub], 0.0)
        intra.append(og)
    o_intra = jnp.concatenate(intra, axis=0)

    head_mask = (lax.broadcasted_iota(jnp.int32, (HGW, HGW), 0) // HGH
                 == lax.broadcasted_iota(jnp.int32, (HGW, HGW), 1) // HGH)
    for c in range(nc):
        dst[c] = jnp.where(head_mask, _dot_tn(v3[c], ke[c]), 0.0)
    st = st_ref[...]
    order = range(nc - 1, -1, -1) if rev else range(nc)
    for c in order:
        stall[c] = st.astype(BF16)
        st = dec[c] * st + dst[c]
    inter = [_dot_nt(qe[c], stall[c]) for c in range(nc)]
    o_ref[...] = o_intra + jnp.concatenate(inter, axis=0)

    return st, in_range


def _hgrn_kernel(zqf_ref, zff_ref, zif_ref, zqb_ref, zfb_ref, zib_ref, lb_ref, s0_ref,
                 of_ref, ob_ref, s_ref, stall_f, dst_f, stall_b, dst_b, *, tb, layer):
    @pl.when(pl.program_id(1) == 0)
    def _():
        s_ref[...] = s0_ref[...]

    logits = lb_ref[...]
    e = jnp.exp(logits - jnp.max(logits, axis=0, keepdims=True))
    p = e / jnp.sum(e, axis=0, keepdims=True)
    lb = jnp.sum(p[0:layer + 1], axis=0) - p[0]
    fwd = (zqf_ref.at[0], zff_ref.at[0, 0], zif_ref.at[0], lb[0:1], s_ref.at[0, 0], of_ref.at[0])
    bwd = (zqb_ref.at[0], zfb_ref.at[0, 0], zib_ref.at[0], lb[1:2], s_ref.at[0, 1], ob_ref.at[0])
    st_f, ok_f = _hgrn_direction(*fwd, stall_f, dst_f, rev=False, tb=tb)
    st_b, ok_b = _hgrn_direction(*bwd, stall_b, dst_b, rev=True, tb=tb)
    in_range = jnp.logical_and(ok_f, ok_b)

    @pl.when(in_range)
    def _():
        s_ref[0, 0] = st_f
        s_ref[0, 1] = st_b

    @pl.when(jnp.logical_not(in_range))
    def _():
        _hgrn_token_scan(*fwd, rev=False, tb=tb)
        _hgrn_token_scan(*bwd, rev=True, tb=tb)


def _hgrn(zq, zf, zi, lb_logits, s0, *, layer, tb):
    b, t, _ = zq.shape
    nb = t // tb
    fwd = lambda bi, i: (bi, i, 0)
    bwd = lambda bi, i: (bi, nb - 1 - i, 0)
    in_specs = [pl.BlockSpec((1, tb, HGW), fwd),
                pl.BlockSpec((1, 1, tb, HGW), lambda bi, i: (bi, 0, i, 0)),
                pl.BlockSpec((1, tb, HGW), fwd),
                pl.BlockSpec((1, tb, HGW), bwd),
                pl.BlockSpec((1, 1, tb, HGW), lambda bi, i: (bi, 1, nb - 1 - i, 0)),
                pl.BlockSpec((1, tb, HGW), bwd),
                pl.BlockSpec(lb_logits.shape, lambda bi, i: (0, 0, 0)),
                pl.BlockSpec((1, 2, HGW, HGW), lambda bi, i: (bi, 0, 0, 0))]
    out_specs = (pl.BlockSpec((1, tb, HGW), fwd),
                 pl.BlockSpec((1, tb, HGW), bwd),
                 pl.BlockSpec((1, 2, HGW, HGW), lambda bi, i: (bi, 0, 0, 0)))
    out_shape = (jax.ShapeDtypeStruct((b, t, HGW), F32),
                 jax.ShapeDtypeStruct((b, t, HGW), F32),
                 jax.ShapeDtypeStruct((b, 2, HGW, HGW), F32))
    nc = tb // CHUNK
    of, ob, s_fin = pl.pallas_call(
        functools.partial(_hgrn_kernel, tb=tb, layer=layer),
        grid=(b, nb), in_specs=in_specs, out_specs=out_specs, out_shape=out_shape,
        scratch_shapes=[pltpu.VMEM((nc, HGW, HGW), BF16), pltpu.VMEM((nc, HGW, HGW), F32),
                        pltpu.VMEM((nc, HGW, HGW), BF16), pltpu.VMEM((nc, HGW, HGW), F32)],
        compiler_params=pltpu.CompilerParams(
            dimension_semantics=("parallel", "arbitrary"), vmem_limit_bytes=VMEM_LIMIT),
        name="hgrn",
    )(zq, zf, zi, zq, zf, zi, lb_logits, s0)
    return of, ob, s_fin


def _merge_kernel(h_ref, u_ref, mod_ref, n2_ref, oda_ref, of_ref, ob_ref, hg_ref,
                  zp_ref, zpp_ref, zpn_ref, hgn_ref, pw_ref, ps_ref,
                  wg_ref, wpa_ref, wph_ref, wpp_ref, wo_ref, w1_ref, w3_ref, w2_ref, fn_ref,
                  out_ref, *, tm, t_total, final):
    i = pl.program_id(1)
    h1 = h_ref[0]
    m = mod_ref[0]

    x = zp_ref[0]
    prev = jnp.where(i > 0, zpp_ref[0], 0.0)
    nxt = jnp.where(i < pl.num_programs(1) - 1, zpn_ref[0], 0.0)
    e = jnp.concatenate([prev, x, nxt], axis=0)
    n = tm + 16
    a2 = e[0:n - 1] + e[1:n]
    a4 = a2[0:n - 3] + a2[2:n - 1]
    a8 = a4[0:n - 7] + a4[4:n - 3]
    a16 = a8[0:n - 15] + a8[8:n - 7]
    sums = (a2[7:7 + tm], a4[6:6 + tm], a8[4:4 + tm], a16[0:tm])
    group = lax.broadcasted_iota(jnp.int32, (1, PW), 1) // 64
    total, half = sums[-1], jnp.full((1, PW), POOL_WINDOWS[-1] // 2, jnp.int32)
    for g in range(len(POOL_WINDOWS) - 2, -1, -1):
        total = jnp.where(group == g, sums[g], total)
        half = jnp.where(group == g, POOL_WINDOWS[g] // 2, half)
    mean = total * (0.5 / half.astype(F32))

    def clipped(rows, r0):
        pos = i * tm + r0 + lax.broadcasted_iota(jnp.int32, (8, PW), 0)
        cnt = jnp.minimum(pos + half, t_total) - jnp.maximum(pos - half, 0)
        return rows / cnt.astype(F32)

    mean = jnp.concatenate([clipped(total[0:8], 0), mean[8:tm - 8], clipped(total[tm - 8:tm], tm - 8)], axis=0)
    o_pool = _dot((mean - x).astype(BF16), pw_ref[...]) * ps_ref[...]

    o = of_ref[0] + ob_ref[0]
    o2 = o * o
    hi = o2.astype(BF16)
    lo = (o2 - hi.astype(F32)).astype(BF16)
    ones_bd = (lax.broadcasted_iota(jnp.int32, (HGW, HGW), 0) // HGH
               == lax.broadcasted_iota(jnp.int32, (HGW, HGW), 1) // HGH).astype(BF16)
    seg = _dot(hi, ones_bd) + _dot(lo, ones_bd)
    zg = hg_ref[0]
    o_hg = (o * lax.rsqrt(seg * (1.0 / HGH) + EPS) * hgn_ref[...]) * (zg * _sigmoid(zg))

    gate = _sigmoid(_dot(u_ref[0], wg_ref[0]))
    y = (gate[:, 0:D] * _dot(oda_ref[0], wpa_ref[0])
         + gate[:, D:2 * D] * _dot(o_hg.astype(BF16), wph_ref[0])
         + gate[:, 2 * D:3 * D] * _dot(o_pool.astype(BF16), wpp_ref[0]))
    mix = _dot_row_groups(y.astype(BF16), wo_ref[0])
    h2 = h1 + m[5:6] * mix
    h3 = _swiglu_half(h2, m, 6, n2_ref[...], w1_ref, w3_ref, w2_ref)
    if final:
        h3 = _rms(h3, fn_ref[...])
    out_ref[0] = h3


def _merge(h1, u, mod, n2, oda, of, ob, hg, zp, hgn, pw, ps, wg, wpa, wph, wpp, wo, w1, w3, w2, fn,
           *, layer, tm, final):
    b, t, _ = h1.shape
    nblk8 = t // 8
    r = tm // 8
    tok = lambda w: pl.BlockSpec((1, tm, w), lambda bi, i: (bi, i, 0))
    in_specs = [tok(D), tok(D),
                pl.BlockSpec((1, NMOD, D), lambda bi, i: (bi, 0, 0)),
                _const_spec((1, D)),
                tok(QKW),
                tok(HGW), tok(HGW),
                tok(HGW), tok(PW),
                pl.BlockSpec((1, 8, PW), lambda bi, i: (bi, jnp.maximum(i * r - 1, 0), 0)),
                pl.BlockSpec((1, 8, PW), lambda bi, i: (bi, jnp.minimum((i + 1) * r, nblk8 - 1), 0)),
                _const_spec((1, HGW)), _const_spec((PW, PW)), _const_spec((1, PW)),
                _layer_spec((D, 3 * D), layer, 1), _layer_spec((QKW, D), layer), _layer_spec((HGW, D), layer),
                _layer_spec((PW, D), layer), _layer_spec((D, D), layer),
                _layer_spec((D, DFF), layer), _layer_spec((D, DFF), layer), _layer_spec((DFF, D), layer),
                _const_spec((1, D))]
    return pl.pallas_call(
        functools.partial(_merge_kernel, tm=tm, t_total=t, final=final),
        grid=(b, t // tm), in_specs=in_specs, out_specs=tok(D),
        out_shape=jax.ShapeDtypeStruct((b, t, D), F32),
        compiler_params=pltpu.CompilerParams(
            dimension_semantics=("parallel", "parallel"), vmem_limit_bytes=VMEM_LIMIT),
        name="merge_ffn",
    )(h1, u, mod, n2, oda, of, ob, hg, zp, zp, zp, hgn, pw, ps, wg, wpa, wph, wpp, wo, w1, w3, w2, fn)


def _rope_tables(n):
    rows = n // GRID_WIDTH
    row = jnp.repeat(jnp.arange(rows, dtype=jnp.int32), GRID_WIDTH).astype(F32)
    col = jnp.tile(jnp.arange(GRID_WIDTH, dtype=jnp.int32), rows).astype(F32)
    axis_dim = HALF // 2
    inv_freq = THETA ** (-jnp.arange(0, axis_dim, 2, dtype=F32) / axis_dim)
    ang_r = row[:, None] * inv_freq[None, :]
    ang_c = col[:, None] * inv_freq[None, :]
    cos64 = jnp.concatenate([jnp.cos(ang_r)] * 2 + [jnp.cos(ang_c)] * 2, axis=1)
    sin64 = jnp.concatenate([-jnp.sin(ang_r), jnp.sin(ang_r), -jnp.sin(ang_c), jnp.sin(ang_c)], axis=1)
    return jnp.tile(cos64, (1, 2)), jnp.tile(sin64, (1, 2))


def _block_diag(w):
    g, a, b = w.shape
    out = jnp.zeros((g * a, g * b), w.dtype)
    for i in range(g):
        out = out.at[i * a:(i + 1) * a, i * b:(i + 1) * b].set(w[i])
    return out


def kernel(x, c, ctx, c_ctx, w_ada, b_ada, norm_ffn1, norm_mix, norm_ffn2, ffn1_w1, ffn1_w3, ffn1_w2,
           ffn2_w1, ffn2_w3, ffn2_w2, w_in, da_lambda_q1, da_lambda_k1, da_lambda_q2, da_lambda_k2,
           da_subln, hg_lb_logits, hg_norm, pool_w, pool_scale, w_proj_da, w_proj_hg, w_proj_pool,
           w_out, final_norm):
    bsz, n, _ = x.shape
    n_ctx = ctx.shape[1]
    depth = w_ada.shape[0]
    lat, cx = _tiles(n), _tiles(n_ctx)

    cvec = jnp.zeros((8, D), F32).at[0:bsz].set(c).at[bsz].set(c_ctx)
    mods = _ada(cvec, w_ada, b_ada).reshape(depth, 8, NMOD, D)
    rc, rs = _rope_tables(n)
    rc_c = jnp.zeros((n_ctx, 128), F32)
    row = lambda v: v.reshape(1, -1)
    bf = lambda w: w.astype(BF16)

    ffn1 = (bf(ffn1_w1), bf(ffn1_w3), bf(ffn1_w2))
    ffn2 = (bf(ffn2_w1), bf(ffn2_w3), bf(ffn2_w2))
    w_in_b = bf(w_in)
    proj = (bf(w_proj_da), bf(w_proj_hg), bf(w_proj_pool), bf(w_out))

    h, hc = x, ctx
    for l in range(depth):
        need_ctx = l < depth - 1
        lam_init = 0.8 - 0.6 * math.exp(-0.3 * l)
        m_lat = mods[l, 0:bsz]
        m_ctx = jnp.broadcast_to(mods[l, bsz][None], (bsz, NMOD, D))
        lamv = jnp.stack([da_lambda_q1[l], da_lambda_k1[l], da_lambda_q2[l], da_lambda_k2[l]]).astype(F32)
        gain_da = da_subln[l].reshape(VDIM, 1)
        hgn = row(jnp.tile(hg_norm[l], HEADS))
        pw_bd = bf(_block_diag(pool_w[l]))
        merge_w = (hgn, pw_bd, row(pool_scale[l]), w_in_b, *proj, *ffn2, row(final_norm))

        (hc1, u_c, qt_c, k_c, vt_c, hq_c, hf_c, hi_c, hg_c, zp_c) = _ffn_in(
            hc, m_ctx, row(norm_ffn1[l]), row(norm_mix[l]), *ffn1, w_in_b, rc_c, rc_c,
            layer=l, rope=False, tm=cx.tm)
        s0 = jnp.zeros((bsz, 2, HGW, HGW), F32)
        of_c, ob_c, s_ctx = _hgrn(hq_c, hf_c, hi_c, hg_lb_logits, s0, layer=l, tb=cx.tb)

        (h1, u, qt, k, vt, hq, hf, hi, hg, zp) = _ffn_in(
            h, m_lat, row(norm_ffn1[l]), row(norm_mix[l]), *ffn1, w_in_b, rc, rs,
            layer=l, rope=True, tm=lat.tm)
        o_da = _attention(qt, k, vt, k_c, vt_c, lamv, gain_da, lam_init=lam_init, tq=lat.tq, tk=lat.tk)
        of, ob, _ = _hgrn(hq, hf, hi, hg_lb_logits, s_ctx, layer=l, tb=lat.tb)
        h = _merge(h1, u, m_lat, row(norm_ffn2[l]), o_da, of, ob, hg, zp, *merge_w,
                   layer=l, tm=lat.tm, final=not need_ctx)
        if need_ctx:
            o_dac = _attention(qt_c, k_c, vt_c, None, None, lamv, gain_da, lam_init=lam_init,
                               tq=cx.tq, tk=cx.tk)
            hc = _merge(hc1, u_c, m_ctx, row(norm_ffn2[l]), o_dac, of_c, ob_c, hg_c, zp_c,
                        *merge_w, layer=l, tm=cx.tm, final=False)
    return h
```

```python
import functools
import math
from typing import NamedTuple

import jax
import jax.numpy as jnp
from jax import lax
from jax.experimental import pallas as pl
from jax.experimental.pallas import tpu as pltpu

D = 1024
GRID_WIDTH = 64
EPS = 1e-6
LB_MIN = 1e-20
NMOD = 9
HEADS = 4
HALF = 64
VDIM = 128
QKW = HEADS * 2 * HALF
HGW = 256
HGH = 64
PW = 256
POOL_WINDOWS = (2, 4, 8, 16)
DFF = 2816
THETA = 10000.0
IN_A = 3 * QKW + 6 * 256
CHUNK = 64
PAIR_RANGE = 80.0
NEG = -1e30
LOG2E = 1.4426950408889634
BOUND_MARGIN = 1.01
L_FLOOR = 2.0 ** -86

F32 = jnp.float32
BF16 = jnp.bfloat16
VMEM_LIMIT = 56 * 1024 * 1024

TOKEN_TILE = 512
QUERY_TILE = 1024
KEY_CHUNK = 2048
SCAN_BLOCK = 1024


class _Tiles(NamedTuple):
    tm: int
    tq: int
    tk: int
    tb: int


def _tiles(n):
    return _Tiles(min(TOKEN_TILE, n), min(QUERY_TILE, n), min(KEY_CHUNK, n), min(SCAN_BLOCK, n))


def _const_spec(shape):
    nd = len(shape)
    return pl.BlockSpec(shape, lambda *_: (0,) * nd, pipeline_mode=pl.Buffered(1))


def _layer_spec(shape, layer, col=0):
    return pl.BlockSpec((1,) + shape, lambda *_: (layer, 0, col), pipeline_mode=pl.Buffered(1))


def _rms(x, gain):
    return x * lax.rsqrt(jnp.mean(x * x, axis=-1, keepdims=True) + EPS) * gain


def _sigmoid(x):
    return 1.0 / (1.0 + jnp.exp(-x))


def _dot(a, b):
    return jnp.dot(a, b, preferred_element_type=F32)


def _dot_row_groups(a, b):
    g = 256
    if a.shape[0] <= g:
        return _dot(a, b)
    return jnp.concatenate([_dot(a[r:r + g], b) for r in range(0, a.shape[0], g)], axis=0)


def _dot_nt(a, b):
    return lax.dot_general(a, b, (((1,), (1,)), ((), ())), preferred_element_type=F32)


def _dot_tn(a, b):
    return lax.dot_general(a, b, (((0,), (0,)), ((), ())), preferred_element_type=F32)


def _ada_kernel(c_ref, w_ref, b_ref, o_ref):
    c = c_ref[...]
    s = c * _sigmoid(c)
    o_ref[0] = jnp.dot(s, w_ref[0], preferred_element_type=F32,
                       precision=lax.Precision.HIGHEST) + b_ref[0]


def _ada(cvec, w_ada, b_ada):
    depth = w_ada.shape[0]
    nblk = (NMOD * D) // D
    return pl.pallas_call(
        _ada_kernel,
        grid=(depth, nblk),
        in_specs=[pl.BlockSpec((8, D), lambda l, j: (0, 0)),
                  pl.BlockSpec((1, D, D), lambda l, j: (l, 0, j)),
                  pl.BlockSpec((1, 1, D), lambda l, j: (l, 0, j))],
        out_specs=pl.BlockSpec((1, 8, D), lambda l, j: (l, 0, j)),
        out_shape=jax.ShapeDtypeStruct((depth, 8, NMOD * D), F32),
        name="ada",
    )(cvec, w_ada, b_ada.reshape(depth, 1, NMOD * D))


def _swiglu_half(h, m, shift_i, gain, w1_ref, w3_ref, w2_ref):
    x = _rms(h, gain) * (1.0 + m[shift_i + 1:shift_i + 2]) + m[shift_i:shift_i + 1]
    xb = x.astype(BF16)
    a = _dot(xb, w1_ref[0])
    b = _dot(xb, w3_ref[0])
    g = (a * _sigmoid(a) * b).astype(BF16)
    y = _dot_row_groups(g, w2_ref[0])
    return h + 0.5 * m[shift_i + 2:shift_i + 3] * y


def _ffn_in_kernel(h_ref, mod_ref, n1_ref, nm_ref, w1_ref, w3_ref, w2_ref, win_ref, rc_ref, rs_ref,
                   h1_ref, u_ref, qt_ref, k_ref, vt_ref, hq_ref, hf_ref, hi_ref, hg_ref, zp_ref, *, rope):
    h = h_ref[0]
    m = mod_ref[0]
    h1 = _swiglu_half(h, m, 0, n1_ref[...], w1_ref, w3_ref, w2_ref)
    h1_ref[0] = h1
    u = (_rms(h1, nm_ref[...]) * (1.0 + m[4:5]) + m[3:4]).astype(BF16)
    u_ref[0] = u
    z = _dot(u, win_ref[0])
    q = z[:, 0:QKW]
    k = z[:, QKW:2 * QKW]
    v = z[:, 2 * QKW:3 * QKW]
    if rope:
        lane = lax.broadcasted_iota(jnp.int32, q.shape, 1)
        first = (lane % 32) < 16
        rc = jnp.concatenate([rc_ref[...]] * (QKW // 128), axis=1)
        rs = jnp.concatenate([rs_ref[...]] * (QKW // 128), axis=1)

        def rot(x):
            partner = jnp.where(first, pltpu.roll(x, QKW - 16, 1), pltpu.roll(x, 16, 1))
            return x * rc + partner * rs

        q = rot(q)
        k = rot(k)
    q = q * (HALF ** -0.5 * LOG2E)
    qt_ref[0] = q.T.astype(BF16)
    k_ref[0] = k.astype(BF16)
    vt_ref[0] = v.T.astype(BF16)
    o = 3 * QKW
    hq_ref[0] = z[:, o:o + 256]
    hf_ref[0, 0] = z[:, o + 256:o + 512]
    hf_ref[0, 1] = z[:, o + 512:o + 768]
    hi_ref[0] = z[:, o + 768:o + 1024]
    hg_ref[0] = z[:, o + 1024:o + 1280]
    zp_ref[0] = z[:, o + 1280:o + 1536]


def _ffn_in(h, mod, n1, nm, w1, w3, w2, win, rc, rs, *, layer, rope, tm):
    b, t, _ = h.shape
    grid = (b, t // tm)
    tok = lambda w: pl.BlockSpec((1, tm, w), lambda bi, i: (bi, i, 0))
    tokt = pl.BlockSpec((1, QKW, tm), lambda bi, i: (bi, 0, i))
    out_shape = (
        jax.ShapeDtypeStruct((b, t, D), F32),
        jax.ShapeDtypeStruct((b, t, D), BF16),
        jax.ShapeDtypeStruct((b, QKW, t), BF16),
        jax.ShapeDtypeStruct((b, t, QKW), BF16),
        jax.ShapeDtypeStruct((b, QKW, t), BF16),
        jax.ShapeDtypeStruct((b, t, 256), F32),
        jax.ShapeDtypeStruct((b, 2, t, 256), F32),
        jax.ShapeDtypeStruct((b, t, 256), F32),
        jax.ShapeDtypeStruct((b, t, 256), F32),
        jax.ShapeDtypeStruct((b, t, 256), F32),
    )
    out_specs = (tok(D), tok(D), tokt, tok(QKW), tokt, tok(256),
                 pl.BlockSpec((1, 2, tm, 256), lambda bi, i: (bi, 0, i, 0)),
                 tok(256), tok(256), tok(256))
    in_specs = [tok(D),
                pl.BlockSpec((1, NMOD, D), lambda bi, i: (bi, 0, 0)),
                _const_spec((1, D)), _const_spec((1, D)),
                _layer_spec((D, DFF), layer), _layer_spec((D, DFF), layer), _layer_spec((DFF, D), layer),
                _layer_spec((D, IN_A), layer, 0),
                pl.BlockSpec((tm, 128), lambda bi, i: (i, 0)),
                pl.BlockSpec((tm, 128), lambda bi, i: (i, 0))]
    return pl.pallas_call(
        functools.partial(_ffn_in_kernel, rope=rope),
        grid=grid, in_specs=in_specs, out_specs=out_specs, out_shape=out_shape,
        compiler_params=pltpu.CompilerParams(
            dimension_semantics=("parallel", "parallel"), vmem_limit_bytes=VMEM_LIMIT),
        name="ffn_in",
    )(h, mod, n1, nm, w1, w3, w2, win, rc, rs)


def _attn_kernel(*refs, tq, tk, n_main, has_extra, lam_init):
    if has_extra:
        (qt_ref, k_ref, vt_ref, kx_ref, vtx_ref, lam_ref, gain_ref, o_ref,
         acc1, acc2, l_sc, kmax) = refs
    else:
        (qt_ref, k_ref, vt_ref, lam_ref, gain_ref, o_ref, acc1, acc2, l_sc, kmax) = refs
        kx_ref = vtx_ref = None
    qt = qt_ref[0]
    row = lax.broadcasted_iota(jnp.int32, qt.shape, 0)
    zero = jnp.zeros_like(qt)
    qbd = jnp.concatenate([jnp.where(row < HALF, qt, zero), jnp.where(row >= HALF, qt, zero)], axis=1)

    def rows(c):
        return pl.ds(c * tk if isinstance(c, int) else pl.multiple_of(c * tk, tk), tk)

    def kchunk(c):
        return k_ref[0, rows(c), :]

    def vchunk(c):
        return vt_ref[0, :, rows(c)]

    @pl.when(pl.program_id(2) == 0)
    def _():
        lane = lax.broadcasted_iota(jnp.int32, (16, VDIM), 1)
        r16 = lax.broadcasted_iota(jnp.int32, (16, VDIM), 0)
        sel = jnp.where((r16 == 0) & (lane < HALF) | (r16 == 1) & (lane >= HALF), 1.0, 0.0).astype(BF16)

        def sq_norms(kb):
            kf = kb.astype(F32)
            return _dot_nt(sel, (kf * kf).astype(BF16))

        n2 = lax.fori_loop(0, n_main, lambda c, mx: jnp.maximum(mx, sq_norms(kchunk(c))),
                           jnp.zeros((16, tk), F32))
        top = jnp.max(n2, axis=1, keepdims=True)
        if has_extra:
            top = jnp.maximum(top, jnp.max(sq_norms(kx_ref[0]), axis=1, keepdims=True))
        kmax[...] = jnp.broadcast_to(jnp.sqrt(top[0:8]), kmax.shape)

    qf = qt.astype(F32)
    q1 = jnp.sqrt(jnp.sum(jnp.where(row < HALF, qf * qf, 0.0), axis=0, keepdims=True))
    q2 = jnp.sqrt(jnp.sum(jnp.where(row >= HALF, qf * qf, 0.0), axis=0, keepdims=True))
    km = kmax[...]
    bound = jnp.concatenate([q1 * km[0:1, 0:1], q2 * km[1:2, 0:1]], axis=1) * BOUND_MARGIN
    acc1[...] = jnp.zeros_like(acc1)
    acc2[...] = jnp.zeros_like(acc2)

    def fast_step(kb, vtb, l):
        p = jnp.exp2(_dot(kb, qbd) - bound)
        pb = p.astype(BF16)
        acc1[...] += _dot(vtb, pb[:, :tq])
        acc2[...] += _dot(vtb, pb[:, tq:])
        return l + jnp.sum(p, axis=0, keepdims=True)

    l = jnp.zeros((1, 2 * tq), F32)
    for c in range(n_main):
        l = fast_step(kchunk(c), vchunk(c), l)
    if has_extra:
        l = fast_step(kx_ref[0], vtx_ref[0], l)
    l_sc[...] = jnp.broadcast_to(l, l_sc.shape)

    @pl.when(jnp.logical_not(jnp.min(l) >= L_FLOOR))
    def _():
        acc1[...] = jnp.zeros_like(acc1)
        acc2[...] = jnp.zeros_like(acc2)

        def safe_step(kb, vtb, carry):
            m, ls = carry
            s = _dot(kb, qbd)
            m_new = jnp.maximum(m, jnp.max(s, axis=0, keepdims=True))
            alpha = jnp.exp2(m - m_new)
            p = jnp.exp2(s - m_new)
            pb = p.astype(BF16)
            acc1[...] = acc1[...] * alpha[:, :tq] + _dot(vtb, pb[:, :tq])
            acc2[...] = acc2[...] * alpha[:, tq:] + _dot(vtb, pb[:, tq:])
            return m_new, alpha * ls + jnp.sum(p, axis=0, keepdims=True)

        carry = (jnp.full((1, 2 * tq), NEG, F32), jnp.zeros((1, 2 * tq), F32))
        carry = lax.fori_loop(0, n_main, lambda c, cr: safe_step(kchunk(c), vchunk(c), cr), carry)
        if has_extra:
            carry = safe_step(kx_ref[0], vtx_ref[0], carry)
        l_sc[...] = jnp.broadcast_to(carry[1], l_sc.shape)

    l = l_sc[0:1, :]

    lv = lam_ref[...]
    lam = (jnp.exp(jnp.sum(lv[0:1] * lv[1:2], axis=1, keepdims=True))
           - jnp.exp(jnp.sum(lv[2:3] * lv[3:4], axis=1, keepdims=True)) + lam_init)
    o = acc1[...] / l[:, :tq] - lam * (acc2[...] / l[:, tq:])
    o = o * lax.rsqrt(jnp.mean(o * o, axis=0, keepdims=True) + EPS) * gain_ref[...] * (1.0 - lam_init)
    o_ref[0] = o.T.astype(BF16)


def _attention(qt, k, vt, kx, vtx, lamv, gain, *, lam_init, tq, tk):
    b, _, t_q = qt.shape
    t_k = k.shape[1]
    has_extra = kx is not None
    assert t_k % tk == 0
    grid = (b, HEADS, t_q // tq)
    in_specs = [pl.BlockSpec((1, VDIM, tq), lambda bi, hi, i: (bi, hi, i)),
                pl.BlockSpec((1, t_k, VDIM), lambda bi, hi, i: (bi, 0, hi)),
                pl.BlockSpec((1, VDIM, t_k), lambda bi, hi, i: (bi, hi, 0))]
    args = [qt, k, vt]
    if has_extra:
        t_x = kx.shape[1]
        in_specs += [pl.BlockSpec((1, t_x, VDIM), lambda bi, hi, i: (bi, 0, hi)),
                     pl.BlockSpec((1, VDIM, t_x), lambda bi, hi, i: (bi, hi, 0))]
        args += [kx, vtx]
    in_specs += [pl.BlockSpec((4, HALF), lambda bi, hi, i: (0, 0)),
                 pl.BlockSpec((VDIM, 1), lambda bi, hi, i: (0, 0))]
    args += [lamv, gain]
    return pl.pallas_call(
        functools.partial(_attn_kernel, tq=tq, tk=tk, n_main=t_k // tk, has_extra=has_extra,
                          lam_init=lam_init),
        grid=grid, in_specs=in_specs,
        out_specs=pl.BlockSpec((1, tq, VDIM), lambda bi, hi, i: (bi, i, hi)),
        out_shape=jax.ShapeDtypeStruct((b, t_q, HEADS * VDIM), BF16),
        scratch_shapes=[pltpu.VMEM((VDIM, tq), F32), pltpu.VMEM((VDIM, tq), F32),
                        pltpu.VMEM((8, 2 * tq), F32), pltpu.VMEM((8, 128), F32)],
        compiler_params=pltpu.CompilerParams(
            dimension_semantics=("parallel", "parallel", "arbitrary"), vmem_limit_bytes=VMEM_LIMIT),
        name="diff_attn",
    )(*args)


def _chunk_cumsum(x, rev):
    n = x.shape[0]
    pos = lax.broadcasted_iota(jnp.int32, x.shape, 0) % CHUNK
    sh = 1
    while sh < CHUNK:
        if rev:
            x = x + jnp.where(pos + sh < CHUNK, pltpu.roll(x, n - sh, 0), 0.0)
        else:
            x = x + jnp.where(pos >= sh, pltpu.roll(x, sh, 0), 0.0)
        sh *= 2
    return x


def _hgrn_gates(zq, zf, lbv):
    q = zq * _sigmoid(zq)
    e = jnp.exp(-jnp.abs(zf))
    r = 1.0 / (1.0 + e)
    er = e * r
    pos = zf >= 0.0
    kk = (1.0 - lbv) * jnp.where(pos, er, r)
    lf = jnp.log(jnp.maximum(lbv, LB_MIN) + (1.0 - lbv) * jnp.where(pos, r, er))
    return q, kk, lf


def _hgrn_token_scan(zq_ref, zf_ref, zi_ref, lbv, st_ref, o_ref, *, rev, tb):
    head_mask = (lax.broadcasted_iota(jnp.int32, (HGW, HGW), 0) // HGH
                 == lax.broadcasted_iota(jnp.int32, (HGW, HGW), 1) // HGH)
    first = lax.broadcasted_iota(jnp.int32, (16, HGW), 0) == 0

    def pad16(r):
        return jnp.where(first, jnp.broadcast_to(r, (16, HGW)), 0.0).astype(BF16)

    def body(i, carry):
        t = (tb - 1 - i) if rev else i
        q, kk, lf = _hgrn_gates(zq_ref[pl.ds(t, 1), :], zf_ref[pl.ds(t, 1), :], lbv)
        st = st_ref[...] * jnp.exp(lf) + jnp.where(
            head_mask, _dot_tn(pad16(zi_ref[pl.ds(t, 1), :]), pad16(kk)), 0.0)
        st_ref[...] = st
        o_ref[pl.ds(t, 1), :] = _dot_nt(pad16(q), st.astype(BF16))[0:1]
        return carry

    lax.fori_loop(0, tb, body, 0)


def _hgrn_direction(zq_ref, zf_ref, zi_ref, lbv, st_ref, o_ref, stall, dst, *, rev, tb):
    nc = tb // CHUNK
    zi = zi_ref[...]
    q, kk, lf = _hgrn_gates(zq_ref[...], zf_ref[...], lbv)
    a = _chunk_cumsum(lf, rev)
    a3 = a.reshape(nc, CHUNK, HGW)
    end_row = 0 if rev else CHUNK - 1
    mid_row = CHUNK // 2 if rev else CHUNK // 2 - 1
    a_end = a3[:, end_row:end_row + 1, :]
    a_mid = a3[:, mid_row:mid_row + 1, :]
    in_range = jnp.max(jnp.abs(a3 - a_mid)) <= PAIR_RANGE
    q3 = q.reshape(nc, CHUNK, HGW)
    k3 = kk.reshape(nc, CHUNK, HGW)
    qe = (q3 * jnp.exp(a3)).astype(BF16)
    ke = (k3 * jnp.exp(a_end - a3)).astype(BF16)
    qm = (q3 * jnp.exp(a3 - a_mid)).reshape(tb, HGW)
    km = (k3 * jnp.exp(a_mid - a3)).reshape(tb, HGW).astype(BF16)
    vb = zi.astype(BF16)
    v3 = vb.reshape(nc, CHUNK, HGW)
    dec = jnp.exp(a_end)

    lane_head = lax.broadcasted_iota(jnp.int32, (1, HGW), 1) // HGH
    sub = 128
    r_i = lax.broadcasted_iota(jnp.int32, (sub, sub), 0)
    c_i = lax.broadcasted_iota(jnp.int32, (sub, sub), 1)
    same = (r_i // CHUNK) == (c_i // CHUNK)
    causal = jnp.logical_and(same, (c_i >= r_i) if rev else (c_i <= r_i))
    causal4 = jnp.concatenate([causal] * HEADS, axis=0)
    intra = []
    for g in range(tb // sub):
        sl = slice(g * sub, (g + 1) * sub)
        qg = qm[sl]
        qstack = jnp.concatenate(
            [jnp.where(lane_head == hh, qg, 0.0) for hh in range(HEADS)], axis=0).astype(BF16)
        sc = _dot_nt(qstack, km[sl])
        sc = jnp.where(causal4, sc, 0.0).astype(BF16)
        r = _dot(sc, vb[sl])
        og = jnp.zeros((sub, HGW), F32)
        for hh in range(HEADS):
            og = og + jnp.where(lane_head == hh, r[hh * sub:(hh + 1) * sub], 0.0)
        intra.append(og)
    o_intra = jnp.concatenate(intra, axis=0)

    head_mask = (lax.broadcasted_iota(jnp.int32, (HGW, HGW), 0) // HGH
                 == lax.broadcasted_iota(jnp.int32, (HGW, HGW), 1) // HGH)
    for c in range(nc):
        dst[c] = jnp.where(head_mask, _dot_tn(v3[c], ke[c]), 0.0)
    st = st_ref[...]
    order = range(nc - 1, -1, -1) if rev else range(nc)
    for c in order:
        stall[c] = st.astype(BF16)
        st = dec[c] * st + dst[c]
    inter = [_dot_nt(qe[c], stall[c]) for c in range(nc)]
    o_ref[...] = o_intra + jnp.concatenate(inter, axis=0)

    return st, in_range


def _hgrn_kernel(zqf_ref, zff_ref, zif_ref, zqb_ref, zfb_ref, zib_ref, lb_ref, s0_ref,
                 of_ref, ob_ref, s_ref, stall_f, dst_f, stall_b, dst_b, *, tb, layer):
    @pl.when(pl.program_id(1) == 0)
    def _():
        s_ref[...] = s0_ref[...]

    logits = lb_ref[...]
    e = jnp.exp(logits - jnp.max(logits, axis=0, keepdims=True))
    p = e / jnp.sum(e, axis=0, keepdims=True)
    lb = jnp.sum(p[0:layer + 1], axis=0) - p[0]
    fwd = (zqf_ref.at[0], zff_ref.at[0, 0], zif_ref.at[0], lb[0:1], s_ref.at[0, 0], of_ref.at[0])
    bwd = (zqb_ref.at[0], zfb_ref.at[0, 0], zib_ref.at[0], lb[1:2], s_ref.at[0, 1], ob_ref.at[0])
    st_f, ok_f = _hgrn_direction(*fwd, stall_f, dst_f, rev=False, tb=tb)
    st_b, ok_b = _hgrn_direction(*bwd, stall_b, dst_b, rev=True, tb=tb)
    in_range = jnp.logical_and(ok_f, ok_b)

    @pl.when(in_range)
    def _():
        s_ref[0, 0] = st_f
        s_ref[0, 1] = st_b

    @pl.when(jnp.logical_not(in_range))
    def _():
        _hgrn_token_scan(*fwd, rev=False, tb=tb)
        _hgrn_token_scan(*bwd, rev=True, tb=tb)


def _hgrn(zq, zf, zi, lb_logits, s0, *, layer, tb):
    b, t, _ = zq.shape
    nb = t // tb
    fwd = lambda bi, i: (bi, i, 0)
    bwd = lambda bi, i: (bi, nb - 1 - i, 0)
    in_specs = [pl.BlockSpec((1, tb, HGW), fwd),
                pl.BlockSpec((1, 1, tb, HGW), lambda bi, i: (bi, 0, i, 0)),
                pl.BlockSpec((1, tb, HGW), fwd),
                pl.BlockSpec((1, tb, HGW), bwd),
                pl.BlockSpec((1, 1, tb, HGW), lambda bi, i: (bi, 1, nb - 1 - i, 0)),
                pl.BlockSpec((1, tb, HGW), bwd),
                pl.BlockSpec(lb_logits.shape, lambda bi, i: (0, 0, 0)),
                pl.BlockSpec((1, 2, HGW, HGW), lambda bi, i: (bi, 0, 0, 0))]
    out_specs = (pl.BlockSpec((1, tb, HGW), fwd),
                 pl.BlockSpec((1, tb, HGW), bwd),
                 pl.BlockSpec((1, 2, HGW, HGW), lambda bi, i: (bi, 0, 0, 0)))
    out_shape = (jax.ShapeDtypeStruct((b, t, HGW), F32),
                 jax.ShapeDtypeStruct((b, t, HGW), F32),
                 jax.ShapeDtypeStruct((b, 2, HGW, HGW), F32))
    nc = tb // CHUNK
    of, ob, s_fin = pl.pallas_call(
        functools.partial(_hgrn_kernel, tb=tb, layer=layer),
        grid=(b, nb), in_specs=in_specs, out_specs=out_specs, out_shape=out_shape,
        scratch_shapes=[pltpu.VMEM((nc, HGW, HGW), BF16), pltpu.VMEM((nc, HGW, HGW), F32),
                        pltpu.VMEM((nc, HGW, HGW), BF16), pltpu.VMEM((nc, HGW, HGW), F32)],
        compiler_params=pltpu.CompilerParams(
            dimension_semantics=("parallel", "arbitrary"), vmem_limit_bytes=VMEM_LIMIT),
        name="hgrn",
    )(zq, zf, zi, zq, zf, zi, lb_logits, s0)
    return of, ob, s_fin


def _merge_kernel(h_ref, u_ref, mod_ref, n2_ref, oda_ref, of_ref, ob_ref, hg_ref,
                  zp_ref, zpp_ref, zpn_ref, hgn_ref, pw_ref, ps_ref,
                  wg_ref, wpa_ref, wph_ref, wpp_ref, wo_ref, w1_ref, w3_ref, w2_ref, fn_ref,
                  out_ref, *, tm, t_total, final):
    i = pl.program_id(1)
    h1 = h_ref[0]
    m = mod_ref[0]

    x = zp_ref[0]
    prev = jnp.where(i > 0, zpp_ref[0], 0.0)
    nxt = jnp.where(i < pl.num_programs(1) - 1, zpn_ref[0], 0.0)
    e = jnp.concatenate([prev, x, nxt], axis=0)
    n = tm + 16
    a2 = e[0:n - 1] + e[1:n]
    a4 = a2[0:n - 3] + a2[2:n - 1]
    a8 = a4[0:n - 7] + a4[4:n - 3]
    a16 = a8[0:n - 15] + a8[8:n - 7]
    sums = (a2[7:7 + tm], a4[6:6 + tm], a8[4:4 + tm], a16[0:tm])
    group = lax.broadcasted_iota(jnp.int32, (1, PW), 1) // 64
    total, half = sums[-1], jnp.full((1, PW), POOL_WINDOWS[-1] // 2, jnp.int32)
    for g in range(len(POOL_WINDOWS) - 2, -1, -1):
        total = jnp.where(group == g, sums[g], total)
        half = jnp.where(group == g, POOL_WINDOWS[g] // 2, half)
    mean = total * (0.5 / half.astype(F32))

    def clipped(rows, r0):
        pos = i * tm + r0 + lax.broadcasted_iota(jnp.int32, (8, PW), 0)
        cnt = jnp.minimum(pos + half, t_total) - jnp.maximum(pos - half, 0)
        return rows / cnt.astype(F32)

    mean = jnp.concatenate([clipped(total[0:8], 0), mean[8:tm - 8], clipped(total[tm - 8:tm], tm - 8)], axis=0)
    o_pool = _dot((mean - x).astype(BF16), pw_ref[...]) * ps_ref[...]

    o = of_ref[0] + ob_ref[0]
    o2 = o * o
    hi = o2.astype(BF16)
    lo = (o2 - hi.astype(F32)).astype(BF16)
    ones_bd = (lax.broadcasted_iota(jnp.int32, (HGW, HGW), 0) // HGH
               == lax.broadcasted_iota(jnp.int32, (HGW, HGW), 1) // HGH).astype(BF16)
    seg = _dot(hi, ones_bd) + _dot(lo, ones_bd)
    zg = hg_ref[0]
    o_hg = (o * lax.rsqrt(seg * (1.0 / HGH) + EPS) * hgn_ref[...]) * (zg * _sigmoid(zg))

    gate = _sigmoid(_dot(u_ref[0], wg_ref[0]))
    y = (gate[:, 0:D] * _dot(oda_ref[0], wpa_ref[0])
         + gate[:, D:2 * D] * _dot(o_hg.astype(BF16), wph_ref[0])
         + gate[:, 2 * D:3 * D] * _dot(o_pool.astype(BF16), wpp_ref[0]))
    mix = _dot_row_groups(y.astype(BF16), wo_ref[0])
    h2 = h1 + m[5:6] * mix
    h3 = _swiglu_half(h2, m, 6, n2_ref[...], w1_ref, w3_ref, w2_ref)
    if final:
        h3 = _rms(h3, fn_ref[...])
    out_ref[0] = h3


def _merge(h1, u, mod, n2, oda, of, ob, hg, zp, hgn, pw, ps, wg, wpa, wph, wpp, wo, w1, w3, w2, fn,
           *, layer, tm, final):
    b, t, _ = h1.shape
    nblk8 = t // 8
    r = tm // 8
    tok = lambda w: pl.BlockSpec((1, tm, w), lambda bi, i: (bi, i, 0))
    in_specs = [tok(D), tok(D),
                pl.BlockSpec((1, NMOD, D), lambda bi, i: (bi, 0, 0)),
                _const_spec((1, D)),
                tok(QKW),
                tok(HGW), tok(HGW),
                tok(HGW), tok(PW),
                pl.BlockSpec((1, 8, PW), lambda bi, i: (bi, jnp.maximum(i * r - 1, 0), 0)),
                pl.BlockSpec((1, 8, PW), lambda bi, i: (bi, jnp.minimum((i + 1) * r, nblk8 - 1), 0)),
                _const_spec((1, HGW)), _const_spec((PW, PW)), _const_spec((1, PW)),
                _layer_spec((D, 3 * D), layer, 1), _layer_spec((QKW, D), layer), _layer_spec((HGW, D), layer),
                _layer_spec((PW, D), layer), _layer_spec((D, D), layer),
                _layer_spec((D, DFF), layer), _layer_spec((D, DFF), layer), _layer_spec((DFF, D), layer),
                _const_spec((1, D))]
    return pl.pallas_call(
        functools.partial(_merge_kernel, tm=tm, t_total=t, final=final),
        grid=(b, t // tm), in_specs=in_specs, out_specs=tok(D),
        out_shape=jax.ShapeDtypeStruct((b, t, D), F32),
        compiler_params=pltpu.CompilerParams(
            dimension_semantics=("parallel", "parallel"), vmem_limit_bytes=VMEM_LIMIT),
        name="merge_ffn",
    )(h1, u, mod, n2, oda, of, ob, hg, zp, zp, zp, hgn, pw, ps, wg, wpa, wph, wpp, wo, w1, w3, w2, fn)


def _rope_tables(n):
    rows = n // GRID_WIDTH
    row = jnp.repeat(jnp.arange(rows, dtype=jnp.int32), GRID_WIDTH).astype(F32)
    col = jnp.tile(jnp.arange(GRID_WIDTH, dtype=jnp.int32), rows).astype(F32)
    axis_dim = HALF // 2
    inv_freq = THETA ** (-jnp.arange(0, axis_dim, 2, dtype=F32) / axis_dim)
    ang_r = row[:, None] * inv_freq[None, :]
    ang_c = col[:, None] * inv_freq[None, :]
    cos64 = jnp.concatenate([jnp.cos(ang_r)] * 2 + [jnp.cos(ang_c)] * 2, axis=1)
    sin64 = jnp.concatenate([-jnp.sin(ang_r), jnp.sin(ang_r), -jnp.sin(ang_c), jnp.sin(ang_c)], axis=1)
    return jnp.tile(cos64, (1, 2)), jnp.tile(sin64, (1, 2))


def _block_diag(w):
    g, a, b = w.shape
    out = jnp.zeros((g * a, g * b), w.dtype)
    for i in range(g):
        out = out.at[i * a:(i + 1) * a, i * b:(i + 1) * b].set(w[i])
    return out


def kernel(x, c, ctx, c_ctx, w_ada, b_ada, norm_ffn1, norm_mix, norm_ffn2, ffn1_w1, ffn1_w3, ffn1_w2,
           ffn2_w1, ffn2_w3, ffn2_w2, w_in, da_lambda_q1, da_lambda_k1, da_lambda_q2, da_lambda_k2,
           da_subln, hg_lb_logits, hg_norm, pool_w, pool_scale, w_proj_da, w_proj_hg, w_proj_pool,
           w_out, final_norm):
    bsz, n, _ = x.shape
    n_ctx = ctx.shape[1]
    depth = w_ada.shape[0]
    lat, cx = _tiles(n), _tiles(n_ctx)

    cvec = jnp.zeros((8, D), F32).at[0:bsz].set(c).at[bsz].set(c_ctx)
    mods = _ada(cvec, w_ada, b_ada).reshape(depth, 8, NMOD, D)
    rc, rs = _rope_tables(n)
    rc_c = jnp.zeros((n_ctx, 128), F32)
    row = lambda v: v.reshape(1, -1)
    bf = lambda w: w.astype(BF16)

    ffn1 = (bf(ffn1_w1), bf(ffn1_w3), bf(ffn1_w2))
    ffn2 = (bf(ffn2_w1), bf(ffn2_w3), bf(ffn2_w2))
    w_in_b = bf(w_in)
    proj = (bf(w_proj_da), bf(w_proj_hg), bf(w_proj_pool), bf(w_out))

    h, hc = x, ctx
    for l in range(depth):
        need_ctx = l < depth - 1
        lam_init = 0.8 - 0.6 * math.exp(-0.3 * l)
        m_lat = mods[l, 0:bsz]
        m_ctx = jnp.broadcast_to(mods[l, bsz][None], (bsz, NMOD, D))
        lamv = jnp.stack([da_lambda_q1[l], da_lambda_k1[l], da_lambda_q2[l], da_lambda_k2[l]]).astype(F32)
        gain_da = da_subln[l].reshape(VDIM, 1)
        hgn = row(jnp.tile(hg_norm[l], HEADS))
        pw_bd = bf(_block_diag(pool_w[l]))
        merge_w = (hgn, pw_bd, row(pool_scale[l]), w_in_b, *proj, *ffn2, row(final_norm))

        (hc1, u_c, qt_c, k_c, vt_c, hq_c, hf_c, hi_c, hg_c, zp_c) = _ffn_in(
            hc, m_ctx, row(norm_ffn1[l]), row(norm_mix[l]), *ffn1, w_in_b, rc_c, rc_c,
            layer=l, rope=False, tm=cx.tm)
        s0 = jnp.zeros((bsz, 2, HGW, HGW), F32)
        of_c, ob_c, s_ctx = _hgrn(hq_c, hf_c, hi_c, hg_lb_logits, s0, layer=l, tb=cx.tb)

        (h1, u, qt, k, vt, hq, hf, hi, hg, zp) = _ffn_in(
            h, m_lat, row(norm_ffn1[l]), row(norm_mix[l]), *ffn1, w_in_b, rc, rs,
            layer=l, rope=True, tm=lat.tm)
        o_da = _attention(qt, k, vt, k_c, vt_c, lamv, gain_da, lam_init=lam_init, tq=lat.tq, tk=lat.tk)
        of, ob, _ = _hgrn(hq, hf, hi, hg_lb_logits, s_ctx, layer=l, tb=lat.tb)
        h = _merge(h1, u, m_lat, row(norm_ffn2[l]), o_da, of, ob, hg, zp, *merge_w,
                   layer=l, tm=lat.tm, final=not need_ctx)
        if need_ctx:
            o_dac = _attention(qt_c, k_c, vt_c, None, None, lamv, gain_da, lam_init=lam_init,
                               tq=cx.tq, tk=cx.tk)
            hc = _merge(hc1, u_c, m_ctx, row(norm_ffn2[l]), o_dac, of_c, ob_c, hg_c, zp_c,
                        *merge_w, layer=l, tm=cx.tm, final=False)
    return h
```
